```python
import jax, jax.numpy as jnp
from jax import lax
import numpy as np

D_MODEL = 2048
BATCH = 8
SEQ = 4096
DEPTH = 2

GRID_W = 64
HEAD_DIM = 128
N_HEADS = 16
N_KV = 4
GROUP = N_HEADS // N_KV
Q_WIDTH = N_HEADS * HEAD_DIM
KV_WIDTH = N_KV * HEAD_DIM
AXIS_DIM = HEAD_DIM // 2
ROPE_THETA = 10000.0
Q_BLOCK = 128
N_FOURIER_GROUPS = 4
FOURIER_GROUP_DIM = 256
F_WIDTH = N_FOURIER_GROUPS * FOURIER_GROUP_DIM
IN_WIDTH = Q_WIDTH + 2 * KV_WIDTH + F_WIDTH
N_BRANCHES = 2
MEM_TOKENS = 256
X_HEADS = 4
X_HEAD_DIM = 128
X_WIDTH = X_HEADS * X_HEAD_DIM
D_FF = 5632
CONV_WIDTH = 3
EPS = 1e-6

kernel_name = "hybrid_gqa_fourier_memxattn_convffn_encoder"


def rms_norm(x, g):
    xf = x.astype(jnp.float32)
    y = xf * lax.rsqrt(jnp.mean(xf * xf, axis=-1, keepdims=True) + EPS)
    return (y * g.astype(jnp.float32)).astype(x.dtype)


def axial_rope_angles(seq_len):
    rows = seq_len // GRID_W
    row_ids = jnp.repeat(jnp.arange(rows), GRID_W).astype(jnp.float32)
    col_ids = jnp.tile(jnp.arange(GRID_W), rows).astype(jnp.float32)
    inv_freq = 1.0 / (ROPE_THETA ** (jnp.arange(0, AXIS_DIM, 2, dtype=jnp.float32) / AXIS_DIM))
    ang = jnp.stack([row_ids[:, None] * inv_freq, col_ids[:, None] * inv_freq], axis=1)
    return jnp.cos(ang), jnp.sin(ang)


def apply_axial_rope(x, cos, sin):
    b, s, h, _ = x.shape
    xf = x.astype(jnp.float32).reshape(b, s, h, 2, AXIS_DIM)
    x1, x2 = xf[..., : AXIS_DIM // 2], xf[..., AXIS_DIM // 2:]
    c = cos[None, :, None, :, :]
    sn = sin[None, :, None, :, :]
    out = jnp.concatenate([x1 * c - x2 * sn, x2 * c + x1 * sn], axis=-1)
    return out.reshape(b, s, h, HEAD_DIM).astype(x.dtype)


def blocked_gqa(q, k, v):
    b, s = q.shape[0], q.shape[1]
    nb = s // Q_BLOCK
    qb = q.reshape(b, nb, Q_BLOCK, N_KV, GROUP, HEAD_DIM).transpose(1, 0, 2, 3, 4, 5)
    scale = HEAD_DIM ** -0.5

    def one_block(qi):
        sc = jnp.einsum('bqhgd,bkhd->bhgqk', qi, k, preferred_element_type=jnp.float32) * scale
        p = jax.nn.softmax(sc, axis=-1).astype(v.dtype)
        return jnp.einsum('bhgqk,bkhd->bqhgd', p, v)

    ob = lax.map(one_block, qb)
    return ob.transpose(1, 0, 2, 3, 4, 5).reshape(b, s, Q_WIDTH)


def fourier_mix(u):
    b, s = u.shape[0], u.shape[1]
    ug = u.astype(jnp.float32).reshape(b, s, N_FOURIER_GROUPS, FOURIER_GROUP_DIM)
    y = jnp.fft.fft2(ug, axes=(1, 3), norm='ortho').real
    return y.reshape(b, s, F_WIDTH).astype(u.dtype)


def token_mixer(h, cos, sin, w_in, q_norm_g, k_norm_g, w_attn_o, w_four_o, w_gate, b_gate, w_mix_o):
    b, s = h.shape[0], h.shape[1]
    proj = h @ w_in
    q = proj[..., :Q_WIDTH].reshape(b, s, N_HEADS, HEAD_DIM)
    k = proj[..., Q_WIDTH:Q_WIDTH + KV_WIDTH].reshape(b, s, N_KV, HEAD_DIM)
    v = proj[..., Q_WIDTH + KV_WIDTH:Q_WIDTH + 2 * KV_WIDTH].reshape(b, s, N_KV, HEAD_DIM)
    uf = proj[..., Q_WIDTH + 2 * KV_WIDTH:]
    q = apply_axial_rope(rms_norm(q, q_norm_g), cos, sin)
    k = apply_axial_rope(rms_norm(k, k_norm_g), cos, sin)
    a = blocked_gqa(q, k, v) @ w_attn_o
    f = fourier_mix(uf) @ w_four_o
    g = jax.nn.sigmoid(h @ w_gate + b_gate)
    merged = g[..., :D_MODEL] * a + g[..., D_MODEL:] * f
    return merged @ w_mix_o


def memory_xattn(h, mem_n, w_xq, w_xkv, w_xo):
    b, s = h.shape[0], h.shape[1]
    m = mem_n.shape[1]
    q = (h @ w_xq).reshape(b, s, X_HEADS, X_HEAD_DIM)
    kv = (mem_n @ w_xkv).reshape(b, m, 2, X_HEADS, X_HEAD_DIM)
    k, v = kv[:, :, 0], kv[:, :, 1]
    sc = jnp.einsum('bqhd,bkhd->bhqk', q, k, preferred_element_type=jnp.float32) * (X_HEAD_DIM ** -0.5)
    p = jax.nn.softmax(sc, axis=-1).astype(v.dtype)
    o = jnp.einsum('bhqk,bkhd->bqhd', p, v).reshape(b, s, X_WIDTH)
    return o @ w_xo


def conv_ffn(h, w_up, conv_w, conv_b, w_down):
    u = h @ w_up
    up = jnp.pad(u, ((0, 0), (1, 1), (0, 0)))
    u = up[:, :-2] * conv_w[0] + up[:, 1:-1] * conv_w[1] + up[:, 2:] * conv_w[2] + conv_b
    gate, val = u[..., :D_FF], u[..., D_FF:]
    return (jax.nn.gelu(gate, approximate=True) * val) @ w_down


def setup_inputs(seed: int = 0) -> dict:
    key = jax.random.key(seed)
    ks = jax.random.split(key, 24)
    L = DEPTH

    def dense(k, shape, fan_in):
        return jax.random.normal(k, shape, jnp.float32) * (fan_in ** -0.5)

    def gain(k, shape):
        return 1.0 + 0.02 * jax.random.normal(k, shape, jnp.float32)

    return {
        'x': jax.random.normal(ks[0], (BATCH, SEQ, D_MODEL), jnp.float32),
        'mem': jax.random.normal(ks[1], (BATCH, MEM_TOKENS, D_MODEL), jnp.float32),
        'mix_pre_g': gain(ks[2], (L, D_MODEL)),
        'w_in': dense(ks[3], (L, D_MODEL, IN_WIDTH), D_MODEL),
        'q_norm_g': gain(ks[4], (L, HEAD_DIM)),
        'k_norm_g': gain(ks[5], (L, HEAD_DIM)),
        'w_attn_o': dense(ks[6], (L, Q_WIDTH, D_MODEL), Q_WIDTH),
        'w_four_o': dense(ks[7], (L, F_WIDTH, D_MODEL), F_WIDTH),
        'w_gate': dense(ks[8], (L, D_MODEL, N_BRANCHES * D_MODEL), D_MODEL),
        'b_gate': 0.01 * jax.random.normal(ks[9], (L, N_BRANCHES * D_MODEL), jnp.float32),
        'w_mix_o': dense(ks[10], (L, D_MODEL, D_MODEL), D_MODEL),
        'mix_post_g': gain(ks[11], (L, D_MODEL)),
        'xa_pre_g': gain(ks[12], (L, D_MODEL)),
        'mem_norm_g': gain(ks[13], (L, D_MODEL)),
        'w_xq': dense(ks[14], (L, D_MODEL, X_WIDTH), D_MODEL),
        'w_xkv': dense(ks[15], (L, D_MODEL, 2 * X_WIDTH), D_MODEL),
        'w_xo': dense(ks[16], (L, X_WIDTH, D_MODEL), X_WIDTH),
        'xa_post_g': gain(ks[17], (L, D_MODEL)),
        'ffn_pre_g': gain(ks[18], (L, D_MODEL)),
        'w_up': dense(ks[19], (L, D_MODEL, 2 * D_FF), D_MODEL),
        'conv_w': dense(ks[20], (L, CONV_WIDTH, 2 * D_FF), CONV_WIDTH),
        'conv_b': 0.01 * jax.random.normal(ks[21], (L, 2 * D_FF), jnp.float32),
        'w_down': dense(ks[22], (L, D_FF, D_MODEL), D_FF),
        'ffn_post_g': gain(ks[23], (L, D_MODEL)),
    }


def reference(x, mem, mix_pre_g, w_in, q_norm_g, k_norm_g, w_attn_o, w_four_o, w_gate, b_gate,
              w_mix_o, mix_post_g, xa_pre_g, mem_norm_g, w_xq, w_xkv, w_xo, xa_post_g,
              ffn_pre_g, w_up, conv_w, conv_b, w_down, ffn_post_g):
    seq_len = x.shape[1]
    cos, sin = axial_rope_angles(seq_len)
    for l in range(DEPTH):
        h = rms_norm(x, mix_pre_g[l])
        y = token_mixer(h, cos, sin, w_in[l], q_norm_g[l], k_norm_g[l], w_attn_o[l],
                        w_four_o[l], w_gate[l], b_gate[l], w_mix_o[l])
        x = x + rms_norm(y, mix_post_g[l])

        h = rms_norm(x, xa_pre_g[l])
        mem_n = rms_norm(mem, mem_norm_g[l])
        y = memory_xattn(h, mem_n, w_xq[l], w_xkv[l], w_xo[l])
        x = x + rms_norm(y, xa_post_g[l])

        h = rms_norm(x, ffn_pre_g[l])
        y = conv_ffn(h, w_up[l], conv_w[l], conv_b[l], w_down[l])
        x = x + rms_norm(y, ffn_post_g[l])
    return x
```

```python
import functools
import math

import jax
import jax.numpy as jnp
from jax import lax
from jax.experimental import pallas as pl
from jax.experimental.pallas import tpu as pltpu

HEAD_DIM = 128
N_KV = 4
GROUP = 4
GRID_W = 64
ROPE_THETA = 10000.0
N_FOURIER_GROUPS = 4
X_HEADS = 4
X_HEAD_DIM = 128
EPS = 1e-6

V7X_VMEM_LIMIT_BYTES = 56 * 1024 * 1024
BF16_SUBLANES = 16

F32 = jnp.float32
BF16 = jnp.bfloat16


def _params(*sem):
    return pltpu.CompilerParams(dimension_semantics=sem, vmem_limit_bytes=V7X_VMEM_LIMIT_BYTES)


def _rms(xf, g):
    ms = jnp.mean(xf * xf, axis=-1, keepdims=True)
    return xf * lax.rsqrt(ms + EPS) * g


def _dot(a, b):
    return jnp.dot(a, b, preferred_element_type=F32)


def _dot_nt(a, b):
    return lax.dot_general(a, b, (((1,), (1,)), ((), ())), preferred_element_type=F32)


def _mixer_in_kernel(x_ref, g_ref, w_ref, hg_ref, b_ref, cos_ref, sin_ref, o_ref, hn_ref,
                     *, n_qk_tiles, gate_start):
    j = pl.program_id(1)

    @pl.when(j == 0)
    def _():
        hn_ref[...] = _rms(x_ref[...], g_ref[...]).astype(BF16)

    z = _dot(hn_ref[...], w_ref[...])
    tn = z.shape[1]

    @pl.when(j < n_qk_tiles)
    def _():
        c = cos_ref[...]
        s = sin_ref[...]
        lane = lax.broadcasted_iota(jnp.int32, (1, HEAD_DIM), 1)
        first_half = (lane % (HEAD_DIM // 2)) < (HEAD_DIM // 4)
        for h in range(tn // HEAD_DIM):
            sl = slice(h * HEAD_DIM, (h + 1) * HEAD_DIM)
            y = _rms(z[:, sl], hg_ref[:, sl])
            rot = jnp.where(first_half,
                            pltpu.roll(y, HEAD_DIM - HEAD_DIM // 4, 1),
                            pltpu.roll(y, HEAD_DIM // 4, 1))
            o_ref[:, sl] = (y * c + rot * s).astype(BF16)

    @pl.when(jnp.logical_and(j >= n_qk_tiles, j < gate_start))
    def _():
        o_ref[...] = z.astype(BF16)

    @pl.when(j >= gate_start)
    def _():
        o_ref[...] = jax.nn.sigmoid(z + b_ref[...]).astype(BF16)


def _mixer_in(x, g_pre, w_cat, head_gain, b_gate, cos_t, sin_t, *, seq, qk_width, gate_col, tm, tn):
    t, d = x.shape
    n = w_cat.shape[1]
    n_qk_tiles = qk_width // tn
    gate_start = gate_col // tn
    seq_tiles = seq // tm
    kern = functools.partial(_mixer_in_kernel, n_qk_tiles=n_qk_tiles, gate_start=gate_start)
    return pl.pallas_call(
        kern,
        grid=(t // tm, n // tn),
        in_specs=[
            pl.BlockSpec((tm, d), lambda i, j: (i, 0)),
            pl.BlockSpec((1, d), lambda i, j: (0, 0)),
            pl.BlockSpec((d, tn), lambda i, j: (0, j)),
            pl.BlockSpec((1, tn), lambda i, j: (0, jnp.minimum(j, n_qk_tiles - 1))),
            pl.BlockSpec((1, tn), lambda i, j: (0, jnp.maximum(j - gate_start, 0))),
            pl.BlockSpec((tm, HEAD_DIM), lambda i, j: (i % seq_tiles, 0)),
            pl.BlockSpec((tm, HEAD_DIM), lambda i, j: (i % seq_tiles, 0)),
        ],
        out_specs=pl.BlockSpec((tm, tn), lambda i, j: (i, j)),
        out_shape=jax.ShapeDtypeStruct((t, n), BF16),
        scratch_shapes=[pltpu.VMEM((tm, d), BF16)],
        compiler_params=_params("parallel", "arbitrary"),
        name="mixer_in",
    )(x, g_pre, w_cat, head_gain, b_gate, cos_t, sin_t)


def _attn_kernel(q_ref, k_ref, v_ref, o_ref, *, kc):
    tq = q_ref.shape[0]
    s_len = k_ref.shape[0]
    q = jnp.concatenate(
        [q_ref[:, g * HEAD_DIM:(g + 1) * HEAD_DIM] for g in range(GROUP)], axis=0)
    rows = GROUP * tq
    m = jnp.full((rows, 1), -jnp.inf, F32)
    l = jnp.zeros((rows, 1), F32)
    acc = jnp.zeros((rows, HEAD_DIM), F32)
    for c in range(s_len // kc):
        kk = k_ref[c * kc:(c + 1) * kc, :]
        vv = v_ref[c * kc:(c + 1) * kc, :]
        s = _dot_nt(q, kk)
        m_new = jnp.maximum(m, jnp.max(s, axis=-1, keepdims=True))
        alpha = jnp.exp(m - m_new)
        p = jnp.exp(s - m_new)
        l = alpha * l + jnp.sum(p, axis=-1, keepdims=True)
        acc = alpha * acc + _dot(p.astype(BF16), vv)
        m = m_new
    o = acc * (1.0 / l)
    for g in range(GROUP):
        o_ref[:, g * HEAD_DIM:(g + 1) * HEAD_DIM] = o[g * tq:(g + 1) * tq].astype(BF16)


def _attention(proj, *, batch, seq, q_width, kv_width, tq, kc):
    t = proj.shape[0]
    gw = GROUP * HEAD_DIM
    k_blk0 = q_width // HEAD_DIM
    v_blk0 = (q_width + kv_width) // HEAD_DIM
    nq = seq // tq
    return pl.pallas_call(
        functools.partial(_attn_kernel, kc=kc),
        grid=(batch, N_KV, nq),
        in_specs=[
            pl.BlockSpec((tq, gw), lambda b, h, i: (b * nq + i, h)),
            pl.BlockSpec((seq, HEAD_DIM), lambda b, h, i: (b, k_blk0 + h)),
            pl.BlockSpec((seq, HEAD_DIM), lambda b, h, i: (b, v_blk0 + h)),
        ],
        out_specs=pl.BlockSpec((tq, gw), lambda b, h, i: (b * nq + i, h)),
        out_shape=jax.ShapeDtypeStruct((t, q_width), BF16),
        compiler_params=_params("parallel", "parallel", "arbitrary"),
        name="attention",
    )(proj, proj, proj)


def _four_chan_kernel(u_ref, dft_ref, o_ref, *, gdim):
    for g in range(u_ref.shape[1] // gdim):
        sl = slice(g * gdim, (g + 1) * gdim)
        r = _dot(u_ref[:, sl], dft_ref[...])
        o_ref[0, 0, :, sl] = r[:, :gdim].astype(BF16)
        o_ref[0, 1, :, sl] = r[:, gdim:].astype(BF16)


def _four_chan(proj, dft_c, *, batch, seq, u_col, f_width, tm):
    gdim = f_width // N_FOURIER_GROUPS
    ns = seq // tm
    return pl.pallas_call(
        functools.partial(_four_chan_kernel, gdim=gdim),
        grid=(batch, ns),
        in_specs=[
            pl.BlockSpec((tm, f_width), lambda b, i: (b * ns + i, u_col // f_width)),
            pl.BlockSpec((gdim, 2 * gdim), lambda b, i: (0, 0)),
        ],
        out_specs=pl.BlockSpec((1, 2, tm, f_width), lambda b, i: (b, 0, i, 0)),
        out_shape=jax.ShapeDtypeStruct((batch, 2, seq, f_width), BF16),
        compiler_params=_params("parallel", "parallel"),
        name="fourier_chan",
    )(proj, dft_c)


def _four_seq_kernel(a_ref, u_ref, o_ref):
    o_ref[...] = _dot(a_ref[...], u_ref[0]).astype(BF16)


def _four_seq(dft_s, ucs, *, batch, seq, f_width, tm, tn):
    ni = seq // tm
    return pl.pallas_call(
        _four_seq_kernel,
        grid=(ni, batch, f_width // tn),
        in_specs=[
            pl.BlockSpec((tm, 2 * seq), lambda i, b, n: (i, 0)),
            pl.BlockSpec((1, 2 * seq, tn), lambda i, b, n: (b, 0, n)),
        ],
        out_specs=pl.BlockSpec((tm, tn), lambda i, b, n: (b * ni + i, n)),
        out_shape=jax.ShapeDtypeStruct((batch * seq, f_width), BF16),
        compiler_params=_params("parallel", "parallel", "parallel"),
        name="fourier_seq",
    )(dft_s, ucs)


def _merge_kernel(att_ref, y_ref, g1_ref, g2_ref, x_ref, wa_ref, wf_ref, wm_ref, gp_ref, o_ref):
    a = _dot(att_ref[...], wa_ref[...])
    f = _dot(y_ref[...], wf_ref[...])
    merged = g1_ref[...].astype(F32) * a + g2_ref[...].astype(F32) * f
    y = _dot(merged.astype(BF16), wm_ref[...])
    o_ref[...] = x_ref[...] + _rms(y, gp_ref[...])


def _resident(shape):
    return pl.BlockSpec(shape, lambda *_: (0,) * len(shape), pipeline_mode=pl.Buffered(1))


def _merge(att, four, proj, x, wa, wf, wm, g_post, *, gate_col, tm):
    t, d = x.shape
    return pl.pallas_call(
        _merge_kernel,
        grid=(t // tm,),
        in_specs=[
            pl.BlockSpec((tm, att.shape[1]), lambda i: (i, 0)),
            pl.BlockSpec((tm, four.shape[1]), lambda i: (i, 0)),
            pl.BlockSpec((tm, d), lambda i: (i, gate_col // d)),
            pl.BlockSpec((tm, d), lambda i: (i, gate_col // d + 1)),
            pl.BlockSpec((tm, d), lambda i: (i, 0)),
            _resident(wa.shape),
            _resident(wf.shape),
            _resident(wm.shape),
            _resident(g_post.shape),
        ],
        out_specs=pl.BlockSpec((tm, d), lambda i: (i, 0)),
        out_shape=jax.ShapeDtypeStruct((t, d), F32),
        compiler_params=_params("parallel"),
        name="merge",
    )(att, four, proj, proj, x, wa, wf, wm, g_post)


def _norm_matmul_kernel(x_ref, g_ref, w_ref, o_ref):
    hn = _rms(x_ref[...], g_ref[...]).astype(BF16)
    o_ref[...] = _dot(hn, w_ref[...]).astype(BF16)


def _norm_matmul(x, g, w, *, tm):
    t, d = x.shape
    n = w.shape[1]
    return pl.pallas_call(
        _norm_matmul_kernel,
        grid=(t // tm,),
        in_specs=[
            pl.BlockSpec((tm, d), lambda i: (i, 0)),
            _resident(g.shape),
            _resident(w.shape),
        ],
        out_specs=pl.BlockSpec((tm, n), lambda i: (i, 0)),
        out_shape=jax.ShapeDtypeStruct((t, n), BF16),
        compiler_params=_params("parallel"),
        name="mem_kv",
    )(x, g, w)


def _xattn_kernel(x_ref, gpre_ref, wq_ref, k_ref, v_ref, wo_ref, gpost_ref, o_ref):
    x = x_ref[...]
    hn = _rms(x, gpre_ref[...]).astype(BF16)
    q = (_dot(hn, wq_ref[...]) * (X_HEAD_DIM ** -0.5)).astype(BF16)
    outs = []
    for h in range(X_HEADS):
        sl = slice(h * X_HEAD_DIM, (h + 1) * X_HEAD_DIM)
        s = _dot_nt(q[:, sl], k_ref[:, sl])
        m = jnp.max(s, axis=-1, keepdims=True)
        p = jnp.exp(s - m)
        l = jnp.sum(p, axis=-1, keepdims=True)
        outs.append(_dot(p.astype(BF16), v_ref[:, sl]) * (1.0 / l))
    o = jnp.concatenate(outs, axis=1).astype(BF16)
    y = _dot(o, wo_ref[...])
    o_ref[...] = x + _rms(y, gpost_ref[...])


def _xattn(x, g_pre, wq, kv, wo, g_post, *, batch, seq, mem_tokens, tm):
    t, d = x.shape
    xw = wq.shape[1]
    ns = seq // tm
    return pl.pallas_call(
        _xattn_kernel,
        grid=(batch, ns),
        in_specs=[
            pl.BlockSpec((tm, d), lambda b, i: (b * ns + i, 0)),
            _resident(g_pre.shape),
            _resident(wq.shape),
            pl.BlockSpec((mem_tokens, xw), lambda b, i: (b, 0)),
            pl.BlockSpec((mem_tokens, xw), lambda b, i: (b, 1)),
            _resident(wo.shape),
            _resident(g_post.shape),
        ],
        out_specs=pl.BlockSpec((tm, d), lambda b, i: (b * ns + i, 0)),
        out_shape=jax.ShapeDtypeStruct((t, d), F32),
        compiler_params=_params("parallel", "parallel"),
        name="xattn",
    )(x, g_pre, wq, kv, kv, wo, g_post)


def _gelu_tanh(x):
    return 0.5 * x * (1.0 + jnp.tanh(math.sqrt(2.0 / math.pi) * (x + 0.044715 * (x * x * x))))


def _ffn_kernel(xp_ref, x_ref, xn_ref, gpre_ref, wg_ref, wv_ref, cwg_ref, cwv_ref, cbg_ref, cbv_ref,
                wd_ref, gpost_ref, o_ref, hn_ref, acc_ref, ug_ref, uv_ref, *, tiles_per_seq):
    i = pl.program_id(0)
    j = pl.program_id(1)
    tm = x_ref.shape[0]
    halo = xp_ref.shape[0]

    @pl.when(j == 0)
    def _():
        g = gpre_ref[...]
        pos = i % tiles_per_seq
        hp = jnp.where(pos == 0, 0.0, _rms(xp_ref[...], g))
        hx = jnp.where(pos == tiles_per_seq - 1, 0.0, _rms(xn_ref[...], g))
        hn_ref[0:halo, :] = hp.astype(BF16)
        hn_ref[halo:halo + tm, :] = _rms(x_ref[...], g).astype(BF16)
        hn_ref[halo + tm:, :] = hx.astype(BF16)
        acc_ref[...] = jnp.zeros_like(acc_ref)

    hn = hn_ref[...]
    ug_ref[...] = _dot(hn, wg_ref[...])
    uv_ref[...] = _dot(hn, wv_ref[...])

    def conv(u_ref, cw_ref, cb_ref):
        return (u_ref[halo - 1:halo - 1 + tm, :] * cw_ref[0:1, :]
                + u_ref[halo:halo + tm, :] * cw_ref[1:2, :]
                + u_ref[halo + 1:halo + 1 + tm, :] * cw_ref[2:3, :]
                + cb_ref[...])

    act = _gelu_tanh(conv(ug_ref, cwg_ref, cbg_ref)) * conv(uv_ref, cwv_ref, cbv_ref)
    acc_ref[...] += _dot(act.astype(BF16), wd_ref[...])

    @pl.when(j == pl.num_programs(1) - 1)
    def _():
        o_ref[...] = x_ref[...] + _rms(acc_ref[...], gpost_ref[...])


def _ffn(x, g_pre, w_up, conv_w, conv_b, w_down, g_post, *, seq, tm, tf):
    t, d = x.shape
    d_ff = w_down.shape[0]
    nf = d_ff // tf
    halo = BF16_SUBLANES
    hb = tm // halo
    n_hblk = t // halo
    kern = functools.partial(_ffn_kernel, tiles_per_seq=seq // tm)
    return pl.pallas_call(
        kern,
        grid=(t // tm, nf),
        in_specs=[
            pl.BlockSpec((halo, d), lambda i, j: (jnp.maximum(i * hb - 1, 0), 0)),
            pl.BlockSpec((tm, d), lambda i, j: (i, 0)),
            pl.BlockSpec((halo, d), lambda i, j: (jnp.minimum((i + 1) * hb, n_hblk - 1), 0)),
            pl.BlockSpec((1, d), lambda i, j: (0, 0)),
            pl.BlockSpec((d, tf), lambda i, j: (0, j)),
            pl.BlockSpec((d, tf), lambda i, j: (0, nf + j)),
            pl.BlockSpec((3, tf), lambda i, j: (0, j)),
            pl.BlockSpec((3, tf), lambda i, j: (0, nf + j)),
            pl.BlockSpec((1, tf), lambda i, j: (0, j)),
            pl.BlockSpec((1, tf), lambda i, j: (0, nf + j)),
            pl.BlockSpec((tf, d), lambda i, j: (j, 0)),
            pl.BlockSpec((1, d), lambda i, j: (0, 0)),
        ],
        out_specs=pl.BlockSpec((tm, d), lambda i, j: (i, 0)),
        out_shape=jax.ShapeDtypeStruct((t, d), F32),
        scratch_shapes=[
            pltpu.VMEM((tm + 2 * halo, d), BF16),
            pltpu.VMEM((tm, d), F32),
            pltpu.VMEM((tm + 2 * halo, tf), F32),
            pltpu.VMEM((tm + 2 * halo, tf), F32),
        ],
        compiler_params=_params("parallel", "arbitrary"),
        name="conv_ffn",
    )(x, x, x, g_pre, w_up, w_up, conv_w, conv_w, conv_b, conv_b, w_down, g_post)


def _rope_tables(seq):
    half = HEAD_DIM // 2
    pos = jnp.arange(seq)
    row = (pos // GRID_W).astype(F32)
    col = (pos % GRID_W).astype(F32)
    inv_freq = 1.0 / (ROPE_THETA ** (jnp.arange(0, half, 2, dtype=F32) / half))
    lane = jnp.arange(HEAD_DIM)
    freq = inv_freq[(lane % half) % (half // 2)]
    ids = jnp.where(lane[None, :] < half, row[:, None], col[:, None])
    ang = ids * freq[None, :]
    sign = jnp.where((lane % half) < half // 2, -1.0, 1.0)
    return jnp.cos(ang), jnp.sin(ang) * sign[None, :]


def _dft_tables(seq, gdim):
    def cs(n):
        k = jnp.arange(n, dtype=jnp.int32)
        r = (k[:, None] * k[None, :]) % n
        ang = r.astype(F32) * (2.0 * math.pi / n)
        return jnp.cos(ang) * (n ** -0.5), jnp.sin(ang) * (n ** -0.5)
    cc, sc = cs(gdim)
    cq, sq = cs(seq)
    return (jnp.concatenate([cc, sc], axis=1).astype(BF16),
            jnp.concatenate([cq, -sq], axis=1).astype(BF16))


def _tiles(seq, d_ff):
    def pick(n, pref):
        return pref if n % pref == 0 else n
    tf = next(c for c in (512, 256, 128) if d_ff % c == 0)
    return dict(
        mixer_tm=pick(seq, 1024), mixer_tn=512,
        attn_tq=pick(seq, 256), attn_kc=pick(seq, 1024),
        four_tm=pick(seq, 512), four_tn=512,
        merge_tm=pick(seq, 256),
        xattn_tm=pick(seq, 512),
        ffn_tm=pick(seq, 512), ffn_tf=tf,
    )


def kernel(x, mem, mix_pre_g, w_in, q_norm_g, k_norm_g, w_attn_o, w_four_o, w_gate, b_gate, w_mix_o,
           mix_post_g, xa_pre_g, mem_norm_g, w_xq, w_xkv, w_xo, xa_post_g, ffn_pre_g, w_up, conv_w,
           conv_b, w_down, ffn_post_g):
    batch, seq, d = x.shape
    mem_tokens = mem.shape[1]
    depth = w_in.shape[0]
    q_width = w_attn_o.shape[1]
    f_width = w_four_o.shape[1]
    kv_width = (w_in.shape[2] - q_width - f_width) // 2
    n_heads = q_width // HEAD_DIM
    assert n_heads == N_KV * GROUP and kv_width == N_KV * HEAD_DIM
    assert seq % GRID_W == 0 and w_xq.shape[2] == X_HEADS * X_HEAD_DIM
    d_ff = w_down.shape[1]
    ts = _tiles(seq, d_ff)
    in_width = w_in.shape[2]
    qk_width = q_width + kv_width
    u_col = q_width + 2 * kv_width
    assert qk_width % ts["mixer_tn"] == 0 and in_width % ts["mixer_tn"] == 0
    assert u_col % f_width == 0 and in_width % d == 0

    cos_t, sin_t = _rope_tables(seq)
    dft_c, dft_s = _dft_tables(seq, f_width // N_FOURIER_GROUPS)

    row = lambda v: v.reshape(1, -1)
    xf = x.reshape(batch * seq, d)
    memf = mem.reshape(batch * mem_tokens, d)

    for l in range(depth):
        w_cat = jnp.concatenate([w_in[l], w_gate[l]], axis=1).astype(BF16)
        head_gain = jnp.concatenate([jnp.tile(q_norm_g[l] * (HEAD_DIM ** -0.5), n_heads),
                                     jnp.tile(k_norm_g[l], N_KV)])
        proj = _mixer_in(xf, row(mix_pre_g[l]), w_cat, row(head_gain), row(b_gate[l]), cos_t, sin_t,
                         seq=seq, qk_width=qk_width, gate_col=in_width,
                         tm=ts["mixer_tm"], tn=ts["mixer_tn"])
        att = _attention(proj, batch=batch, seq=seq, q_width=q_width, kv_width=kv_width,
                         tq=ts["attn_tq"], kc=ts["attn_kc"])
        ucs = _four_chan(proj, dft_c, batch=batch, seq=seq, u_col=u_col, f_width=f_width,
                         tm=ts["four_tm"])
        four = _four_seq(dft_s, ucs.reshape(batch, 2 * seq, f_width), batch=batch, seq=seq,
                         f_width=f_width, tm=ts["four_tm"], tn=min(ts["four_tn"], f_width))
        xf = _merge(att, four, proj, xf, w_attn_o[l].astype(BF16), w_four_o[l].astype(BF16),
                    w_mix_o[l].astype(BF16), row(mix_post_g[l]), gate_col=in_width, tm=ts["merge_tm"])

        kv = _norm_matmul(memf, row(mem_norm_g[l]), w_xkv[l].astype(BF16), tm=mem_tokens)
        xf = _xattn(xf, row(xa_pre_g[l]), w_xq[l].astype(BF16), kv, w_xo[l].astype(BF16),
                    row(xa_post_g[l]), batch=batch, seq=seq, mem_tokens=mem_tokens, tm=ts["xattn_tm"])

        xf = _ffn(xf, row(ffn_pre_g[l]), w_up[l].astype(BF16), conv_w[l], row(conv_b[l]),
                  w_down[l].astype(BF16), row(ffn_post_g[l]), seq=seq, tm=ts["ffn_tm"], tf=ts["ffn_tf"])

    return xf.reshape(batch, seq, d)
```

```python
import functools
import math

import jax
import jax.numpy as jnp
from jax import lax
from jax.experimental import pallas as pl
from jax.experimental.pallas import tpu as pltpu

HEAD_DIM = 128
N_KV = 4
GROUP = 4
GRID_W = 64
ROPE_THETA = 10000.0
N_FOURIER_GROUPS = 4
X_HEADS = 4
X_HEAD_DIM = 128
EPS = 1e-6

V7X_VMEM_LIMIT_BYTES = 56 * 1024 * 1024
BF16_SUBLANES = 16

F32 = jnp.float32
BF16 = jnp.bfloat16


def _params(*sem):
    return pltpu.CompilerParams(dimension_semantics=sem, vmem_limit_bytes=V7X_VMEM_LIMIT_BYTES)


def _rms(xf, g):
    ms = jnp.mean(xf * xf, axis=-1, keepdims=True)
    return xf * lax.rsqrt(ms + EPS) * g


def _dot(a, b):
    return jnp.dot(a, b, preferred_element_type=F32)


def _dot_nt(a, b):
    return lax.dot_general(a, b, (((1,), (1,)), ((), ())), preferred_element_type=F32)


def _epilogue_plain(z, o_ref):
    o_ref[...] = z.astype(BF16)


def _epilogue_gate(z, b_ref, o_ref):
    o_ref[...] = (0.5 * jnp.tanh(0.5 * (z + b_ref[...])) + 0.5).astype(BF16)


def _proj_qk_t_kernel(x_ref, g_ref, wt_ref, cg_ref, sg_ref, o_ref, hn_ref):
    @pl.when(pl.program_id(1) == 0)
    def _():
        hn_ref[...] = _rms(x_ref[...], g_ref[...]).astype(BF16)

    zt = _dot_nt(wt_ref[...], hn_ref[...])
    cg = cg_ref[...]
    sg = sg_ref[...]
    half = HEAD_DIM // 2
    for h in range(zt.shape[0] // HEAD_DIM):
        y = zt[h * HEAD_DIM:(h + 1) * HEAD_DIM, :]
        y = y * lax.rsqrt(jnp.mean(y * y, axis=0, keepdims=True) + EPS)
        rot = jnp.concatenate([y[half:], y[:half]], axis=0)
        o_ref[h * HEAD_DIM:(h + 1) * HEAD_DIM, :] = (y * cg + rot * sg).astype(BF16)


def _proj_qk_t(x, g_pre, w_t, cg, sg, *, seq, n_q_tiles, tm, tr):
    t, d = x.shape
    n = w_t.shape[0]
    seq_tiles = seq // tm
    table_spec = pl.BlockSpec(
        (HEAD_DIM, tm), lambda i, j: (jnp.where(j < n_q_tiles, 0, 1), i % seq_tiles))
    return pl.pallas_call(
        _proj_qk_t_kernel,
        grid=(t // tm, n // tr),
        in_specs=[
            pl.BlockSpec((tm, d), lambda i, j: (i, 0)),
            pl.BlockSpec((1, d), lambda i, j: (0, 0)),
            pl.BlockSpec((tr, d), lambda i, j: (j, 0)),
            table_spec,
            table_spec,
        ],
        out_specs=pl.BlockSpec((tr, tm), lambda i, j: (j, i)),
        out_shape=jax.ShapeDtypeStruct((n, t), BF16),
        scratch_shapes=[pltpu.VMEM((tm, d), BF16)],
        compiler_params=_params("parallel", "arbitrary"),
        name="proj_qk",
    )(x, g_pre, w_t, cg, sg)


def _proj_kernel(x_ref, g_ref, w_ref, *rest, epilogue):
    hn_ref = rest[-1]

    @pl.when(pl.program_id(1) == 0)
    def _():
        hn_ref[...] = _rms(x_ref[...], g_ref[...]).astype(BF16)

    epilogue(_dot(hn_ref[...], w_ref[...]), *rest[:-1])


def _proj(x, g_pre, w, epilogue, extra, extra_specs, *, tm, tn, name):
    t, d = x.shape
    n = w.shape[1]
    return pl.pallas_call(
        functools.partial(_proj_kernel, epilogue=epilogue),
        grid=(t // tm, n // tn),
        in_specs=[
            pl.BlockSpec((tm, d), lambda i, j: (i, 0)),
            pl.BlockSpec((1, d), lambda i, j: (0, 0)),
            pl.BlockSpec((d, tn), lambda i, j: (0, j)),
            *extra_specs,
        ],
        out_specs=pl.BlockSpec((tm, tn), lambda i, j: (i, j)),
        out_shape=jax.ShapeDtypeStruct((t, n), BF16),
        scratch_shapes=[pltpu.VMEM((tm, d), BF16)],
        compiler_params=_params("parallel", "arbitrary"),
        name=name,
    )(x, g_pre, w, *extra)


def _attn_kernel(qt_ref, kt_ref, v_ref, o_ref, k_ref, vt_ref, *, kc):
    tq = qt_ref.shape[1]
    s_len = v_ref.shape[0]
    n_chunks = s_len // kc

    @pl.when(pl.program_id(2) == 0)
    def _():
        for c in range(n_chunks):
            ks = slice(c * kc, (c + 1) * kc)
            k_ref[ks, :] = kt_ref[:, ks].T
            vt_ref[:, ks] = v_ref[ks, :].T

    qt = jnp.concatenate(
        [qt_ref[g * HEAD_DIM:(g + 1) * HEAD_DIM, :] for g in range(GROUP)], axis=1)
    cols = GROUP * tq
    m = jnp.full((1, cols), -jnp.inf, F32)
    l = jnp.zeros((1, cols), F32)
    acc = jnp.zeros((HEAD_DIM, cols), F32)
    for c in range(n_chunks):
        ks = slice(c * kc, (c + 1) * kc)
        st = _dot(k_ref[ks, :], qt)
        m_new = jnp.maximum(m, jnp.max(st, axis=0, keepdims=True))
        alpha = jnp.exp2(m - m_new)
        p = jnp.exp2(st - m_new)
        l = alpha * l + jnp.sum(p, axis=0, keepdims=True)
        acc = alpha * acc + _dot(vt_ref[:, ks], p.astype(BF16))
        m = m_new
    o = (acc * (1.0 / l)).T
    for g in range(GROUP):
        o_ref[:, g * HEAD_DIM:(g + 1) * HEAD_DIM] = o[g * tq:(g + 1) * tq].astype(BF16)


def _attention(qk_t, vu, *, batch, seq, q_width, v_col, tq, kc):
    t = vu.shape[0]
    gw = GROUP * HEAD_DIM
    k_blk0 = q_width // HEAD_DIM
    v_blk0 = v_col // HEAD_DIM
    nq = seq // tq
    return pl.pallas_call(
        functools.partial(_attn_kernel, kc=kc),
        grid=(batch, N_KV, nq),
        in_specs=[
            pl.BlockSpec((gw, tq), lambda b, h, i: (h, b * nq + i)),
            pl.BlockSpec((HEAD_DIM, seq), lambda b, h, i: (k_blk0 + h, b)),
            pl.BlockSpec((seq, HEAD_DIM), lambda b, h, i: (b, v_blk0 + h)),
        ],
        out_specs=pl.BlockSpec((tq, gw), lambda b, h, i: (b * nq + i, h)),
        out_shape=jax.ShapeDtypeStruct((t, q_width), BF16),
        scratch_shapes=[pltpu.VMEM((seq, HEAD_DIM), BF16), pltpu.VMEM((HEAD_DIM, seq), BF16)],
        compiler_params=_params("parallel", "parallel", "arbitrary"),
        name="attention",
    )(qk_t, qk_t, vu)


def _four_chan_kernel(u_ref, dft_ref, o_ref, *, gdim):
    for g in range(u_ref.shape[1] // gdim):
        sl = slice(g * gdim, (g + 1) * gdim)
        r = _dot(u_ref[:, sl], dft_ref[...])
        o_ref[0, 0, :, sl] = r[:, :gdim].astype(BF16)
        o_ref[0, 1, :, sl] = r[:, gdim:].astype(BF16)


def _four_chan(uv, dft_c, *, batch, seq, f_width, tm):
    gdim = f_width // N_FOURIER_GROUPS
    ns = seq // tm
    return pl.pallas_call(
        functools.partial(_four_chan_kernel, gdim=gdim),
        grid=(batch, ns),
        in_specs=[
            pl.BlockSpec((tm, f_width), lambda b, i: (b * ns + i, 0)),
            pl.BlockSpec((gdim, 2 * gdim), lambda b, i: (0, 0)),
        ],
        out_specs=pl.BlockSpec((1, 2, tm, f_width), lambda b, i: (b, 0, i, 0)),
        out_shape=jax.ShapeDtypeStruct((batch, 2, seq, f_width), BF16),
        compiler_params=_params("parallel", "parallel"),
        name="fourier_chan",
    )(uv, dft_c)


def _four_seq_kernel(a_ref, u_ref, o_ref):
    o_ref[...] = _dot(a_ref[...], u_ref[0]).astype(BF16)


def _four_seq(dft_s, ucs, *, batch, seq, f_width, tm, tn):
    ni = seq // tm
    return pl.pallas_call(
        _four_seq_kernel,
        grid=(ni, batch, f_width // tn),
        in_specs=[
            pl.BlockSpec((tm, 2 * seq), lambda i, b, n: (i, 0)),
            pl.BlockSpec((1, 2 * seq, tn), lambda i, b, n: (b, 0, n)),
        ],
        out_specs=pl.BlockSpec((tm, tn), lambda i, b, n: (b * ni + i, n)),
        out_shape=jax.ShapeDtypeStruct((batch * seq, f_width), BF16),
        compiler_params=_params("parallel", "parallel", "parallel"),
        name="fourier_seq",
    )(dft_s, ucs)


def _merge_kernel(att_ref, y_ref, g1_ref, g2_ref, x_ref, wa_ref, wf_ref, wm_ref, gp_ref, o_ref):
    a = _dot(att_ref[...], wa_ref[...])
    f = _dot(y_ref[...], wf_ref[...])
    merged = g1_ref[...].astype(F32) * a + g2_ref[...].astype(F32) * f
    y = _dot(merged.astype(BF16), wm_ref[...])
    o_ref[...] = x_ref[...] + _rms(y, gp_ref[...])


def _resident(shape):
    return pl.BlockSpec(shape, lambda *_: (0,) * len(shape), pipeline_mode=pl.Buffered(1))


def _merge(att, four, gates, x, wa, wf, wm, g_post, *, tm):
    t, d = x.shape
    return pl.pallas_call(
        _merge_kernel,
        grid=(t // tm,),
        in_specs=[
            pl.BlockSpec((tm, att.shape[1]), lambda i: (i, 0)),
            pl.BlockSpec((tm, four.shape[1]), lambda i: (i, 0)),
            pl.BlockSpec((tm, d), lambda i: (i, 0)),
            pl.BlockSpec((tm, d), lambda i: (i, 1)),
            pl.BlockSpec((tm, d), lambda i: (i, 0)),
            _resident(wa.shape),
            _resident(wf.shape),
            _resident(wm.shape),
            _resident(g_post.shape),
        ],
        out_specs=pl.BlockSpec((tm, d), lambda i: (i, 0)),
        out_shape=jax.ShapeDtypeStruct((t, d), F32),
        compiler_params=_params("parallel"),
        name="merge",
    )(att, four, gates, gates, x, wa, wf, wm, g_post)


def _norm_matmul_kernel(x_ref, g_ref, w_ref, o_ref):
    hn = _rms(x_ref[...], g_ref[...]).astype(BF16)
    o_ref[...] = _dot(hn, w_ref[...]).astype(BF16)


def _norm_matmul(x, g, w, *, tm):
    t, d = x.shape
    n = w.shape[1]
    return pl.pallas_call(
        _norm_matmul_kernel,
        grid=(t // tm,),
        in_specs=[
            pl.BlockSpec((tm, d), lambda i: (i, 0)),
            _resident(g.shape),
            _resident(w.shape),
        ],
        out_specs=pl.BlockSpec((tm, n), lambda i: (i, 0)),
        out_shape=jax.ShapeDtypeStruct((t, n), BF16),
        compiler_params=_params("parallel"),
        name="mem_kv",
    )(x, g, w)


def _xattn_kernel(x_ref, gpre_ref, wq_ref, k_ref, v_ref, wo_ref, gpost_ref, o_ref):
    x = x_ref[...]
    hn = _rms(x, gpre_ref[...]).astype(BF16)
    q = (_dot(hn, wq_ref[...]) * (X_HEAD_DIM ** -0.5)).astype(BF16)
    outs = []
    for h in range(X_HEADS):
        sl = slice(h * X_HEAD_DIM, (h + 1) * X_HEAD_DIM)
        s = _dot_nt(q[:, sl], k_ref[:, sl])
        m = jnp.max(s, axis=-1, keepdims=True)
        p = jnp.exp(s - m)
        l = jnp.sum(p, axis=-1, keepdims=True)
        outs.append(_dot(p.astype(BF16), v_ref[:, sl]) * (1.0 / l))
    o = jnp.concatenate(outs, axis=1).astype(BF16)
    y = _dot(o, wo_ref[...])
    o_ref[...] = x + _rms(y, gpost_ref[...])


def _xattn(x, g_pre, wq, kv, wo, g_post, *, batch, seq, mem_tokens, tm):
    t, d = x.shape
    xw = wq.shape[1]
    ns = seq // tm
    return pl.pallas_call(
        _xattn_kernel,
        grid=(batch, ns),
        in_specs=[
            pl.BlockSpec((tm, d), lambda b, i: (b * ns + i, 0)),
            _resident(g_pre.shape),
            _resident(wq.shape),
            pl.BlockSpec((mem_tokens, xw), lambda b, i: (b, 0)),
            pl.BlockSpec((mem_tokens, xw), lambda b, i: (b, 1)),
            _resident(wo.shape),
            _resident(g_post.shape),
        ],
        out_specs=pl.BlockSpec((tm, d), lambda b, i: (b * ns + i, 0)),
        out_shape=jax.ShapeDtypeStruct((t, d), F32),
        compiler_params=_params("parallel", "parallel"),
        name="xattn",
    )(x, g_pre, wq, kv, kv, wo, g_post)


def _gelu_tanh(x):
    return 0.5 * x * (1.0 + jnp.tanh(math.sqrt(2.0 / math.pi) * (x + 0.044715 * (x * x * x))))


def _ffn_kernel(xp_ref, x_ref, xn_ref, gpre_ref, wg_ref, wv_ref, cwg_ref, cwv_ref, cbg_ref, cbv_ref,
                wd_ref, gpost_ref, o_ref, hn_ref, acc_ref, ug_ref, uv_ref, *, tiles_per_seq):
    i = pl.program_id(0)
    j = pl.program_id(1)
    tm = x_ref.shape[0]
    halo = xp_ref.shape[0]

    @pl.when(j == 0)
    def _():
        g = gpre_ref[...]
        pos = i % tiles_per_seq
        hp = jnp.where(pos == 0, 0.0, _rms(xp_ref[...], g))
        hx = jnp.where(pos == tiles_per_seq - 1, 0.0, _rms(xn_ref[...], g))
        hn_ref[0:halo, :] = hp.astype(BF16)
        hn_ref[halo:halo + tm, :] = _rms(x_ref[...], g).astype(BF16)
        hn_ref[halo + tm:, :] = hx.astype(BF16)
        acc_ref[...] = jnp.zeros_like(acc_ref)

    hn = hn_ref[...]
    ug_ref[...] = _dot(hn, wg_ref[...])
    uv_ref[...] = _dot(hn, wv_ref[...])

    def conv(u_ref, cw_ref, cb_ref):
        return (u_ref[halo - 1:halo - 1 + tm, :] * cw_ref[0:1, :]
                + u_ref[halo:halo + tm, :] * cw_ref[1:2, :]
                + u_ref[halo + 1:halo + 1 + tm, :] * cw_ref[2:3, :]
                + cb_ref[...])

    act = _gelu_tanh(conv(ug_ref, cwg_ref, cbg_ref)) * conv(uv_ref, cwv_ref, cbv_ref)
    acc_ref[...] += _dot(act.astype(BF16), wd_ref[...])

    @pl.when(j == pl.num_programs(1) - 1)
    def _():
        o_ref[...] = x_ref[...] + _rms(acc_ref[...], gpost_ref[...])


def _ffn(x, g_pre, w_up, conv_w, conv_b, w_down, g_post, *, seq, tm, tf):
    t, d = x.shape
    d_ff = w_down.shape[0]
    nf = d_ff // tf
    halo = BF16_SUBLANES
    hb = tm // halo
    n_hblk = t // halo
    kern = functools.partial(_ffn_kernel, tiles_per_seq=seq // tm)
    return pl.pallas_call(
        kern,
        grid=(t // tm, nf),
        in_specs=[
            pl.BlockSpec((halo, d), lambda i, j: (jnp.maximum(i * hb - 1, 0), 0)),
            pl.BlockSpec((tm, d), lambda i, j: (i, 0)),
            pl.BlockSpec((halo, d), lambda i, j: (jnp.minimum((i + 1) * hb, n_hblk - 1), 0)),
            pl.BlockSpec((1, d), lambda i, j: (0, 0)),
            pl.BlockSpec((d, tf), lambda i, j: (0, j)),
            pl.BlockSpec((d, tf), lambda i, j: (0, nf + j)),
            pl.BlockSpec((3, tf), lambda i, j: (0, j)),
            pl.BlockSpec((3, tf), lambda i, j: (0, nf + j)),
            pl.BlockSpec((1, tf), lambda i, j: (0, j)),
            pl.BlockSpec((1, tf), lambda i, j: (0, nf + j)),
            pl.BlockSpec((tf, d), lambda i, j: (j, 0)),
            pl.BlockSpec((1, d), lambda i, j: (0, 0)),
        ],
        out_specs=pl.BlockSpec((tm, d), lambda i, j: (i, 0)),
        out_shape=jax.ShapeDtypeStruct((t, d), F32),
        scratch_shapes=[
            pltpu.VMEM((tm + 2 * halo, d), BF16),
            pltpu.VMEM((tm, d), F32),
            pltpu.VMEM((tm + 2 * halo, tf), F32),
            pltpu.VMEM((tm + 2 * halo, tf), F32),
        ],
        compiler_params=_params("parallel", "arbitrary"),
        name="conv_ffn",
    )(x, x, x, g_pre, w_up, w_up, conv_w, conv_w, conv_b, conv_b, w_down, g_post)


def _head_lane_order():
    quarter = HEAD_DIM // 4
    blocks = (0, 2, 1, 3)
    return jnp.concatenate([jnp.arange(b * quarter, (b + 1) * quarter) for b in blocks])


def _rope_tables(seq):
    half = HEAD_DIM // 2
    pos = jnp.arange(seq)
    row = (pos // GRID_W).astype(F32)
    col = (pos % GRID_W).astype(F32)
    inv_freq = 1.0 / (ROPE_THETA ** (jnp.arange(0, half, 2, dtype=F32) / half))
    lane = _head_lane_order()
    freq = inv_freq[(lane % half) % (half // 2)]
    ids = jnp.where(lane[None, :] < half, row[:, None], col[:, None])
    ang = ids * freq[None, :]
    sign = jnp.where((lane % half) < half // 2, -1.0, 1.0)
    return jnp.cos(ang), jnp.sin(ang) * sign[None, :]


def _dft_tables(seq, gdim):
    def cs(n):
        k = jnp.arange(n, dtype=jnp.int32)
        r = (k[:, None] * k[None, :]) % n
        ang = r.astype(F32) * (2.0 * math.pi / n)
        return jnp.cos(ang) * (n ** -0.5), jnp.sin(ang) * (n ** -0.5)
    cc, sc = cs(gdim)
    cq, sq = cs(seq)
    return (jnp.concatenate([cc, sc], axis=1).astype(BF16),
            jnp.concatenate([cq, -sq], axis=1).astype(BF16))


def _tiles(seq, d_ff):
    def pick(n, pref):
        return pref if n % pref == 0 else n
    tf = next(c for c in (512, 256, 128) if d_ff % c == 0)
    return dict(
        mixer_tm=pick(seq, 1024), mixer_tn=512,
        attn_tq=pick(seq, 256), attn_kc=pick(seq, 1024),
        four_tm=pick(seq, 512), four_tn=512,
        merge_tm=pick(seq, 256),
        xattn_tm=pick(seq, 512),
        ffn_tm=pick(seq, 512), ffn_tf=tf,
    )


def kernel(x, mem, mix_pre_g, w_in, q_norm_g, k_norm_g, w_attn_o, w_four_o, w_gate, b_gate, w_mix_o,
           mix_post_g, xa_pre_g, mem_norm_g, w_xq, w_xkv, w_xo, xa_post_g, ffn_pre_g, w_up, conv_w,
           conv_b, w_down, ffn_post_g):
    batch, seq, d = x.shape
    mem_tokens = mem.shape[1]
    depth = w_in.shape[0]
    q_width = w_attn_o.shape[1]
    f_width = w_four_o.shape[1]
    kv_width = (w_in.shape[2] - q_width - f_width) // 2
    n_heads = q_width // HEAD_DIM
    assert n_heads == N_KV * GROUP and kv_width == N_KV * HEAD_DIM
    assert seq % GRID_W == 0 and w_xq.shape[2] == X_HEADS * X_HEAD_DIM
    d_ff = w_down.shape[1]
    ts = _tiles(seq, d_ff)
    qk_width = q_width + kv_width
    u_col = q_width + 2 * kv_width
    tm, tn = ts["mixer_tm"], ts["mixer_tn"]
    assert qk_width % tn == 0 and (kv_width + f_width) % tn == 0 and w_gate.shape[2] % tn == 0

    cos_t, sin_t = _rope_tables(seq)
    dft_c, dft_s = _dft_tables(seq, f_width // N_FOURIER_GROUPS)
    order = _head_lane_order()
    qk_cols = (jnp.arange(qk_width // HEAD_DIM)[:, None] * HEAD_DIM + order[None, :]).reshape(-1)
    partner = jnp.roll(jnp.arange(HEAD_DIM), HEAD_DIM // 2)
    col_spec = pl.BlockSpec((1, tn), lambda i, j: (0, j))

    row = lambda v: v.reshape(1, -1)
    xf = x.reshape(batch * seq, d)
    memf = mem.reshape(batch * mem_tokens, d)

    for l in range(depth):
        w_qk_t = w_in[l][:, qk_cols].T.astype(BF16)
        w_uv = jnp.concatenate([w_in[l][:, u_col:], w_in[l][:, qk_width:u_col]], axis=1).astype(BF16)
        q_gain = q_norm_g[l][order] * (HEAD_DIM ** -0.5 * math.log2(math.e))
        k_gain = k_norm_g[l][order]
        cg = jnp.concatenate([g[:, None] * cos_t.T for g in (q_gain, k_gain)], axis=0)
        sg = jnp.concatenate([g[partner][:, None] * sin_t.T for g in (q_gain, k_gain)], axis=0)
        g_pre = row(mix_pre_g[l])
        qk = _proj_qk_t(xf, g_pre, w_qk_t, cg, sg, seq=seq, n_q_tiles=q_width // tn, tm=tm, tr=tn)
        uv = _proj(xf, g_pre, w_uv, _epilogue_plain, (), (), tm=tm, tn=tn, name="proj_uv")
        gates = _proj(xf, g_pre, w_gate[l].astype(BF16), _epilogue_gate, (row(b_gate[l]),),
                      (col_spec,), tm=tm, tn=tn, name="proj_gate")
        att = _attention(qk, uv, batch=batch, seq=seq, q_width=q_width, v_col=f_width,
                         tq=ts["attn_tq"], kc=ts["attn_kc"])
        ucs = _four_chan(uv, dft_c, batch=batch, seq=seq, f_width=f_width, tm=ts["four_tm"])
        four = _four_seq(dft_s, ucs.reshape(batch, 2 * seq, f_width), batch=batch, seq=seq,
                         f_width=f_width, tm=ts["four_tm"], tn=min(ts["four_tn"], f_width))
        xf = _merge(att, four, gates, xf, w_attn_o[l].astype(BF16), w_four_o[l].astype(BF16),
                    w_mix_o[l].astype(BF16), row(mix_post_g[l]), tm=ts["merge_tm"])

        kv = _norm_matmul(memf, row(mem_norm_g[l]), w_xkv[l].astype(BF16), tm=mem_tokens)
        xf = _xattn(xf, row(xa_pre_g[l]), w_xq[l].astype(BF16), kv, w_xo[l].astype(BF16),
                    row(xa_post_g[l]), batch=batch, seq=seq, mem_tokens=mem_tokens, tm=ts["xattn_tm"])

        xf = _ffn(xf, row(ffn_pre_g[l]), w_up[l].astype(BF16), conv_w[l], row(conv_b[l]),
                  w_down[l].astype(BF16), row(ffn_post_g[l]), seq=seq, tm=ts["ffn_tm"], tf=ts["ffn_tf"])

    return xf.reshape(batch, seq, d)
```

```python
import functools
import math

import jax
import jax.numpy as jnp
from jax import lax
from jax.experimental import pallas as pl
from jax.experimental.pallas import tpu as pltpu

HEAD_DIM = 128
N_KV = 4
GROUP = 4
GRID_W = 64
ROPE_THETA = 10000.0
N_FOURIER_GROUPS = 4
X_HEADS = 4
X_HEAD_DIM = 128
EPS = 1e-6

V7X_VMEM_LIMIT_BYTES = 56 * 1024 * 1024
BF16_SUBLANES = 16
ATTN_MIN_DENOMINATOR = 2.0 ** -60

F32 = jnp.float32
BF16 = jnp.bfloat16


def _params(*sem):
    return pltpu.CompilerParams(dimension_semantics=sem, vmem_limit_bytes=V7X_VMEM_LIMIT_BYTES)


def _rms(xf, g):
    ms = jnp.mean(xf * xf, axis=-1, keepdims=True)
    return xf * lax.rsqrt(ms + EPS) * g


def _dot(a, b):
    return jnp.dot(a, b, preferred_element_type=F32)


def _dot_nt(a, b):
    return lax.dot_general(a, b, (((1,), (1,)), ((), ())), preferred_element_type=F32)


def _epilogue_plain(z, o_ref):
    o_ref[...] = z.astype(BF16)


def _epilogue_gate(z, b_ref, o_ref):
    o_ref[...] = (0.5 * jnp.tanh(0.5 * (z + b_ref[...])) + 0.5).astype(BF16)


def _proj_qk_t_kernel(x_ref, g_ref, wt_ref, cg_ref, sg_ref, o_ref, hn_ref):
    @pl.when(pl.program_id(1) == 0)
    def _():
        hn_ref[...] = _rms(x_ref[...], g_ref[...]).astype(BF16)

    zt = _dot_nt(wt_ref[...], hn_ref[...])
    cg = cg_ref[...]
    sg = sg_ref[...]
    half = HEAD_DIM // 2
    for h in range(zt.shape[0] // HEAD_DIM):
        y = zt[h * HEAD_DIM:(h + 1) * HEAD_DIM, :]
        y = y * lax.rsqrt(jnp.mean(y * y, axis=0, keepdims=True) + EPS)
        rot = jnp.concatenate([y[half:], y[:half]], axis=0)
        o_ref[h * HEAD_DIM:(h + 1) * HEAD_DIM, :] = (y * cg + rot * sg).astype(BF16)


def _proj_qk_t(x, g_pre, w_t, cg, sg, *, seq, n_q_tiles, tm, tr):
    t, d = x.shape
    n = w_t.shape[0]
    seq_tiles = seq // tm
    table_spec = pl.BlockSpec(
        (HEAD_DIM, tm), lambda i, j: (jnp.where(j < n_q_tiles, 0, 1), i % seq_tiles))
    return pl.pallas_call(
        _proj_qk_t_kernel,
        grid=(t // tm, n // tr),
        in_specs=[
            pl.BlockSpec((tm, d), lambda i, j: (i, 0)),
            pl.BlockSpec((1, d), lambda i, j: (0, 0)),
            pl.BlockSpec((tr, d), lambda i, j: (j, 0)),
            table_spec,
            table_spec,
        ],
        out_specs=pl.BlockSpec((tr, tm), lambda i, j: (j, i)),
        out_shape=jax.ShapeDtypeStruct((n, t), BF16),
        scratch_shapes=[pltpu.VMEM((tm, d), BF16)],
        compiler_params=_params("parallel", "arbitrary"),
        name="proj_qk",
    )(x, g_pre, w_t, cg, sg)


def _proj_kernel(x_ref, g_ref, w_ref, *rest, epilogue):
    hn_ref = rest[-1]

    @pl.when(pl.program_id(1) == 0)
    def _():
        hn_ref[...] = _rms(x_ref[...], g_ref[...]).astype(BF16)

    epilogue(_dot(hn_ref[...], w_ref[...]), *rest[:-1])


def _proj(x, g_pre, w, epilogue, extra, extra_specs, *, tm, tn, name):
    t, d = x.shape
    n = w.shape[1]
    return pl.pallas_call(
        functools.partial(_proj_kernel, epilogue=epilogue),
        grid=(t // tm, n // tn),
        in_specs=[
            pl.BlockSpec((tm, d), lambda i, j: (i, 0)),
            pl.BlockSpec((1, d), lambda i, j: (0, 0)),
            pl.BlockSpec((d, tn), lambda i, j: (0, j)),
            *extra_specs,
        ],
        out_specs=pl.BlockSpec((tm, tn), lambda i, j: (i, j)),
        out_shape=jax.ShapeDtypeStruct((t, n), BF16),
        scratch_shapes=[pltpu.VMEM((tm, d), BF16)],
        compiler_params=_params("parallel", "arbitrary"),
        name=name,
    )(x, g_pre, w, *extra)


def _attn_exact(q, ka_ref, va_ref, kc):
    rows = q.shape[0]
    m = jnp.full((rows, 1), -jnp.inf, F32)
    l = jnp.zeros((rows, 1), F32)
    acc = jnp.zeros((rows, HEAD_DIM), F32)
    for c in range(ka_ref.shape[0] // kc):
        ks = slice(c * kc, (c + 1) * kc)
        s = _dot_nt(q, ka_ref[ks, :HEAD_DIM])
        m_new = jnp.maximum(m, jnp.max(s, axis=-1, keepdims=True))
        alpha = jnp.exp2(m - m_new)
        p = jnp.exp2(s - m_new)
        l = alpha * l + jnp.sum(p, axis=-1, keepdims=True)
        acc = alpha * acc + _dot(p.astype(BF16), va_ref[ks, :HEAD_DIM])
        m = m_new
    return acc * (1.0 / l)


def _attn_kernel(qt_ref, kt_ref, v_ref, o_ref, ka_ref, va_ref, kmax_ref, *, kc):
    tq = qt_ref.shape[1]
    s_len = v_ref.shape[0]
    n_chunks = s_len // kc
    lane = lax.broadcasted_iota(jnp.int32, (1, HEAD_DIM), 1)

    @pl.when(pl.program_id(2) == 0)
    def _():
        ones_col = jnp.broadcast_to(jnp.where(lane == 0, 1.0, 0.0), (kc, HEAD_DIM)).astype(BF16)
        k2max = jnp.zeros((1, 1), F32)
        for c in range(n_chunks):
            ks = slice(c * kc, (c + 1) * kc)
            kt = kt_ref[:, ks]
            kf = kt.astype(F32)
            k2max = jnp.maximum(k2max, jnp.max(jnp.sum(kf * kf, axis=0, keepdims=True),
                                               axis=1, keepdims=True))
            ka_ref[ks, :HEAD_DIM] = kt.T
            ka_ref[ks, HEAD_DIM:] = ones_col
            va_ref[ks, :HEAD_DIM] = v_ref[ks, :]
            va_ref[ks, HEAD_DIM:] = ones_col
        kmax_ref[...] = jnp.broadcast_to(jnp.sqrt(k2max), kmax_ref.shape)

    q = jnp.concatenate(
        [qt_ref[g * HEAD_DIM:(g + 1) * HEAD_DIM, :].T for g in range(GROUP)], axis=0)
    qf = q.astype(F32)
    shift = jnp.sqrt(jnp.sum(qf * qf, axis=-1, keepdims=True)) * kmax_ref[0:1, 0:1]
    shift_col = (-shift) * jnp.where(lane == 0, 1.0, 0.0)
    q_aug = jnp.concatenate([q, shift_col.astype(BF16)], axis=1)
    acc = jnp.zeros((GROUP * tq, 2 * HEAD_DIM), F32)
    for c in range(n_chunks):
        ks = slice(c * kc, (c + 1) * kc)
        p = jnp.exp2(_dot_nt(q_aug, ka_ref[ks, :])).astype(BF16)
        acc = acc + _dot(p, va_ref[ks, :])
    l = acc[:, HEAD_DIM:HEAD_DIM + 1]
    accurate = jnp.min(l) >= ATTN_MIN_DENOMINATOR

    def write(o):
        for g in range(GROUP):
            o_ref[:, g * HEAD_DIM:(g + 1) * HEAD_DIM] = o[g * tq:(g + 1) * tq].astype(BF16)

    @pl.when(accurate)
    def _():
        write(acc[:, :HEAD_DIM] * (1.0 / l))

    @pl.when(jnp.logical_not(accurate))
    def _():
        write(_attn_exact(q, ka_ref, va_ref, kc))


def _attention(qk_t, vu, *, batch, seq, q_width, v_col, tq, kc):
    t = vu.shape[0]
    gw = GROUP * HEAD_DIM
    k_blk0 = q_width // HEAD_DIM
    v_blk0 = v_col // HEAD_DIM
    nq = seq // tq
    return pl.pallas_call(
        functools.partial(_attn_kernel, kc=kc),
        grid=(batch, N_KV, nq),
        in_specs=[
            pl.BlockSpec((gw, tq), lambda b, h, i: (h, b * nq + i)),
            pl.BlockSpec((HEAD_DIM, seq), lambda b, h, i: (k_blk0 + h, b)),
            pl.BlockSpec((seq, HEAD_DIM), lambda b, h, i: (b, v_blk0 + h)),
        ],
        out_specs=pl.BlockSpec((tq, gw), lambda b, h, i: (b * nq + i, h)),
        out_shape=jax.ShapeDtypeStruct((t, q_width), BF16),
        scratch_shapes=[
            pltpu.VMEM((seq, 2 * HEAD_DIM), BF16),
            pltpu.VMEM((seq, 2 * HEAD_DIM), BF16),
            pltpu.VMEM((8, HEAD_DIM), F32),
        ],
        compiler_params=_params("parallel", "parallel", "arbitrary"),
        name="attention",
    )(qk_t, qk_t, vu)


def _four_chan_kernel(u_ref, dft_ref, o_ref, *, gdim):
    for g in range(u_ref.shape[1] // gdim):
        sl = slice(g * gdim, (g + 1) * gdim)
        r = _dot(u_ref[:, sl], dft_ref[...])
        o_ref[0, 0, :, sl] = r[:, :gdim].astype(BF16)
        o_ref[0, 1, :, sl] = r[:, gdim:].astype(BF16)


def _four_chan(uv, dft_c, *, batch, seq, f_width, tm):
    gdim = f_width // N_FOURIER_GROUPS
    ns = seq // tm
    return pl.pallas_call(
        functools.partial(_four_chan_kernel, gdim=gdim),
        grid=(batch, ns),
        in_specs=[
            pl.BlockSpec((tm, f_width), lambda b, i: (b * ns + i, 0)),
            pl.BlockSpec((gdim, 2 * gdim), lambda b, i: (0, 0)),
        ],
        out_specs=pl.BlockSpec((1, 2, tm, f_width), lambda b, i: (b, 0, i, 0)),
        out_shape=jax.ShapeDtypeStruct((batch, 2, seq, f_width), BF16),
        compiler_params=_params("parallel", "parallel"),
        name="fourier_chan",
    )(uv, dft_c)


def _four_seq_kernel(a_ref, u_ref, o_ref):
    o_ref[...] = _dot(a_ref[...], u_ref[0]).astype(BF16)


def _four_seq(dft_s, ucs, *, batch, seq, f_width, tm, tn):
    ni = seq // tm
    return pl.pallas_call(
        _four_seq_kernel,
        grid=(ni, batch, f_width // tn),
        in_specs=[
            pl.BlockSpec((tm, 2 * seq), lambda i, b, n: (i, 0)),
            pl.BlockSpec((1, 2 * seq, tn), lambda i, b, n: (b, 0, n)),
        ],
        out_specs=pl.BlockSpec((tm, tn), lambda i, b, n: (b * ni + i, n)),
        out_shape=jax.ShapeDtypeStruct((batch * seq, f_width), BF16),
        compiler_params=_params("parallel", "parallel", "parallel"),
        name="fourier_seq",
    )(dft_s, ucs)


def _merge_kernel(att_ref, y_ref, g1_ref, g2_ref, x_ref, wa_ref, wf_ref, wm_ref, gp_ref, o_ref):
    a = _dot(att_ref[...], wa_ref[...])
    f = _dot(y_ref[...], wf_ref[...])
    merged = g1_ref[...].astype(F32) * a + g2_ref[...].astype(F32) * f
    y = _dot(merged.astype(BF16), wm_ref[...])
    o_ref[...] = x_ref[...] + _rms(y, gp_ref[...])


def _resident(shape):
    return pl.BlockSpec(shape, lambda *_: (0,) * len(shape), pipeline_mode=pl.Buffered(1))


def _merge(att, four, gates, x, wa, wf, wm, g_post, *, tm):
    t, d = x.shape
    return pl.pallas_call(
        _merge_kernel,
        grid=(t // tm,),
        in_specs=[
            pl.BlockSpec((tm, att.shape[1]), lambda i: (i, 0)),
            pl.BlockSpec((tm, four.shape[1]), lambda i: (i, 0)),
            pl.BlockSpec((tm, d), lambda i: (i, 0)),
            pl.BlockSpec((tm, d), lambda i: (i, 1)),
            pl.BlockSpec((tm, d), lambda i: (i, 0)),
            _resident(wa.shape),
            _resident(wf.shape),
            _resident(wm.shape),
            _resident(g_post.shape),
        ],
        out_specs=pl.BlockSpec((tm, d), lambda i: (i, 0)),
        out_shape=jax.ShapeDtypeStruct((t, d), F32),
        compiler_params=_params("parallel"),
        name="merge",
    )(att, four, gates, gates, x, wa, wf, wm, g_post)


def _norm_matmul_kernel(x_ref, g_ref, w_ref, o_ref):
    hn = _rms(x_ref[...], g_ref[...]).astype(BF16)
    o_ref[...] = _dot(hn, w_ref[...]).astype(BF16)


def _norm_matmul(x, g, w, *, tm):
    t, d = x.shape
    n = w.shape[1]
    return pl.pallas_call(
        _norm_matmul_kernel,
        grid=(t // tm,),
        in_specs=[
            pl.BlockSpec((tm, d), lambda i: (i, 0)),
            _resident(g.shape),
            _resident(w.shape),
        ],
        out_specs=pl.BlockSpec((tm, n), lambda i: (i, 0)),
        out_shape=jax.ShapeDtypeStruct((t, n), BF16),
        compiler_params=_params("parallel"),
        name="mem_kv",
    )(x, g, w)


def _xattn_kernel(x_ref, gpre_ref, wq_ref, k_ref, v_ref, wo_ref, gpost_ref, o_ref):
    x = x_ref[...]
    hn = _rms(x, gpre_ref[...]).astype(BF16)
    q = (_dot(hn, wq_ref[...]) * (X_HEAD_DIM ** -0.5)).astype(BF16)
    outs = []
    for h in range(X_HEADS):
        sl = slice(h * X_HEAD_DIM, (h + 1) * X_HEAD_DIM)
        s = _dot_nt(q[:, sl], k_ref[:, sl])
        m = jnp.max(s, axis=-1, keepdims=True)
        p = jnp.exp(s - m)
        l = jnp.sum(p, axis=-1, keepdims=True)
        outs.append(_dot(p.astype(BF16), v_ref[:, sl]) * (1.0 / l))
    o = jnp.concatenate(outs, axis=1).astype(BF16)
    y = _dot(o, wo_ref[...])
    o_ref[...] = x + _rms(y, gpost_ref[...])


def _xattn(x, g_pre, wq, kv, wo, g_post, *, batch, seq, mem_tokens, tm):
    t, d = x.shape
    xw = wq.shape[1]
    ns = seq // tm
    return pl.pallas_call(
        _xattn_kernel,
        grid=(batch, ns),
        in_specs=[
            pl.BlockSpec((tm, d), lambda b, i: (b * ns + i, 0)),
            _resident(g_pre.shape),
            _resident(wq.shape),
            pl.BlockSpec((mem_tokens, xw), lambda b, i: (b, 0)),
            pl.BlockSpec((mem_tokens, xw), lambda b, i: (b, 1)),
            _resident(wo.shape),
            _resident(g_post.shape),
        ],
        out_specs=pl.BlockSpec((tm, d), lambda b, i: (b * ns + i, 0)),
        out_shape=jax.ShapeDtypeStruct((t, d), F32),
        compiler_params=_params("parallel", "parallel"),
        name="xattn",
    )(x, g_pre, wq, kv, kv, wo, g_post)


def _gelu_tanh(x):
    return 0.5 * x * (1.0 + jnp.tanh(math.sqrt(2.0 / math.pi) * (x + 0.044715 * (x * x * x))))


def _ffn_kernel(xp_ref, x_ref, xn_ref, gpre_ref, wg_ref, wv_ref, cwg_ref, cwv_ref, cbg_ref, cbv_ref,
                wd_ref, gpost_ref, o_ref, hn_ref, acc_ref, ug_ref, uv_ref, *, tiles_per_seq):
    i = pl.program_id(0)
    j = pl.program_id(1)
    tm = x_ref.shape[0]
    halo = xp_ref.shape[0]

    @pl.when(j == 0)
    def _():
        g = gpre_ref[...]
        pos = i % tiles_per_seq
        hp = jnp.where(pos == 0, 0.0, _rms(xp_ref[...], g))
        hx = jnp.where(pos == tiles_per_seq - 1, 0.0, _rms(xn_ref[...], g))
        hn_ref[0:halo, :] = hp.astype(BF16)
        hn_ref[halo:halo + tm, :] = _rms(x_ref[...], g).astype(BF16)
        hn_ref[halo + tm:, :] = hx.astype(BF16)
        acc_ref[...] = jnp.zeros_like(acc_ref)

    hn = hn_ref[...]
    ug_ref[...] = _dot(hn, wg_ref[...])
    uv_ref[...] = _dot(hn, wv_ref[...])

    def conv(u_ref, cw_ref, cb_ref):
        return (u_ref[halo - 1:halo - 1 + tm, :] * cw_ref[0:1, :]
                + u_ref[halo:halo + tm, :] * cw_ref[1:2, :]
                + u_ref[halo + 1:halo + 1 + tm, :] * cw_ref[2:3, :]
                + cb_ref[...])

    act = _gelu_tanh(conv(ug_ref, cwg_ref, cbg_ref)) * conv(uv_ref, cwv_ref, cbv_ref)
    acc_ref[...] += _dot(act.astype(BF16), wd_ref[...])

    @pl.when(j == pl.num_programs(1) - 1)
    def _():
        o_ref[...] = x_ref[...] + _rms(acc_ref[...], gpost_ref[...])


def _ffn(x, g_pre, w_up, conv_w, conv_b, w_down, g_post, *, seq, tm, tf):
    t, d = x.shape
    d_ff = w_down.shape[0]
    nf = d_ff // tf
    halo = BF16_SUBLANES
    hb = tm // halo
    n_hblk = t // halo
    kern = functools.partial(_ffn_kernel, tiles_per_seq=seq // tm)
    return pl.pallas_call(
        kern,
        grid=(t // tm, nf),
        in_specs=[
            pl.BlockSpec((halo, d), lambda i, j: (jnp.maximum(i * hb - 1, 0), 0)),
            pl.BlockSpec((tm, d), lambda i, j: (i, 0)),
            pl.BlockSpec((halo, d), lambda i, j: (jnp.minimum((i + 1) * hb, n_hblk - 1), 0)),
            pl.BlockSpec((1, d), lambda i, j: (0, 0)),
            pl.BlockSpec((d, tf), lambda i, j: (0, j)),
            pl.BlockSpec((d, tf), lambda i, j: (0, nf + j)),
            pl.BlockSpec((3, tf), lambda i, j: (0, j)),
            pl.BlockSpec((3, tf), lambda i, j: (0, nf + j)),
            pl.BlockSpec((1, tf), lambda i, j: (0, j)),
            pl.BlockSpec((1, tf), lambda i, j: (0, nf + j)),
            pl.BlockSpec((tf, d), lambda i, j: (j, 0)),
            pl.BlockSpec((1, d), lambda i, j: (0, 0)),
        ],
        out_specs=pl.BlockSpec((tm, d), lambda i, j: (i, 0)),
        out_shape=jax.ShapeDtypeStruct((t, d), F32),
        scratch_shapes=[
            pltpu.VMEM((tm + 2 * halo, d), BF16),
            pltpu.VMEM((tm, d), F32),
            pltpu.VMEM((tm + 2 * halo, tf), F32),
            pltpu.VMEM((tm + 2 * halo, tf), F32),
        ],
        compiler_params=_params("parallel", "arbitrary"),
        name="conv_ffn",
    )(x, x, x, g_pre, w_up, w_up, conv_w, conv_w, conv_b, conv_b, w_down, g_post)


def _head_lane_order():
    quarter = HEAD_DIM // 4
    blocks = (0, 2, 1, 3)
    return jnp.concatenate([jnp.arange(b * quarter, (b + 1) * quarter) for b in blocks])


def _rope_tables(seq):
    half = HEAD_DIM // 2
    pos = jnp.arange(seq)
    row = (pos // GRID_W).astype(F32)
    col = (pos % GRID_W).astype(F32)
    inv_freq = 1.0 / (ROPE_THETA ** (jnp.arange(0, half, 2, dtype=F32) / half))
    lane = _head_lane_order()
    freq = inv_freq[(lane % half) % (half // 2)]
    ids = jnp.where(lane[None, :] < half, row[:, None], col[:, None])
    ang = ids * freq[None, :]
    sign = jnp.where((lane % half) < half // 2, -1.0, 1.0)
    return jnp.cos(ang), jnp.sin(ang) * sign[None, :]


def _dft_tables(seq, gdim):
    def cs(n):
        k = jnp.arange(n, dtype=jnp.int32)
        r = (k[:, None] * k[None, :]) % n
        ang = r.astype(F32) * (2.0 * math.pi / n)
        return jnp.cos(ang) * (n ** -0.5), jnp.sin(ang) * (n ** -0.5)
    cc, sc = cs(gdim)
    cq, sq = cs(seq)
    return (jnp.concatenate([cc, sc], axis=1).astype(BF16),
            jnp.concatenate([cq, -sq], axis=1).astype(BF16))


def _tiles(seq, d_ff):
    def pick(n, pref):
        return pref if n % pref == 0 else n
    tf = next(c for c in (512, 256, 128) if d_ff % c == 0)
    return dict(
        mixer_tm=pick(seq, 1024), mixer_tn=512,
        attn_tq=pick(seq, 512), attn_kc=pick(seq, 256),
        four_tm=pick(seq, 512), four_tn=512,
        merge_tm=pick(seq, 256),
        xattn_tm=pick(seq, 512),
        ffn_tm=pick(seq, 512), ffn_tf=tf,
    )


def kernel(x, mem, mix_pre_g, w_in, q_norm_g, k_norm_g, w_attn_o, w_four_o, w_gate, b_gate, w_mix_o,
           mix_post_g, xa_pre_g, mem_norm_g, w_xq, w_xkv, w_xo, xa_post_g, ffn_pre_g, w_up, conv_w,
           conv_b, w_down, ffn_post_g):
    batch, seq, d = x.shape
    mem_tokens = mem.shape[1]
    depth = w_in.shape[0]
    q_width = w_attn_o.shape[1]
    f_width = w_four_o.shape[1]
    kv_width = (w_in.shape[2] - q_width - f_width) // 2
    n_heads = q_width // HEAD_DIM
    assert n_heads == N_KV * GROUP and kv_width == N_KV * HEAD_DIM
    assert seq % GRID_W == 0 and w_xq.shape[2] == X_HEADS * X_HEAD_DIM
    d_ff = w_down.shape[1]
    ts = _tiles(seq, d_ff)
    qk_width = q_width + kv_width
    u_col = q_width + 2 * kv_width
    tm, tn = ts["mixer_tm"], ts["mixer_tn"]
    assert qk_width % tn == 0 and (kv_width + f_width) % tn == 0 and w_gate.shape[2] % tn == 0

    cos_t, sin_t = _rope_tables(seq)
    dft_c, dft_s = _dft_tables(seq, f_width // N_FOURIER_GROUPS)
    order = _head_lane_order()
    qk_cols = (jnp.arange(qk_width // HEAD_DIM)[:, None] * HEAD_DIM + order[None, :]).reshape(-1)
    partner = jnp.roll(jnp.arange(HEAD_DIM), HEAD_DIM // 2)
    col_spec = pl.BlockSpec((1, tn), lambda i, j: (0, j))

    row = lambda v: v.reshape(1, -1)
    xf = x.reshape(batch * seq, d)
    memf = mem.reshape(batch * mem_tokens, d)

    for l in range(depth):
        w_qk_t = w_in[l][:, qk_cols].T.astype(BF16)
        w_uv = jnp.concatenate([w_in[l][:, u_col:], w_in[l][:, qk_width:u_col]], axis=1).astype(BF16)
        q_gain = q_norm_g[l][order] * (HEAD_DIM ** -0.5 * math.log2(math.e))
        k_gain = k_norm_g[l][order]
        cg = jnp.concatenate([g[:, None] * cos_t.T for g in (q_gain, k_gain)], axis=0)
        sg = jnp.concatenate([g[partner][:, None] * sin_t.T for g in (q_gain, k_gain)], axis=0)
        g_pre = row(mix_pre_g[l])
        qk = _proj_qk_t(xf, g_pre, w_qk_t, cg, sg, seq=seq, n_q_tiles=q_width // tn, tm=tm, tr=tn)
        uv = _proj(xf, g_pre, w_uv, _epilogue_plain, (), (), tm=tm, tn=tn, name="proj_uv")
        gates = _proj(xf, g_pre, w_gate[l].astype(BF16), _epilogue_gate, (row(b_gate[l]),),
                      (col_spec,), tm=tm, tn=tn, name="proj_gate")
        att = _attention(qk, uv, batch=batch, seq=seq, q_width=q_width, v_col=f_width,
                         tq=ts["attn_tq"], kc=ts["attn_kc"])
        ucs = _four_chan(uv, dft_c, batch=batch, seq=seq, f_width=f_width, tm=ts["four_tm"])
        four = _four_seq(dft_s, ucs.reshape(batch, 2 * seq, f_width), batch=batch, seq=seq,
                         f_width=f_width, tm=ts["four_tm"], tn=min(ts["four_tn"], f_width))
        xf = _merge(att, four, gates, xf, w_attn_o[l].astype(BF16), w_four_o[l].astype(BF16),
                    w_mix_o[l].astype(BF16), row(mix_post_g[l]), tm=ts["merge_tm"])

        kv = _norm_matmul(memf, row(mem_norm_g[l]), w_xkv[l].astype(BF16), tm=mem_tokens)
        xf = _xattn(xf, row(xa_pre_g[l]), w_xq[l].astype(BF16), kv, w_xo[l].astype(BF16),
                    row(xa_post_g[l]), batch=batch, seq=seq, mem_tokens=mem_tokens, tm=ts["xattn_tm"])

        xf = _ffn(xf, row(ffn_pre_g[l]), w_up[l].astype(BF16), conv_w[l], row(conv_b[l]),
                  w_down[l].astype(BF16), row(ffn_post_g[l]), seq=seq, tm=ts["ffn_tm"], tf=ts["ffn_tf"])

    return xf.reshape(batch, seq, d)
```

```python
import functools
import math

import jax
import jax.numpy as jnp
from jax import lax
from jax.experimental import pallas as pl
from jax.experimental.pallas import tpu as pltpu

HEAD_DIM = 128
N_KV = 4
GROUP = 4
GRID_W = 64
ROPE_THETA = 10000.0
N_FOURIER_GROUPS = 4
X_HEADS = 4
X_HEAD_DIM = 128
EPS = 1e-6

V7X_VMEM_LIMIT_BYTES = 56 * 1024 * 1024
BF16_SUBLANES = 16
ATTN_MIN_DENOMINATOR = 2.0 ** -60
FOUR_EXTRA_ROWS = BF16_SUBLANES

F32 = jnp.float32
BF16 = jnp.bfloat16


def _params(*sem):
    return pltpu.CompilerParams(dimension_semantics=sem, vmem_limit_bytes=V7X_VMEM_LIMIT_BYTES)


def _rms(xf, g):
    ms = jnp.mean(xf * xf, axis=-1, keepdims=True)
    return xf * lax.rsqrt(ms + EPS) * g


def _dot(a, b):
    return jnp.dot(a, b, preferred_element_type=F32)


def _dot_nt(a, b):
    return lax.dot_general(a, b, (((1,), (1,)), ((), ())), preferred_element_type=F32)


def _epilogue_plain(z, o_ref):
    o_ref[...] = z.astype(BF16)


def _epilogue_gate(z, b_ref, o_ref):
    o_ref[...] = (0.5 * jnp.tanh(0.5 * (z + b_ref[...])) + 0.5).astype(BF16)


def _proj_qk_t_kernel(x_ref, g_ref, wt_ref, cg_ref, sg_ref, o_ref, hn_ref):
    @pl.when(pl.program_id(1) == 0)
    def _():
        hn_ref[...] = _rms(x_ref[...], g_ref[...]).astype(BF16)

    zt = _dot_nt(wt_ref[...], hn_ref[...])
    cg = cg_ref[...]
    sg = sg_ref[...]
    half = HEAD_DIM // 2
    for h in range(zt.shape[0] // HEAD_DIM):
        y = zt[h * HEAD_DIM:(h + 1) * HEAD_DIM, :]
        y = y * lax.rsqrt(jnp.mean(y * y, axis=0, keepdims=True) + EPS)
        rot = jnp.concatenate([y[half:], y[:half]], axis=0)
        o_ref[h * HEAD_DIM:(h + 1) * HEAD_DIM, :] = (y * cg + rot * sg).astype(BF16)


def _proj_qk_t(x, g_pre, w_t, cg, sg, *, seq, n_q_tiles, tm, tr):
    t, d = x.shape
    n = w_t.shape[0]
    seq_tiles = seq // tm
    table_spec = pl.BlockSpec(
        (HEAD_DIM, tm), lambda i, j: (jnp.where(j < n_q_tiles, 0, 1), i % seq_tiles))
    return pl.pallas_call(
        _proj_qk_t_kernel,
        grid=(t // tm, n // tr),
        in_specs=[
            pl.BlockSpec((tm, d), lambda i, j: (i, 0)),
            pl.BlockSpec((1, d), lambda i, j: (0, 0)),
            pl.BlockSpec((tr, d), lambda i, j: (j, 0)),
            table_spec,
            table_spec,
        ],
        out_specs=pl.BlockSpec((tr, tm), lambda i, j: (j, i)),
        out_shape=jax.ShapeDtypeStruct((n, t), BF16),
        scratch_shapes=[pltpu.VMEM((tm, d), BF16)],
        compiler_params=_params("parallel", "arbitrary"),
        name="proj_qk",
    )(x, g_pre, w_t, cg, sg)


def _proj_kernel(x_ref, g_ref, w_ref, *rest, epilogue):
    hn_ref = rest[-1]

    @pl.when(pl.program_id(1) == 0)
    def _():
        hn_ref[...] = _rms(x_ref[...], g_ref[...]).astype(BF16)

    epilogue(_dot(hn_ref[...], w_ref[...]), *rest[:-1])


def _proj(x, g_pre, w, epilogue, extra, extra_specs, *, tm, tn, name):
    t, d = x.shape
    n = w.shape[1]
    return pl.pallas_call(
        functools.partial(_proj_kernel, epilogue=epilogue),
        grid=(t // tm, n // tn),
        in_specs=[
            pl.BlockSpec((tm, d), lambda i, j: (i, 0)),
            pl.BlockSpec((1, d), lambda i, j: (0, 0)),
            pl.BlockSpec((d, tn), lambda i, j: (0, j)),
            *extra_specs,
        ],
        out_specs=pl.BlockSpec((tm, tn), lambda i, j: (i, j)),
        out_shape=jax.ShapeDtypeStruct((t, n), BF16),
        scratch_shapes=[pltpu.VMEM((tm, d), BF16)],
        compiler_params=_params("parallel", "arbitrary"),
        name=name,
    )(x, g_pre, w, *extra)


def _attn_exact(q, ka_ref, va_ref, kc):
    rows = q.shape[0]
    m = jnp.full((rows, 1), -jnp.inf, F32)
    l = jnp.zeros((rows, 1), F32)
    acc = jnp.zeros((rows, HEAD_DIM), F32)
    for c in range(ka_ref.shape[0] // kc):
        ks = slice(c * kc, (c + 1) * kc)
        s = _dot_nt(q, ka_ref[ks, :HEAD_DIM])
        m_new = jnp.maximum(m, jnp.max(s, axis=-1, keepdims=True))
        alpha = jnp.exp2(m - m_new)
        p = jnp.exp2(s - m_new)
        l = alpha * l + jnp.sum(p, axis=-1, keepdims=True)
        acc = alpha * acc + _dot(p.astype(BF16), va_ref[ks, :HEAD_DIM])
        m = m_new
    return acc * (1.0 / l)


def _attn_kernel(qt_ref, kt_ref, v_ref, o_ref, ka_ref, va_ref, kmax_ref, *, kc):
    tq = qt_ref.shape[1]
    s_len = v_ref.shape[0]
    n_chunks = s_len // kc
    lane = lax.broadcasted_iota(jnp.int32, (1, HEAD_DIM), 1)

    @pl.when(pl.program_id(2) == 0)
    def _():
        ones_col = jnp.broadcast_to(jnp.where(lane == 0, 1.0, 0.0), (kc, HEAD_DIM)).astype(BF16)
        k2max = jnp.zeros((1, 1), F32)
        for c in range(n_chunks):
            ks = slice(c * kc, (c + 1) * kc)
            kt = kt_ref[:, ks]
            kf = kt.astype(F32)
            k2max = jnp.maximum(k2max, jnp.max(jnp.sum(kf * kf, axis=0, keepdims=True),
                                               axis=1, keepdims=True))
            ka_ref[ks, :HEAD_DIM] = kt.T
            ka_ref[ks, HEAD_DIM:] = ones_col
            va_ref[ks, :HEAD_DIM] = v_ref[ks, :]
            va_ref[ks, HEAD_DIM:] = ones_col
        kmax_ref[...] = jnp.broadcast_to(jnp.sqrt(k2max), kmax_ref.shape)

    q = jnp.concatenate(
        [qt_ref[g * HEAD_DIM:(g + 1) * HEAD_DIM, :].T for g in range(GROUP)], axis=0)
    qf = q.astype(F32)
    shift = jnp.sqrt(jnp.sum(qf * qf, axis=-1, keepdims=True)) * kmax_ref[0:1, 0:1]
    shift_col = (-shift) * jnp.where(lane == 0, 1.0, 0.0)
    q_aug = jnp.concatenate([q, shift_col.astype(BF16)], axis=1)
    acc = jnp.zeros((GROUP * tq, 2 * HEAD_DIM), F32)
    for c in range(n_chunks):
        ks = slice(c * kc, (c + 1) * kc)
        p = jnp.exp2(_dot_nt(q_aug, ka_ref[ks, :])).astype(BF16)
        acc = acc + _dot(p, va_ref[ks, :])
    l = acc[:, HEAD_DIM:HEAD_DIM + 1]
    accurate = jnp.min(l) >= ATTN_MIN_DENOMINATOR

    def write(o):
        for g in range(GROUP):
            o_ref[:, g * HEAD_DIM:(g + 1) * HEAD_DIM] = o[g * tq:(g + 1) * tq].astype(BF16)

    @pl.when(accurate)
    def _():
        write(acc[:, :HEAD_DIM] * (1.0 / l))

    @pl.when(jnp.logical_not(accurate))
    def _():
        write(_attn_exact(q, ka_ref, va_ref, kc))


def _attention(qk_t, vu, *, batch, seq, q_width, v_col, tq, kc):
    t = vu.shape[0]
    gw = GROUP * HEAD_DIM
    k_blk0 = q_width // HEAD_DIM
    v_blk0 = v_col // HEAD_DIM
    nq = seq // tq
    return pl.pallas_call(
        functools.partial(_attn_kernel, kc=kc),
        grid=(batch, N_KV, nq),
        in_specs=[
            pl.BlockSpec((gw, tq), lambda b, h, i: (h, b * nq + i)),
            pl.BlockSpec((HEAD_DIM, seq), lambda b, h, i: (k_blk0 + h, b)),
            pl.BlockSpec((seq, HEAD_DIM), lambda b, h, i: (b, v_blk0 + h)),
        ],
        out_specs=pl.BlockSpec((tq, gw), lambda b, h, i: (b * nq + i, h)),
        out_shape=jax.ShapeDtypeStruct((t, q_width), BF16),
        scratch_shapes=[
            pltpu.VMEM((seq, 2 * HEAD_DIM), BF16),
            pltpu.VMEM((seq, 2 * HEAD_DIM), BF16),
            pltpu.VMEM((8, HEAD_DIM), F32),
        ],
        compiler_params=_params("parallel", "parallel", "arbitrary"),
        name="attention",
    )(qk_t, qk_t, vu)


def _four_chan_kernel(u_ref, dft_ref, o_ref, *, gdim):
    for g in range(u_ref.shape[1] // gdim):
        sl = slice(g * gdim, (g + 1) * gdim)
        r = _dot(u_ref[:, sl], dft_ref[...])
        o_ref[0, 0, :, sl] = r[:, :gdim].astype(BF16)
        o_ref[0, 1, :, sl] = r[:, gdim:].astype(BF16)


def _four_chan(uv, dft_c, *, batch, seq, f_width, tm):
    gdim = f_width // N_FOURIER_GROUPS
    ns = seq // tm
    return pl.pallas_call(
        functools.partial(_four_chan_kernel, gdim=gdim),
        grid=(batch, ns),
        in_specs=[
            pl.BlockSpec((tm, f_width), lambda b, i: (b * ns + i, 0)),
            pl.BlockSpec((gdim, 2 * gdim), lambda b, i: (0, 0)),
        ],
        out_specs=pl.BlockSpec((1, 2, tm, f_width), lambda b, i: (b, 0, i, 0)),
        out_shape=jax.ShapeDtypeStruct((batch, 2, seq, f_width), BF16),
        compiler_params=_params("parallel", "parallel"),
        name="fourier_chan",
    )(uv, dft_c)


def _four_seq_kernel(ch_ref, sh_ref, uc_ref, us_ref, flip_ref, o_ref):
    fr = o_ref.shape[3]
    p = _dot(ch_ref[0], uc_ref[0, 0])
    q = _dot(sh_ref[0], us_ref[0, 0])
    o_ref[0, 0, 0] = (p - q)[:fr].astype(BF16)
    o_ref[0, 1, 0] = _dot(flip_ref[...], (p + q).astype(BF16)).astype(BF16)


def _four_seq(ch, sh, flip, ucs, *, batch, seq, f_width):
    nt, rows, _ = ch.shape
    fr = flip.shape[0]
    half_spec = pl.BlockSpec((1, rows, seq), lambda b, t: (t, 0, 0))
    return pl.pallas_call(
        _four_seq_kernel,
        grid=(batch, nt),
        in_specs=[
            half_spec,
            half_spec,
            pl.BlockSpec((1, 1, seq, f_width), lambda b, t: (b, 0, 0, 0), pipeline_mode=pl.Buffered(1)),
            pl.BlockSpec((1, 1, seq, f_width), lambda b, t: (b, 1, 0, 0), pipeline_mode=pl.Buffered(1)),
            _resident(flip.shape),
        ],
        out_specs=pl.BlockSpec((1, 2, 1, fr, f_width), lambda b, t: (b, 0, t, 0, 0)),
        out_shape=jax.ShapeDtypeStruct((batch, 2, nt, fr, f_width), BF16),
        compiler_params=_params("parallel", "arbitrary"),
        name="fourier_seq",
    )(ch, sh, ucs, ucs, flip)


def _merge_kernel(att_ref, y_ref, g1_ref, g2_ref, x_ref, wa_ref, wf_ref, wm_ref, gp_ref, o_ref):
    a = _dot(att_ref[...], wa_ref[...])
    f = _dot(y_ref[...], wf_ref[...])
    merged = g1_ref[...].astype(F32) * a + g2_ref[...].astype(F32) * f
    y = _dot(merged.astype(BF16), wm_ref[...])
    o_ref[...] = x_ref[...] + _rms(y, gp_ref[...])


def _resident(shape):
    return pl.BlockSpec(shape, lambda *_: (0,) * len(shape), pipeline_mode=pl.Buffered(1))


def _merge(att, four, gates, x, wa, wf, wm, g_post, *, seq, tm):
    t, d = x.shape
    tiles = seq // tm

    def four_block(i):
        it = i % tiles
        return (i - it + jnp.where(it < tiles // 2, it, tiles + tiles // 2 - 1 - it), 0)

    return pl.pallas_call(
        _merge_kernel,
        grid=(t // tm,),
        in_specs=[
            pl.BlockSpec((tm, att.shape[1]), lambda i: (i, 0)),
            pl.BlockSpec((tm, four.shape[1]), four_block),
            pl.BlockSpec((tm, d), lambda i: (i, 0)),
            pl.BlockSpec((tm, d), lambda i: (i, 1)),
            pl.BlockSpec((tm, d), lambda i: (i, 0)),
            _resident(wa.shape),
            _resident(wf.shape),
            _resident(wm.shape),
            _resident(g_post.shape),
        ],
        out_specs=pl.BlockSpec((tm, d), lambda i: (i, 0)),
        out_shape=jax.ShapeDtypeStruct((t, d), F32),
        compiler_params=_params("parallel"),
        name="merge",
    )(att, four, gates, gates, x, wa, wf, wm, g_post)


def _norm_matmul_kernel(x_ref, g_ref, w_ref, o_ref):
    hn = _rms(x_ref[...], g_ref[...]).astype(BF16)
    o_ref[...] = _dot(hn, w_ref[...]).astype(BF16)


def _norm_matmul(x, g, w, *, tm):
    t, d = x.shape
    n = w.shape[1]
    return pl.pallas_call(
        _norm_matmul_kernel,
        grid=(t // tm,),
        in_specs=[
            pl.BlockSpec((tm, d), lambda i: (i, 0)),
            _resident(g.shape),
            _resident(w.shape),
        ],
        out_specs=pl.BlockSpec((tm, n), lambda i: (i, 0)),
        out_shape=jax.ShapeDtypeStruct((t, n), BF16),
        compiler_params=_params("parallel"),
        name="mem_kv",
    )(x, g, w)


def _xattn_kernel(x_ref, gpre_ref, wq_ref, k_ref, v_ref, wo_ref, gpost_ref, o_ref):
    x = x_ref[...]
    hn = _rms(x, gpre_ref[...]).astype(BF16)
    q = (_dot(hn, wq_ref[...]) * (X_HEAD_DIM ** -0.5)).astype(BF16)
    outs = []
    for h in range(X_HEADS):
        sl = slice(h * X_HEAD_DIM, (h + 1) * X_HEAD_DIM)
        s = _dot_nt(q[:, sl], k_ref[:, sl])
        m = jnp.max(s, axis=-1, keepdims=True)
        p = jnp.exp(s - m)
        l = jnp.sum(p, axis=-1, keepdims=True)
        outs.append(_dot(p.astype(BF16), v_ref[:, sl]) * (1.0 / l))
    o = jnp.concatenate(outs, axis=1).astype(BF16)
    y = _dot(o, wo_ref[...])
    o_ref[...] = x + _rms(y, gpost_ref[...])


def _xattn(x, g_pre, wq, kv, wo, g_post, *, batch, seq, mem_tokens, tm):
    t, d = x.shape
    xw = wq.shape[1]
    ns = seq // tm
    return pl.pallas_call(
        _xattn_kernel,
        grid=(batch, ns),
        in_specs=[
            pl.BlockSpec((tm, d), lambda b, i: (b * ns + i, 0)),
            _resident(g_pre.shape),
            _resident(wq.shape),
            pl.BlockSpec((mem_tokens, xw), lambda b, i: (b, 0)),
            pl.BlockSpec((mem_tokens, xw), lambda b, i: (b, 1)),
            _resident(wo.shape),
            _resident(g_post.shape),
        ],
        out_specs=pl.BlockSpec((tm, d), lambda b, i: (b * ns + i, 0)),
        out_shape=jax.ShapeDtypeStruct((t, d), F32),
        compiler_params=_params("parallel", "parallel"),
        name="xattn",
    )(x, g_pre, wq, kv, kv, wo, g_post)


def _gelu_tanh(x):
    return 0.5 * x * (1.0 + jnp.tanh(math.sqrt(2.0 / math.pi) * (x + 0.044715 * (x * x * x))))


def _ffn_kernel(xp_ref, x_ref, xn_ref, gpre_ref, wg_ref, wv_ref, cwg_ref, cwv_ref, cbg_ref, cbv_ref,
                wd_ref, gpost_ref, o_ref, hn_ref, acc_ref, ug_ref, uv_ref, *, tiles_per_seq):
    i = pl.program_id(0)
    j = pl.program_id(1)
    tm = x_ref.shape[0]
    halo = xp_ref.shape[0]

    @pl.when(j == 0)
    def _():
        g = gpre_ref[...]
        pos = i % tiles_per_seq
        hp = jnp.where(pos == 0, 0.0, _rms(xp_ref[...], g))
        hx = jnp.where(pos == tiles_per_seq - 1, 0.0, _rms(xn_ref[...], g))
        hn_ref[0:halo, :] = hp.astype(BF16)
        hn_ref[halo:halo + tm, :] = _rms(x_ref[...], g).astype(BF16)
        hn_ref[halo + tm:, :] = hx.astype(BF16)
        acc_ref[...] = jnp.zeros_like(acc_ref)

    hn = hn_ref[...]
    ug_ref[...] = _dot(hn, wg_ref[...])
    uv_ref[...] = _dot(hn, wv_ref[...])

    def conv(u_ref, cw_ref, cb_ref):
        return (u_ref[halo - 1:halo - 1 + tm, :] * cw_ref[0:1, :]
                + u_ref[halo:halo + tm, :] * cw_ref[1:2, :]
                + u_ref[halo + 1:halo + 1 + tm, :] * cw_ref[2:3, :]
                + cb_ref[...])

    act = _gelu_tanh(conv(ug_ref, cwg_ref, cbg_ref)) * conv(uv_ref, cwv_ref, cbv_ref)
    acc_ref[...] += _dot(act.astype(BF16), wd_ref[...])

    @pl.when(j == pl.num_programs(1) - 1)
    def _():
        o_ref[...] = x_ref[...] + _rms(acc_ref[...], gpost_ref[...])


def _ffn(x, g_pre, w_up, conv_w, conv_b, w_down, g_post, *, seq, tm, tf):
    t, d = x.shape
    d_ff = w_down.shape[0]
    nf = d_ff // tf
    halo = BF16_SUBLANES
    hb = tm // halo
    n_hblk = t // halo
    kern = functools.partial(_ffn_kernel, tiles_per_seq=seq // tm)
    return pl.pallas_call(
        kern,
        grid=(t // tm, nf),
        in_specs=[
            pl.BlockSpec((halo, d), lambda i, j: (jnp.maximum(i * hb - 1, 0), 0)),
            pl.BlockSpec((tm, d), lambda i, j: (i, 0)),
            pl.BlockSpec((halo, d), lambda i, j: (jnp.minimum((i + 1) * hb, n_hblk - 1), 0)),
            pl.BlockSpec((1, d), lambda i, j: (0, 0)),
            pl.BlockSpec((d, tf), lambda i, j: (0, j)),
            pl.BlockSpec((d, tf), lambda i, j: (0, nf + j)),
            pl.BlockSpec((3, tf), lambda i, j: (0, j)),
            pl.BlockSpec((3, tf), lambda i, j: (0, nf + j)),
            pl.BlockSpec((1, tf), lambda i, j: (0, j)),
            pl.BlockSpec((1, tf), lambda i, j: (0, nf + j)),
            pl.BlockSpec((tf, d), lambda i, j: (j, 0)),
            pl.BlockSpec((1, d), lambda i, j: (0, 0)),
        ],
        out_specs=pl.BlockSpec((tm, d), lambda i, j: (i, 0)),
        out_shape=jax.ShapeDtypeStruct((t, d), F32),
        scratch_shapes=[
            pltpu.VMEM((tm + 2 * halo, d), BF16),
            pltpu.VMEM((tm, d), F32),
            pltpu.VMEM((tm + 2 * halo, tf), F32),
            pltpu.VMEM((tm + 2 * halo, tf), F32),
        ],
        compiler_params=_params("parallel", "arbitrary"),
        name="conv_ffn",
    )(x, x, x, g_pre, w_up, w_up, conv_w, conv_w, conv_b, conv_b, w_down, g_post)


def _head_lane_order():
    quarter = HEAD_DIM // 4
    blocks = (0, 2, 1, 3)
    return jnp.concatenate([jnp.arange(b * quarter, (b + 1) * quarter) for b in blocks])


def _rope_tables(seq):
    half = HEAD_DIM // 2
    pos = jnp.arange(seq)
    row = (pos // GRID_W).astype(F32)
    col = (pos % GRID_W).astype(F32)
    inv_freq = 1.0 / (ROPE_THETA ** (jnp.arange(0, half, 2, dtype=F32) / half))
    lane = _head_lane_order()
    freq = inv_freq[(lane % half) % (half // 2)]
    ids = jnp.where(lane[None, :] < half, row[:, None], col[:, None])
    ang = ids * freq[None, :]
    sign = jnp.where((lane % half) < half // 2, -1.0, 1.0)
    return jnp.cos(ang), jnp.sin(ang) * sign[None, :]


def _dft_tables(seq, gdim, fr):
    def cs(k, n, period):
        r = (k[..., None] * n) % period
        ang = r.astype(F32) * (2.0 * math.pi / period)
        return jnp.cos(ang), jnp.sin(ang)
    c = jnp.arange(gdim, dtype=jnp.int32)
    cc, sc = (t * gdim ** -0.5 for t in cs(c, c, gdim))
    rows = fr + FOUR_EXTRA_ROWS
    k = jnp.arange(seq // 2 // fr, dtype=jnp.int32)[:, None] * fr + jnp.arange(rows, dtype=jnp.int32)
    ca, sa = cs(k, GRID_W * jnp.arange(seq // GRID_W, dtype=jnp.int32), seq)
    cb, sb = cs(k, jnp.arange(GRID_W, dtype=jnp.int32), seq)
    scale = seq ** -0.5
    ch = ((ca[..., :, None] * cb[..., None, :] - sa[..., :, None] * sb[..., None, :]) * scale)
    sh = ((sa[..., :, None] * cb[..., None, :] + ca[..., :, None] * sb[..., None, :]) * scale)
    ch, sh = ch.reshape(*k.shape, seq), sh.reshape(*k.shape, seq)
    flip = (jnp.arange(rows)[None, :] == fr - jnp.arange(fr)[:, None]).astype(BF16)
    return (jnp.concatenate([cc, sc], axis=1).astype(BF16), ch.astype(BF16), sh.astype(BF16), flip)


def _tiles(seq, d_ff):
    def pick(n, pref):
        return pref if n % pref == 0 else n
    tf = next(c for c in (512, 256, 128) if d_ff % c == 0)
    return dict(
        mixer_tm=pick(seq, 1024), qk_tr=512, uv_tn=768, gate_tn=1024,
        attn_tq=pick(seq, 1024), attn_kc=pick(seq, 256),
        four_tm=pick(seq, 512),
        merge_tm=pick(seq, 256),
        xattn_tm=pick(seq, 512),
        ffn_tm=pick(seq, 512), ffn_tf=tf,
    )


def kernel(x, mem, mix_pre_g, w_in, q_norm_g, k_norm_g, w_attn_o, w_four_o, w_gate, b_gate, w_mix_o,
           mix_post_g, xa_pre_g, mem_norm_g, w_xq, w_xkv, w_xo, xa_post_g, ffn_pre_g, w_up, conv_w,
           conv_b, w_down, ffn_post_g):
    batch, seq, d = x.shape
    mem_tokens = mem.shape[1]
    depth = w_in.shape[0]
    q_width = w_attn_o.shape[1]
    f_width = w_four_o.shape[1]
    kv_width = (w_in.shape[2] - q_width - f_width) // 2
    n_heads = q_width // HEAD_DIM
    assert n_heads == N_KV * GROUP and kv_width == N_KV * HEAD_DIM
    assert seq % GRID_W == 0 and w_xq.shape[2] == X_HEADS * X_HEAD_DIM
    d_ff = w_down.shape[1]
    ts = _tiles(seq, d_ff)
    qk_width = q_width + kv_width
    u_col = q_width + 2 * kv_width
    tm = ts["mixer_tm"]
    assert q_width % ts["qk_tr"] == 0 and kv_width % ts["qk_tr"] == 0
    assert (kv_width + f_width) % ts["uv_tn"] == 0 and w_gate.shape[2] % ts["gate_tn"] == 0

    cos_t, sin_t = _rope_tables(seq)
    assert seq % (2 * ts["merge_tm"]) == 0
    dft_c, dft_ch, dft_sh, dft_flip = _dft_tables(seq, f_width // N_FOURIER_GROUPS, ts["merge_tm"])
    order = _head_lane_order()
    qk_cols = (jnp.arange(qk_width // HEAD_DIM)[:, None] * HEAD_DIM + order[None, :]).reshape(-1)
    partner = jnp.roll(jnp.arange(HEAD_DIM), HEAD_DIM // 2)
    bias_spec = pl.BlockSpec((1, ts["gate_tn"]), lambda i, j: (0, j))

    row = lambda v: v.reshape(1, -1)
    xf = x.reshape(batch * seq, d)
    memf = mem.reshape(batch * mem_tokens, d)

    for l in range(depth):
        w_qk_t = w_in[l][:, qk_cols].T.astype(BF16)
        w_uv = jnp.concatenate([w_in[l][:, u_col:], w_in[l][:, qk_width:u_col]], axis=1).astype(BF16)
        q_gain = q_norm_g[l][order] * (HEAD_DIM ** -0.5 * math.log2(math.e))
        k_gain = k_norm_g[l][order]
        cg = jnp.concatenate([g[:, None] * cos_t.T for g in (q_gain, k_gain)], axis=0)
        sg = jnp.concatenate([g[partner][:, None] * sin_t.T for g in (q_gain, k_gain)], axis=0)
        g_pre = row(mix_pre_g[l])
        qk = _proj_qk_t(xf, g_pre, w_qk_t, cg, sg, seq=seq, n_q_tiles=q_width // ts["qk_tr"],
                        tm=tm, tr=ts["qk_tr"])
        uv = _proj(xf, g_pre, w_uv, _epilogue_plain, (), (), tm=tm, tn=ts["uv_tn"], name="proj_uv")
        gates = _proj(xf, g_pre, w_gate[l].astype(BF16), _epilogue_gate, (row(b_gate[l]),),
                      (bias_spec,), tm=tm, tn=ts["gate_tn"], name="proj_gate")
        att = _attention(qk, uv, batch=batch, seq=seq, q_width=q_width, v_col=f_width,
                         tq=ts["attn_tq"], kc=ts["attn_kc"])
        ucs = _four_chan(uv, dft_c, batch=batch, seq=seq, f_width=f_width, tm=ts["four_tm"])
        four = _four_seq(dft_ch, dft_sh, dft_flip, ucs, batch=batch, seq=seq, f_width=f_width)
        xf = _merge(att, four.reshape(batch * seq, f_width), gates, xf, w_attn_o[l].astype(BF16),
                    w_four_o[l].astype(BF16), w_mix_o[l].astype(BF16), row(mix_post_g[l]),
                    seq=seq, tm=ts["merge_tm"])

        kv = _norm_matmul(memf, row(mem_norm_g[l]), w_xkv[l].astype(BF16), tm=mem_tokens)
        xf = _xattn(xf, row(xa_pre_g[l]), w_xq[l].astype(BF16), kv, w_xo[l].astype(BF16),
                    row(xa_post_g[l]), batch=batch, seq=seq, mem_tokens=mem_tokens, tm=ts["xattn_tm"])

        xf = _ffn(xf, row(ffn_pre_g[l]), w_up[l].astype(BF16), conv_w[l], row(conv_b[l]),
                  w_down[l].astype(BF16), row(ffn_post_g[l]), seq=seq, tm=ts["ffn_tm"], tf=ts["ffn_tf"])

    return xf.reshape(batch, seq, d)
```

```python
import functools
import math

import jax
import jax.numpy as jnp
from jax import lax
from jax.experimental import pallas as pl
from jax.experimental.pallas import tpu as pltpu

HEAD_DIM = 128
N_KV = 4
GROUP = 4
GRID_W = 64
ROPE_THETA = 10000.0
N_FOURIER_GROUPS = 4
X_HEADS = 4
X_HEAD_DIM = 128
EPS = 1e-6

V7X_VMEM_LIMIT_BYTES = 56 * 1024 * 1024
BF16_SUBLANES = 16
ATTN_MIN_DENOMINATOR = 2.0 ** -60
FOUR_EXTRA_ROWS = BF16_SUBLANES
FFN_NORM_ROWS = 128

F32 = jnp.float32
BF16 = jnp.bfloat16


def _params(*sem):
    return pltpu.CompilerParams(dimension_semantics=sem, vmem_limit_bytes=V7X_VMEM_LIMIT_BYTES)


def _rms(xf, g):
    ms = jnp.mean(xf * xf, axis=-1, keepdims=True)
    return xf * lax.rsqrt(ms + EPS) * g


def _dot(a, b):
    return jnp.dot(a, b, preferred_element_type=F32)


def _dot_nt(a, b):
    return lax.dot_general(a, b, (((1,), (1,)), ((), ())), preferred_element_type=F32)


def _epilogue_plain(z, o_ref):
    o_ref[...] = z.astype(BF16)


def _epilogue_gate(z, b_ref, o_ref):
    o_ref[...] = (0.5 * jnp.tanh(0.5 * (z + b_ref[...])) + 0.5).astype(BF16)


def _proj_qk_t_kernel(x_ref, g_ref, wt_ref, cg_ref, sg_ref, o_ref, *, n_q_heads):
    hn = _rms(x_ref[...], g_ref[...]).astype(BF16)
    zt = _dot_nt(wt_ref[...], hn)
    half = HEAD_DIM // 2
    for h in range(zt.shape[0] // HEAD_DIM):
        tab = slice(0, HEAD_DIM) if h < n_q_heads else slice(HEAD_DIM, 2 * HEAD_DIM)
        y = zt[h * HEAD_DIM:(h + 1) * HEAD_DIM, :]
        y = y * lax.rsqrt(jnp.mean(y * y, axis=0, keepdims=True) + EPS)
        rot = jnp.concatenate([y[half:], y[:half]], axis=0)
        o_ref[h * HEAD_DIM:(h + 1) * HEAD_DIM, :] = (y * cg_ref[tab, :] + rot * sg_ref[tab, :]).astype(BF16)


def _proj_qk_t(x, g_pre, w_t, cg, sg, *, seq, n_q_heads, tm):
    t, d = x.shape
    n = w_t.shape[0]
    seq_tiles = seq // tm
    table_spec = pl.BlockSpec((2 * HEAD_DIM, tm), lambda i: (0, i % seq_tiles))
    return pl.pallas_call(
        functools.partial(_proj_qk_t_kernel, n_q_heads=n_q_heads),
        grid=(t // tm,),
        in_specs=[
            pl.BlockSpec((tm, d), lambda i: (i, 0)),
            _resident(g_pre.shape),
            _resident(w_t.shape),
            table_spec,
            table_spec,
        ],
        out_specs=pl.BlockSpec((n, tm), lambda i: (0, i)),
        out_shape=jax.ShapeDtypeStruct((n, t), BF16),
        compiler_params=_params("parallel"),
        name="proj_qk",
    )(x, g_pre, w_t, cg, sg)


def _proj_kernel(x_ref, g_ref, w_ref, *rest, epilogue):
    hn_ref = rest[-1]

    @pl.when(pl.program_id(1) == 0)
    def _():
        hn_ref[...] = _rms(x_ref[...], g_ref[...]).astype(BF16)

    epilogue(_dot(hn_ref[...], w_ref[...]), *rest[:-1])


def _proj(x, g_pre, w, epilogue, extra, extra_specs, *, tm, tn, name):
    t, d = x.shape
    n = w.shape[1]
    return pl.pallas_call(
        functools.partial(_proj_kernel, epilogue=epilogue),
        grid=(t // tm, n // tn),
        in_specs=[
            pl.BlockSpec((tm, d), lambda i, j: (i, 0)),
            pl.BlockSpec((1, d), lambda i, j: (0, 0)),
            pl.BlockSpec((d, tn), lambda i, j: (0, j)),
            *extra_specs,
        ],
        out_specs=pl.BlockSpec((tm, tn), lambda i, j: (i, j)),
        out_shape=jax.ShapeDtypeStruct((t, n), BF16),
        scratch_shapes=[pltpu.VMEM((tm, d), BF16)],
        compiler_params=_params("parallel", "arbitrary"),
        name=name,
    )(x, g_pre, w, *extra)


def _attn_exact(q, ka_ref, va_ref, kc):
    rows = q.shape[0]
    m = jnp.full((rows, 1), -jnp.inf, F32)
    l = jnp.zeros((rows, 1), F32)
    acc = jnp.zeros((rows, HEAD_DIM), F32)
    for c in range(ka_ref.shape[0] // kc):
        ks = slice(c * kc, (c + 1) * kc)
        s = _dot_nt(q, ka_ref[ks, :HEAD_DIM])
        m_new = jnp.maximum(m, jnp.max(s, axis=-1, keepdims=True))
        alpha = jnp.exp2(m - m_new)
        p = jnp.exp2(s - m_new)
        l = alpha * l + jnp.sum(p, axis=-1, keepdims=True)
        acc = alpha * acc + _dot(p.astype(BF16), va_ref[ks, :HEAD_DIM])
        m = m_new
    return acc * (1.0 / l)


def _attn_kernel(qt_ref, kt_ref, v_ref, o_ref, ka_ref, va_ref, kmax_ref, *, kc):
    tq = qt_ref.shape[1]
    s_len = v_ref.shape[0]
    n_chunks = s_len // kc
    lane = lax.broadcasted_iota(jnp.int32, (1, HEAD_DIM), 1)

    @pl.when(pl.program_id(2) == 0)
    def _():
        ones_col = jnp.broadcast_to(jnp.where(lane == 0, 1.0, 0.0), (kc, HEAD_DIM)).astype(BF16)
        k2max = jnp.zeros((1, 1), F32)
        for c in range(n_chunks):
            ks = slice(c * kc, (c + 1) * kc)
            kt = kt_ref[:, ks]
            kf = kt.astype(F32)
            k2max = jnp.maximum(k2max, jnp.max(jnp.sum(kf * kf, axis=0, keepdims=True),
                                               axis=1, keepdims=True))
            ka_ref[ks, :HEAD_DIM] = kt.T
            ka_ref[ks, HEAD_DIM:] = ones_col
            va_ref[ks, :HEAD_DIM] = v_ref[ks, :]
            va_ref[ks, HEAD_DIM:] = ones_col
        kmax_ref[...] = jnp.broadcast_to(jnp.sqrt(k2max), kmax_ref.shape)

    q = jnp.concatenate(
        [qt_ref[g * HEAD_DIM:(g + 1) * HEAD_DIM, :].T for g in range(GROUP)], axis=0)
    qf = q.astype(F32)
    shift = jnp.sqrt(jnp.sum(qf * qf, axis=-1, keepdims=True)) * kmax_ref[0:1, 0:1]
    shift_col = (-shift) * jnp.where(lane == 0, 1.0, 0.0)
    q_aug = jnp.concatenate([q, shift_col.astype(BF16)], axis=1)
    acc = jnp.zeros((GROUP * tq, 2 * HEAD_DIM), F32)
    for c in range(n_chunks):
        ks = slice(c * kc, (c + 1) * kc)
        p = jnp.exp2(_dot_nt(q_aug, ka_ref[ks, :])).astype(BF16)
        acc = acc + _dot(p, va_ref[ks, :])
    l = acc[:, HEAD_DIM:HEAD_DIM + 1]
    accurate = jnp.min(l) >= ATTN_MIN_DENOMINATOR

    def write(o):
        for g in range(GROUP):
            o_ref[:, g * HEAD_DIM:(g + 1) * HEAD_DIM] = o[g * tq:(g + 1) * tq].astype(BF16)

    @pl.when(accurate)
    def _():
        write(acc[:, :HEAD_DIM] * (1.0 / l))

    @pl.when(jnp.logical_not(accurate))
    def _():
        write(_attn_exact(q, ka_ref, va_ref, kc))


def _attention(qk_t, vu, *, batch, seq, q_width, v_col, tq, kc):
    t = vu.shape[0]
    gw = GROUP * HEAD_DIM
    k_blk0 = q_width // HEAD_DIM
    v_blk0 = v_col // HEAD_DIM
    nq = seq // tq
    return pl.pallas_call(
        functools.partial(_attn_kernel, kc=kc),
        grid=(batch, N_KV, nq),
        in_specs=[
            pl.BlockSpec((gw, tq), lambda b, h, i: (h, b * nq + i)),
            pl.BlockSpec((HEAD_DIM, seq), lambda b, h, i: (k_blk0 + h, b)),
            pl.BlockSpec((seq, HEAD_DIM), lambda b, h, i: (b, v_blk0 + h)),
        ],
        out_specs=pl.BlockSpec((tq, gw), lambda b, h, i: (b * nq + i, h)),
        out_shape=jax.ShapeDtypeStruct((t, q_width), BF16),
        scratch_shapes=[
            pltpu.VMEM((seq, 2 * HEAD_DIM), BF16),
            pltpu.VMEM((seq, 2 * HEAD_DIM), BF16),
            pltpu.VMEM((8, HEAD_DIM), F32),
        ],
        compiler_params=_params("parallel", "parallel", "arbitrary"),
        name="attention",
    )(qk_t, qk_t, vu)


def _four_chan_kernel(u_ref, dft_ref, o_ref, *, gdim):
    for g in range(u_ref.shape[1] // gdim):
        sl = slice(g * gdim, (g + 1) * gdim)
        r = _dot(u_ref[:, sl], dft_ref[...])
        o_ref[0, 0, :, sl] = r[:, :gdim].astype(BF16)
        o_ref[0, 1, :, sl] = r[:, gdim:].astype(BF16)


def _four_chan(uv, dft_c, *, batch, seq, f_width, tm):
    gdim = f_width // N_FOURIER_GROUPS
    ns = seq // tm
    return pl.pallas_call(
        functools.partial(_four_chan_kernel, gdim=gdim),
        grid=(batch, ns),
        in_specs=[
            pl.BlockSpec((tm, f_width), lambda b, i: (b * ns + i, 0)),
            pl.BlockSpec((gdim, 2 * gdim), lambda b, i: (0, 0)),
        ],
        out_specs=pl.BlockSpec((1, 2, tm, f_width), lambda b, i: (b, 0, i, 0)),
        out_shape=jax.ShapeDtypeStruct((batch, 2, seq, f_width), BF16),
        compiler_params=_params("parallel", "parallel"),
        name="fourier_chan",
    )(uv, dft_c)


def _four_seq_kernel(ch_ref, sh_ref, uc_ref, us_ref, flip_ref, o_ref):
    fr = o_ref.shape[3]
    p = _dot(ch_ref[0], uc_ref[0, 0])
    q = _dot(sh_ref[0], us_ref[0, 0])
    o_ref[0, 0, 0] = (p - q)[:fr].astype(BF16)
    o_ref[0, 1, 0] = _dot(flip_ref[...], (p + q).astype(BF16)).astype(BF16)


def _four_seq(ch, sh, flip, ucs, *, batch, seq, f_width):
    nt, rows, _ = ch.shape
    fr = flip.shape[0]
    half_spec = pl.BlockSpec((1, rows, seq), lambda b, t: (t, 0, 0))
    return pl.pallas_call(
        _four_seq_kernel,
        grid=(batch, nt),
        in_specs=[
            half_spec,
            half_spec,
            pl.BlockSpec((1, 1, seq, f_width), lambda b, t: (b, 0, 0, 0), pipeline_mode=pl.Buffered(1)),
            pl.BlockSpec((1, 1, seq, f_width), lambda b, t: (b, 1, 0, 0), pipeline_mode=pl.Buffered(1)),
            _resident(flip.shape),
        ],
        out_specs=pl.BlockSpec((1, 2, 1, fr, f_width), lambda b, t: (b, 0, t, 0, 0)),
        out_shape=jax.ShapeDtypeStruct((batch, 2, nt, fr, f_width), BF16),
        compiler_params=_params("parallel", "arbitrary"),
        name="fourier_seq",
    )(ch, sh, ucs, ucs, flip)


def _merge_kernel(att_ref, y_ref, g1_ref, g2_ref, x_ref, wa_ref, wf_ref, wm_ref, gp_ref, o_ref):
    a = _dot(att_ref[...], wa_ref[...])
    f = _dot(y_ref[...], wf_ref[...])
    merged = g1_ref[...].astype(F32) * a + g2_ref[...].astype(F32) * f
    y = _dot(merged.astype(BF16), wm_ref[...])
    o_ref[...] = x_ref[...] + _rms(y, gp_ref[...])


def _resident(shape):
    return pl.BlockSpec(shape, lambda *_: (0,) * len(shape), pipeline_mode=pl.Buffered(1))


def _merge(att, four, gates, x, wa, wf, wm, g_post, *, seq, tm):
    t, d = x.shape
    tiles = seq // tm

    def four_block(i):
        it = i % tiles
        return (i - it + jnp.where(it < tiles // 2, it, tiles + tiles // 2 - 1 - it), 0)

    return pl.pallas_call(
        _merge_kernel,
        grid=(t // tm,),
        in_specs=[
            pl.BlockSpec((tm, att.shape[1]), lambda i: (i, 0)),
            pl.BlockSpec((tm, four.shape[1]), four_block),
            pl.BlockSpec((tm, d), lambda i: (i, 0)),
            pl.BlockSpec((tm, d), lambda i: (i, 1)),
            pl.BlockSpec((tm, d), lambda i: (i, 0)),
            _resident(wa.shape),
            _resident(wf.shape),
            _resident(wm.shape),
            _resident(g_post.shape),
        ],
        out_specs=pl.BlockSpec((tm, d), lambda i: (i, 0)),
        out_shape=jax.ShapeDtypeStruct((t, d), F32),
        compiler_params=_params("parallel"),
        name="merge",
    )(att, four, gates, gates, x, wa, wf, wm, g_post)


def _norm_matmul_kernel(x_ref, g_ref, w_ref, o_ref):
    hn = _rms(x_ref[...], g_ref[...]).astype(BF16)
    o_ref[...] = _dot(hn, w_ref[...]).astype(BF16)


def _norm_matmul(x, g, w, *, tm):
    t, d = x.shape
    n = w.shape[1]
    return pl.pallas_call(
        _norm_matmul_kernel,
        grid=(t // tm,),
        in_specs=[
            pl.BlockSpec((tm, d), lambda i: (i, 0)),
            _resident(g.shape),
            _resident(w.shape),
        ],
        out_specs=pl.BlockSpec((tm, n), lambda i: (i, 0)),
        out_shape=jax.ShapeDtypeStruct((t, n), BF16),
        compiler_params=_params("parallel"),
        name="mem_kv",
    )(x, g, w)


def _xattn_kernel(x_ref, gpre_ref, wq_ref, k_ref, v_ref, wo_ref, gpost_ref, o_ref):
    x = x_ref[...]
    hn = _rms(x, gpre_ref[...]).astype(BF16)
    q = (_dot(hn, wq_ref[...]) * (X_HEAD_DIM ** -0.5)).astype(BF16)
    outs = []
    for h in range(X_HEADS):
        sl = slice(h * X_HEAD_DIM, (h + 1) * X_HEAD_DIM)
        s = _dot_nt(q[:, sl], k_ref[:, sl])
        m = jnp.max(s, axis=-1, keepdims=True)
        p = jnp.exp(s - m)
        l = jnp.sum(p, axis=-1, keepdims=True)
        outs.append(_dot(p.astype(BF16), v_ref[:, sl]) * (1.0 / l))
    o = jnp.concatenate(outs, axis=1).astype(BF16)
    y = _dot(o, wo_ref[...])
    o_ref[...] = x + _rms(y, gpost_ref[...])


def _xattn(x, g_pre, wq, kv, wo, g_post, *, batch, seq, mem_tokens, tm):
    t, d = x.shape
    xw = wq.shape[1]
    ns = seq // tm
    return pl.pallas_call(
        _xattn_kernel,
        grid=(batch, ns),
        in_specs=[
            pl.BlockSpec((tm, d), lambda b, i: (b * ns + i, 0)),
            _resident(g_pre.shape),
            _resident(wq.shape),
            pl.BlockSpec((mem_tokens, xw), lambda b, i: (b, 0)),
            pl.BlockSpec((mem_tokens, xw), lambda b, i: (b, 1)),
            _resident(wo.shape),
            _resident(g_post.shape),
        ],
        out_specs=pl.BlockSpec((tm, d), lambda b, i: (b * ns + i, 0)),
        out_shape=jax.ShapeDtypeStruct((t, d), F32),
        compiler_params=_params("parallel", "parallel"),
        name="xattn",
    )(x, g_pre, wq, kv, kv, wo, g_post)


def _gelu_tanh(x):
    return 0.5 * x * (1.0 + jnp.tanh(math.sqrt(2.0 / math.pi) * (x + 0.044715 * (x * x * x))))


def _ffn_kernel(xp_ref, x_ref, xn_ref, gpre_ref, wg_ref, wv_ref, cwg_ref, cwv_ref, cbg_ref, cbv_ref,
                wd_ref, gpost_ref, o_ref, hn_ref, ug_ref, uv_ref, *, tiles_per_seq):
    i = pl.program_id(0)
    j = pl.program_id(1)
    tm = x_ref.shape[0]
    halo = xp_ref.shape[0]

    @pl.when(j == 0)
    def _():
        g = gpre_ref[...]
        pos = i % tiles_per_seq
        hp = jnp.where(pos == 0, 0.0, _rms(xp_ref[...], g))
        hx = jnp.where(pos == tiles_per_seq - 1, 0.0, _rms(xn_ref[...], g))
        hn_ref[0:halo, :] = hp.astype(BF16)
        hn_ref[halo + tm:, :] = hx.astype(BF16)
        for r in range(0, tm, FFN_NORM_ROWS):
            rows = slice(r, r + FFN_NORM_ROWS)
            hn_ref[halo + r:halo + r + FFN_NORM_ROWS, :] = _rms(x_ref[rows, :], g).astype(BF16)
        o_ref[...] = jnp.zeros_like(o_ref)

    hn = hn_ref[...]
    ug_ref[...] = _dot(hn, wg_ref[...])
    uv_ref[...] = _dot(hn, wv_ref[...])

    def conv(u_ref, cw_ref, cb_ref):
        return (u_ref[halo - 1:halo - 1 + tm, :] * cw_ref[0:1, :]
                + u_ref[halo:halo + tm, :] * cw_ref[1:2, :]
                + u_ref[halo + 1:halo + 1 + tm, :] * cw_ref[2:3, :]
                + cb_ref[...])

    act = _gelu_tanh(conv(ug_ref, cwg_ref, cbg_ref)) * conv(uv_ref, cwv_ref, cbv_ref)
    o_ref[...] += _dot(act.astype(BF16), wd_ref[...])

    @pl.when(j == pl.num_programs(1) - 1)
    def _():
        for r in range(0, tm, FFN_NORM_ROWS):
            rows = slice(r, r + FFN_NORM_ROWS)
            o_ref[rows, :] = x_ref[rows, :] + _rms(o_ref[rows, :], gpost_ref[...])


def _ffn(x, g_pre, w_up, conv_w, conv_b, w_down, g_post, *, seq, tm, tf):
    t, d = x.shape
    d_ff = w_down.shape[0]
    nf = d_ff // tf
    halo = BF16_SUBLANES
    hb = tm // halo
    n_hblk = t // halo
    kern = functools.partial(_ffn_kernel, tiles_per_seq=seq // tm)
    return pl.pallas_call(
        kern,
        grid=(t // tm, nf),
        in_specs=[
            pl.BlockSpec((halo, d), lambda i, j: (jnp.maximum(i * hb - 1, 0), 0)),
            pl.BlockSpec((tm, d), lambda i, j: (i, 0), pipeline_mode=pl.Buffered(1)),
            pl.BlockSpec((halo, d), lambda i, j: (jnp.minimum((i + 1) * hb, n_hblk - 1), 0)),
            pl.BlockSpec((1, d), lambda i, j: (0, 0)),
            pl.BlockSpec((d, tf), lambda i, j: (0, j)),
            pl.BlockSpec((d, tf), lambda i, j: (0, nf + j)),
            pl.BlockSpec((3, tf), lambda i, j: (0, j)),
            pl.BlockSpec((3, tf), lambda i, j: (0, nf + j)),
            pl.BlockSpec((1, tf), lambda i, j: (0, j)),
            pl.BlockSpec((1, tf), lambda i, j: (0, nf + j)),
            pl.BlockSpec((tf, d), lambda i, j: (j, 0)),
            pl.BlockSpec((1, d), lambda i, j: (0, 0)),
        ],
        out_specs=pl.BlockSpec((tm, d), lambda i, j: (i, 0)),
        out_shape=jax.ShapeDtypeStruct((t, d), F32),
        scratch_shapes=[
            pltpu.VMEM((tm + 2 * halo, d), BF16),
            pltpu.VMEM((tm + 2 * halo, tf), F32),
            pltpu.VMEM((tm + 2 * halo, tf), F32),
        ],
        compiler_params=_params("parallel", "arbitrary"),
        name="conv_ffn",
    )(x, x, x, g_pre, w_up, w_up, conv_w, conv_w, conv_b, conv_b, w_down, g_post)


def _head_lane_order():
    quarter = HEAD_DIM // 4
    blocks = (0, 2, 1, 3)
    return jnp.concatenate([jnp.arange(b * quarter, (b + 1) * quarter) for b in blocks])


def _rope_tables(seq):
    half = HEAD_DIM // 2
    pos = jnp.arange(seq)
    row = (pos // GRID_W).astype(F32)
    col = (pos % GRID_W).astype(F32)
    inv_freq = 1.0 / (ROPE_THETA ** (jnp.arange(0, half, 2, dtype=F32) / half))
    lane = _head_lane_order()
    freq = inv_freq[(lane % half) % (half // 2)]
    ids = jnp.where(lane[None, :] < half, row[:, None], col[:, None])
    ang = ids * freq[None, :]
    sign = jnp.where((lane % half) < half // 2, -1.0, 1.0)
    return jnp.cos(ang), jnp.sin(ang) * sign[None, :]


def _dft_tables(seq, gdim, fr):
    def cs(k, n, period):
        r = (k[..., None] * n) % period
        ang = r.astype(F32) * (2.0 * math.pi / period)
        return jnp.cos(ang), jnp.sin(ang)
    c = jnp.arange(gdim, dtype=jnp.int32)
    cc, sc = (t * gdim ** -0.5 for t in cs(c, c, gdim))
    rows = fr + FOUR_EXTRA_ROWS
    k = jnp.arange(seq // 2 // fr, dtype=jnp.int32)[:, None] * fr + jnp.arange(rows, dtype=jnp.int32)
    ca, sa = cs(k, GRID_W * jnp.arange(seq // GRID_W, dtype=jnp.int32), seq)
    cb, sb = cs(k, jnp.arange(GRID_W, dtype=jnp.int32), seq)
    scale = seq ** -0.5
    ch = ((ca[..., :, None] * cb[..., None, :] - sa[..., :, None] * sb[..., None, :]) * scale)
    sh = ((sa[..., :, None] * cb[..., None, :] + ca[..., :, None] * sb[..., None, :]) * scale)
    ch, sh = ch.reshape(*k.shape, seq), sh.reshape(*k.shape, seq)
    flip = (jnp.arange(rows)[None, :] == fr - jnp.arange(fr)[:, None]).astype(BF16)
    return (jnp.concatenate([cc, sc], axis=1).astype(BF16), ch.astype(BF16), sh.astype(BF16), flip)


def _tiles(seq, d_ff):
    def pick(n, pref):
        return pref if n % pref == 0 else n
    tf = next(c for c in (512, 256, 128) if d_ff % c == 0)
    return dict(
        qk_tm=pick(seq, 512), mixer_tm=pick(seq, 1024), uv_tn=768, gate_tn=1024,
        attn_tq=pick(seq, 512), attn_kc=pick(seq, 256),
        four_tm=pick(seq, 512),
        merge_tm=pick(seq, 256),
        xattn_tm=pick(seq, 512),
        ffn_tm=pick(seq, 1024), ffn_tf=tf,
    )


def kernel(x, mem, mix_pre_g, w_in, q_norm_g, k_norm_g, w_attn_o, w_four_o, w_gate, b_gate, w_mix_o,
           mix_post_g, xa_pre_g, mem_norm_g, w_xq, w_xkv, w_xo, xa_post_g, ffn_pre_g, w_up, conv_w,
           conv_b, w_down, ffn_post_g):
    batch, seq, d = x.shape
    mem_tokens = mem.shape[1]
    depth = w_in.shape[0]
    q_width = w_attn_o.shape[1]
    f_width = w_four_o.shape[1]
    kv_width = (w_in.shape[2] - q_width - f_width) // 2
    n_heads = q_width // HEAD_DIM
    assert n_heads == N_KV * GROUP and kv_width == N_KV * HEAD_DIM
    assert seq % GRID_W == 0 and w_xq.shape[2] == X_HEADS * X_HEAD_DIM
    d_ff = w_down.shape[1]
    ts = _tiles(seq, d_ff)
    qk_width = q_width + kv_width
    u_col = q_width + 2 * kv_width
    tm = ts["mixer_tm"]
    assert (kv_width + f_width) % ts["uv_tn"] == 0 and w_gate.shape[2] % ts["gate_tn"] == 0

    cos_t, sin_t = _rope_tables(seq)
    assert seq % (2 * ts["merge_tm"]) == 0
    dft_c, dft_ch, dft_sh, dft_flip = _dft_tables(seq, f_width // N_FOURIER_GROUPS, ts["merge_tm"])
    order = _head_lane_order()
    qk_cols = (jnp.arange(qk_width // HEAD_DIM)[:, None] * HEAD_DIM + order[None, :]).reshape(-1)
    partner = jnp.roll(jnp.arange(HEAD_DIM), HEAD_DIM // 2)
    bias_spec = pl.BlockSpec((1, ts["gate_tn"]), lambda i, j: (0, j))

    row = lambda v: v.reshape(1, -1)
    xf = x.reshape(batch * seq, d)
    memf = mem.reshape(batch * mem_tokens, d)

    for l in range(depth):
        w_qk_t = w_in[l][:, qk_cols].T.astype(BF16)
        w_uv = jnp.concatenate([w_in[l][:, u_col:], w_in[l][:, qk_width:u_col]], axis=1).astype(BF16)
        q_gain = q_norm_g[l][order] * (HEAD_DIM ** -0.5 * math.log2(math.e))
        k_gain = k_norm_g[l][order]
        cg = jnp.concatenate([g[:, None] * cos_t.T for g in (q_gain, k_gain)], axis=0)
        sg = jnp.concatenate([g[partner][:, None] * sin_t.T for g in (q_gain, k_gain)], axis=0)
        g_pre = row(mix_pre_g[l])
        qk = _proj_qk_t(xf, g_pre, w_qk_t, cg, sg, seq=seq, n_q_heads=n_heads, tm=ts["qk_tm"])
        uv = _proj(xf, g_pre, w_uv, _epilogue_plain, (), (), tm=tm, tn=ts["uv_tn"], name="proj_uv")
        gates = _proj(xf, g_pre, w_gate[l].astype(BF16), _epilogue_gate, (row(b_gate[l]),),
                      (bias_spec,), tm=tm, tn=ts["gate_tn"], name="proj_gate")
        att = _attention(qk, uv, batch=batch, seq=seq, q_width=q_width, v_col=f_width,
                         tq=ts["attn_tq"], kc=ts["attn_kc"])
        ucs = _four_chan(uv, dft_c, batch=batch, seq=seq, f_width=f_width, tm=ts["four_tm"])
        four = _four_seq(dft_ch, dft_sh, dft_flip, ucs, batch=batch, seq=seq, f_width=f_width)
        xf = _merge(att, four.reshape(batch * seq, f_width), gates, xf, w_attn_o[l].astype(BF16),
                    w_four_o[l].astype(BF16), w_mix_o[l].astype(BF16), row(mix_post_g[l]),
                    seq=seq, tm=ts["merge_tm"])

        kv = _norm_matmul(memf, row(mem_norm_g[l]), w_xkv[l].astype(BF16), tm=mem_tokens)
        xf = _xattn(xf, row(xa_pre_g[l]), w_xq[l].astype(BF16), kv, w_xo[l].astype(BF16),
                    row(xa_post_g[l]), batch=batch, seq=seq, mem_tokens=mem_tokens, tm=ts["xattn_tm"])

        xf = _ffn(xf, row(ffn_pre_g[l]), w_up[l].astype(BF16), conv_w[l], row(conv_b[l]),
                  w_down[l].astype(BF16), row(ffn_post_g[l]), seq=seq, tm=ts["ffn_tm"], tf=ts["ffn_tf"])

    return xf.reshape(batch, seq, d)
```

```python
import functools
import math

import jax
import jax.numpy as jnp
from jax import lax
from jax.experimental import pallas as pl
from jax.experimental.pallas import tpu as pltpu

HEAD_DIM = 128
N_KV = 4
GROUP = 4
GRID_W = 64
ROPE_THETA = 10000.0
N_FOURIER_GROUPS = 4
X_HEADS = 4
X_HEAD_DIM = 128
EPS = 1e-6

V7X_VMEM_LIMIT_BYTES = 56 * 1024 * 1024
BF16_SUBLANES = 16
ATTN_MIN_DENOMINATOR = 2.0 ** -60
FOUR_EXTRA_ROWS = BF16_SUBLANES
FFN_NORM_ROWS = 128
FFN_SPLITS = 1

F32 = jnp.float32
BF16 = jnp.bfloat16


def _params(*sem):
    return pltpu.CompilerParams(dimension_semantics=sem, vmem_limit_bytes=V7X_VMEM_LIMIT_BYTES)


def _rms(xf, g):
    ms = jnp.mean(xf * xf, axis=-1, keepdims=True)
    return xf * lax.rsqrt(ms + EPS) * g


def _dot(a, b):
    return jnp.dot(a, b, preferred_element_type=F32)


def _dot_nt(a, b):
    return lax.dot_general(a, b, (((1,), (1,)), ((), ())), preferred_element_type=F32)


def _epilogue_plain(z, o_ref):
    o_ref[...] = z.astype(BF16)


def _epilogue_gate(z, b_ref, o_ref):
    o_ref[...] = (0.5 * jnp.tanh(0.5 * (z + b_ref[...])) + 0.5).astype(BF16)


def _proj_qk_t_kernel(x_ref, g_ref, wt_ref, cg_ref, sg_ref, o_ref, *, n_q_heads):
    hn = _rms(x_ref[...], g_ref[...]).astype(BF16)
    zt = _dot_nt(wt_ref[...], hn)
    half = HEAD_DIM // 2
    for h in range(zt.shape[0] // HEAD_DIM):
        tab = slice(0, HEAD_DIM) if h < n_q_heads else slice(HEAD_DIM, 2 * HEAD_DIM)
        y = zt[h * HEAD_DIM:(h + 1) * HEAD_DIM, :]
        y = y * lax.rsqrt(jnp.mean(y * y, axis=0, keepdims=True) + EPS)
        rot = jnp.concatenate([y[half:], y[:half]], axis=0)
        o_ref[h * HEAD_DIM:(h + 1) * HEAD_DIM, :] = (y * cg_ref[tab, :] + rot * sg_ref[tab, :]).astype(BF16)


def _proj_qk_t(x, g_pre, w_t, cg, sg, *, seq, n_q_heads, tm):
    t, d = x.shape
    n = w_t.shape[0]
    seq_tiles = seq // tm
    table_spec = pl.BlockSpec((2 * HEAD_DIM, tm), lambda i: (0, i % seq_tiles))
    return pl.pallas_call(
        functools.partial(_proj_qk_t_kernel, n_q_heads=n_q_heads),
        grid=(t // tm,),
        in_specs=[
            pl.BlockSpec((tm, d), lambda i: (i, 0)),
            _resident(g_pre.shape),
            _resident(w_t.shape),
            table_spec,
            table_spec,
        ],
        out_specs=pl.BlockSpec((n, tm), lambda i: (0, i)),
        out_shape=jax.ShapeDtypeStruct((n, t), BF16),
        compiler_params=_params("parallel"),
        name="proj_qk",
    )(x, g_pre, w_t, cg, sg)


def _proj_kernel(x_ref, g_ref, w_ref, *rest, epilogue):
    hn_ref = rest[-1]

    @pl.when(pl.program_id(1) == 0)
    def _():
        hn_ref[...] = _rms(x_ref[...], g_ref[...]).astype(BF16)

    epilogue(_dot(hn_ref[...], w_ref[...]), *rest[:-1])


def _proj(x, g_pre, w, epilogue, extra, extra_specs, *, tm, tn, name):
    t, d = x.shape
    n = w.shape[1]
    return pl.pallas_call(
        functools.partial(_proj_kernel, epilogue=epilogue),
        grid=(t // tm, n // tn),
        in_specs=[
            pl.BlockSpec((tm, d), lambda i, j: (i, 0)),
            pl.BlockSpec((1, d), lambda i, j: (0, 0)),
            pl.BlockSpec((d, tn), lambda i, j: (0, j)),
            *extra_specs,
        ],
        out_specs=pl.BlockSpec((tm, tn), lambda i, j: (i, j)),
        out_shape=jax.ShapeDtypeStruct((t, n), BF16),
        scratch_shapes=[pltpu.VMEM((tm, d), BF16)],
        compiler_params=_params("parallel", "arbitrary"),
        name=name,
    )(x, g_pre, w, *extra)


def _attn_exact(q, ka_ref, va_ref, kc):
    rows = q.shape[0]
    m = jnp.full((rows, 1), -jnp.inf, F32)
    l = jnp.zeros((rows, 1), F32)
    acc = jnp.zeros((rows, HEAD_DIM), F32)
    for c in range(ka_ref.shape[0] // kc):
        ks = slice(c * kc, (c + 1) * kc)
        s = _dot_nt(q, ka_ref[ks, :HEAD_DIM])
        m_new = jnp.maximum(m, jnp.max(s, axis=-1, keepdims=True))
        alpha = jnp.exp2(m - m_new)
        p = jnp.exp2(s - m_new)
        l = alpha * l + jnp.sum(p, axis=-1, keepdims=True)
        acc = alpha * acc + _dot(p.astype(BF16), va_ref[ks, :HEAD_DIM])
        m = m_new
    return acc * (1.0 / l)


def _attn_kernel(qt_ref, kt_ref, v_ref, o_ref, ka_ref, va_ref, kmax_ref, *, kc):
    tq = qt_ref.shape[1]
    s_len = v_ref.shape[0]
    n_chunks = s_len // kc
    lane = lax.broadcasted_iota(jnp.int32, (1, HEAD_DIM), 1)

    @pl.when(pl.program_id(2) == 0)
    def _():
        ones_col = jnp.broadcast_to(jnp.where(lane == 0, 1.0, 0.0), (kc, HEAD_DIM)).astype(BF16)
        k2max = jnp.zeros((1, 1), F32)
        for c in range(n_chunks):
            ks = slice(c * kc, (c + 1) * kc)
            kt = kt_ref[:, ks]
            kf = kt.astype(F32)
            k2max = jnp.maximum(k2max, jnp.max(jnp.sum(kf * kf, axis=0, keepdims=True),
                                               axis=1, keepdims=True))
            ka_ref[ks, :HEAD_DIM] = kt.T
            ka_ref[ks, HEAD_DIM:] = ones_col
            va_ref[ks, :HEAD_DIM] = v_ref[ks, :]
            va_ref[ks, HEAD_DIM:] = ones_col
        kmax_ref[...] = jnp.broadcast_to(jnp.sqrt(k2max), kmax_ref.shape)

    q = jnp.concatenate(
        [qt_ref[g * HEAD_DIM:(g + 1) * HEAD_DIM, :].T for g in range(GROUP)], axis=0)
    qf = q.astype(F32)
    shift = jnp.sqrt(jnp.sum(qf * qf, axis=-1, keepdims=True)) * kmax_ref[0:1, 0:1]
    shift_col = (-shift) * jnp.where(lane == 0, 1.0, 0.0)
    q_aug = jnp.concatenate([q, shift_col.astype(BF16)], axis=1)
    acc = jnp.zeros((GROUP * tq, 2 * HEAD_DIM), F32)
    for c in range(n_chunks):
        ks = slice(c * kc, (c + 1) * kc)
        p = jnp.exp2(_dot_nt(q_aug, ka_ref[ks, :])).astype(BF16)
        acc = acc + _dot(p, va_ref[ks, :])
    l = acc[:, HEAD_DIM:HEAD_DIM + 1]
    accurate = jnp.min(l) >= ATTN_MIN_DENOMINATOR

    def write(o):
        for g in range(GROUP):
            o_ref[:, g * HEAD_DIM:(g + 1) * HEAD_DIM] = o[g * tq:(g + 1) * tq].astype(BF16)

    @pl.when(accurate)
    def _():
        write(acc[:, :HEAD_DIM] * (1.0 / l))

    @pl.when(jnp.logical_not(accurate))
    def _():
        write(_attn_exact(q, ka_ref, va_ref, kc))


def _attention(qk_t, vu, *, batch, seq, q_width, v_col, tq, kc):
    t = vu.shape[0]
    gw = GROUP * HEAD_DIM
    k_blk0 = q_width // HEAD_DIM
    v_blk0 = v_col // HEAD_DIM
    nq = seq // tq
    return pl.pallas_call(
        functools.partial(_attn_kernel, kc=kc),
        grid=(batch, N_KV, nq),
        in_specs=[
            pl.BlockSpec((gw, tq), lambda b, h, i: (h, b * nq + i)),
            pl.BlockSpec((HEAD_DIM, seq), lambda b, h, i: (k_blk0 + h, b)),
            pl.BlockSpec((seq, HEAD_DIM), lambda b, h, i: (b, v_blk0 + h)),
        ],
        out_specs=pl.BlockSpec((tq, gw), lambda b, h, i: (b * nq + i, h)),
        out_shape=jax.ShapeDtypeStruct((t, q_width), BF16),
        scratch_shapes=[
            pltpu.VMEM((seq, 2 * HEAD_DIM), BF16),
            pltpu.VMEM((seq, 2 * HEAD_DIM), BF16),
            pltpu.VMEM((8, HEAD_DIM), F32),
        ],
        compiler_params=_params("parallel", "parallel", "arbitrary"),
        name="attention",
    )(qk_t, qk_t, vu)


def _four_chan_kernel(u_ref, dft_ref, o_ref, *, gdim):
    for g in range(u_ref.shape[1] // gdim):
        sl = slice(g * gdim, (g + 1) * gdim)
        r = _dot(u_ref[:, sl], dft_ref[...])
        o_ref[0, 0, :, sl] = r[:, :gdim].astype(BF16)
        o_ref[0, 1, :, sl] = r[:, gdim:].astype(BF16)


def _four_chan(uv, dft_c, *, batch, seq, f_width, tm):
    gdim = f_width // N_FOURIER_GROUPS
    ns = seq // tm
    return pl.pallas_call(
        functools.partial(_four_chan_kernel, gdim=gdim),
        grid=(batch, ns),
        in_specs=[
            pl.BlockSpec((tm, f_width), lambda b, i: (b * ns + i, 0)),
            pl.BlockSpec((gdim, 2 * gdim), lambda b, i: (0, 0)),
        ],
        out_specs=pl.BlockSpec((1, 2, tm, f_width), lambda b, i: (b, 0, i, 0)),
        out_shape=jax.ShapeDtypeStruct((batch, 2, seq, f_width), BF16),
        compiler_params=_params("parallel", "parallel"),
        name="fourier_chan",
    )(uv, dft_c)


def _four_seq_kernel(ch_ref, sh_ref, uc_ref, us_ref, flip_ref, o_ref):
    fr = o_ref.shape[3]
    p = _dot(ch_ref[0], uc_ref[0, 0])
    q = _dot(sh_ref[0], us_ref[0, 0])
    o_ref[0, 0, 0] = (p - q)[:fr].astype(BF16)
    o_ref[0, 1, 0] = _dot(flip_ref[...], (p + q).astype(BF16)).astype(BF16)


def _four_seq(ch, sh, flip, ucs, *, batch, seq, f_width):
    nt, rows, _ = ch.shape
    fr = flip.shape[0]
    half_spec = pl.BlockSpec((1, rows, seq), lambda b, t: (t, 0, 0))
    return pl.pallas_call(
        _four_seq_kernel,
        grid=(batch, nt),
        in_specs=[
            half_spec,
            half_spec,
            pl.BlockSpec((1, 1, seq, f_width), lambda b, t: (b, 0, 0, 0), pipeline_mode=pl.Buffered(1)),
            pl.BlockSpec((1, 1, seq, f_width), lambda b, t: (b, 1, 0, 0), pipeline_mode=pl.Buffered(1)),
            _resident(flip.shape),
        ],
        out_specs=pl.BlockSpec((1, 2, 1, fr, f_width), lambda b, t: (b, 0, t, 0, 0)),
        out_shape=jax.ShapeDtypeStruct((batch, 2, nt, fr, f_width), BF16),
        compiler_params=_params("parallel", "arbitrary"),
        name="fourier_seq",
    )(ch, sh, ucs, ucs, flip)


def _merge_kernel(att_ref, y_ref, g1_ref, g2_ref, x_ref, wa_ref, wf_ref, wm_ref, gp_ref, o_ref):
    a = _dot(att_ref[...], wa_ref[...])
    f = _dot(y_ref[...], wf_ref[...])
    merged = g1_ref[...].astype(F32) * a + g2_ref[...].astype(F32) * f
    y = _dot(merged.astype(BF16), wm_ref[...])
    o_ref[...] = x_ref[...] + _rms(y, gp_ref[...])


def _resident(shape):
    return pl.BlockSpec(shape, lambda *_: (0,) * len(shape), pipeline_mode=pl.Buffered(1))


def _merge(att, four, gates, x, wa, wf, wm, g_post, *, seq, tm):
    t, d = x.shape
    tiles = seq // tm

    def four_block(i):
        it = i % tiles
        return (i - it + jnp.where(it < tiles // 2, it, tiles + tiles // 2 - 1 - it), 0)

    return pl.pallas_call(
        _merge_kernel,
        grid=(t // tm,),
        in_specs=[
            pl.BlockSpec((tm, att.shape[1]), lambda i: (i, 0)),
            pl.BlockSpec((tm, four.shape[1]), four_block),
            pl.BlockSpec((tm, d), lambda i: (i, 0)),
            pl.BlockSpec((tm, d), lambda i: (i, 1)),
            pl.BlockSpec((tm, d), lambda i: (i, 0)),
            _resident(wa.shape),
            _resident(wf.shape),
            _resident(wm.shape),
            _resident(g_post.shape),
        ],
        out_specs=pl.BlockSpec((tm, d), lambda i: (i, 0)),
        out_shape=jax.ShapeDtypeStruct((t, d), F32),
        compiler_params=_params("parallel"),
        name="merge",
    )(att, four, gates, gates, x, wa, wf, wm, g_post)


def _norm_matmul_kernel(x_ref, g_ref, w_ref, o_ref):
    hn = _rms(x_ref[...], g_ref[...]).astype(BF16)
    o_ref[...] = _dot(hn, w_ref[...]).astype(BF16)


def _norm_matmul(x, g, w, *, tm):
    t, d = x.shape
    n = w.shape[1]
    return pl.pallas_call(
        _norm_matmul_kernel,
        grid=(t // tm,),
        in_specs=[
            pl.BlockSpec((tm, d), lambda i: (i, 0)),
            _resident(g.shape),
            _resident(w.shape),
        ],
        out_specs=pl.BlockSpec((tm, n), lambda i: (i, 0)),
        out_shape=jax.ShapeDtypeStruct((t, n), BF16),
        compiler_params=_params("parallel"),
        name="mem_kv",
    )(x, g, w)


def _xattn_kernel(x_ref, gpre_ref, wq_ref, k_ref, v_ref, wo_ref, gpost_ref, o_ref):
    x = x_ref[...]
    hn = _rms(x, gpre_ref[...]).astype(BF16)
    q = (_dot(hn, wq_ref[...]) * (X_HEAD_DIM ** -0.5)).astype(BF16)
    outs = []
    for h in range(X_HEADS):
        sl = slice(h * X_HEAD_DIM, (h + 1) * X_HEAD_DIM)
        s = _dot_nt(q[:, sl], k_ref[:, sl])
        m = jnp.max(s, axis=-1, keepdims=True)
        p = jnp.exp(s - m)
        l = jnp.sum(p, axis=-1, keepdims=True)
        outs.append(_dot(p.astype(BF16), v_ref[:, sl]) * (1.0 / l))
    o = jnp.concatenate(outs, axis=1).astype(BF16)
    y = _dot(o, wo_ref[...])
    o_ref[...] = x + _rms(y, gpost_ref[...])


def _xattn(x, g_pre, wq, kv, wo, g_post, *, batch, seq, mem_tokens, tm):
    t, d = x.shape
    xw = wq.shape[1]
    ns = seq // tm
    return pl.pallas_call(
        _xattn_kernel,
        grid=(batch, ns),
        in_specs=[
            pl.BlockSpec((tm, d), lambda b, i: (b * ns + i, 0)),
            _resident(g_pre.shape),
            _resident(wq.shape),
            pl.BlockSpec((mem_tokens, xw), lambda b, i: (b, 0)),
            pl.BlockSpec((mem_tokens, xw), lambda b, i: (b, 1)),
            _resident(wo.shape),
            _resident(g_post.shape),
        ],
        out_specs=pl.BlockSpec((tm, d), lambda b, i: (b * ns + i, 0)),
        out_shape=jax.ShapeDtypeStruct((t, d), F32),
        compiler_params=_params("parallel", "parallel"),
        name="xattn",
    )(x, g_pre, wq, kv, kv, wo, g_post)


def _gelu_tanh(x):
    return 0.5 * x * (1.0 + jnp.tanh(math.sqrt(2.0 / math.pi) * (x + 0.044715 * (x * x * x))))


def _ffn_kernel(xp_ref, x_ref, xn_ref, gpre_ref, wu_ref, cw_ref, cb_ref, wd_ref, gpost_ref, o_ref,
                hn_ref, u_ref, *, tiles_per_seq):
    i = pl.program_id(0)
    j = pl.program_id(1)
    tm = x_ref.shape[0]
    halo = xp_ref.shape[0]
    tc = wd_ref.shape[0] // FFN_SPLITS

    @pl.when(j == 0)
    def _():
        g = gpre_ref[...]
        pos = i % tiles_per_seq
        hp = jnp.where(pos == 0, 0.0, _rms(xp_ref[...], g))
        hx = jnp.where(pos == tiles_per_seq - 1, 0.0, _rms(xn_ref[...], g))
        hn_ref[0:halo, :] = hp.astype(BF16)
        hn_ref[halo + tm:, :] = hx.astype(BF16)
        for r in range(0, tm, FFN_NORM_ROWS):
            rows = slice(r, r + FFN_NORM_ROWS)
            hn_ref[halo + r:halo + r + FFN_NORM_ROWS, :] = _rms(x_ref[rows, :], g).astype(BF16)
        o_ref[...] = jnp.zeros_like(o_ref)

    hn = hn_ref[...]
    for h in range(FFN_SPLITS):
        cols = slice(2 * tc * h, 2 * tc * (h + 1))
        u_ref[h] = _dot(hn, wu_ref[:, cols])
        c = cb_ref[:, cols]
        for k in range(3):
            c = c + u_ref[h, halo - 1 + k:halo - 1 + k + tm, :] * cw_ref[k:k + 1, cols]
        act = _gelu_tanh(c[:, :tc]) * c[:, tc:]
        o_ref[...] += _dot(act.astype(BF16), wd_ref[h * tc:(h + 1) * tc, :])

    @pl.when(j == pl.num_programs(1) - 1)
    def _():
        for r in range(0, tm, FFN_NORM_ROWS):
            rows = slice(r, r + FFN_NORM_ROWS)
            o_ref[rows, :] = x_ref[rows, :] + _rms(o_ref[rows, :], gpost_ref[...])


def _ffn_pair_columns(a, d_ff, tf):
    tc = tf // FFN_SPLITS
    lead = a.shape[:-1]
    pairs = a.reshape(*lead, 2, d_ff // tc, tc)
    return jnp.swapaxes(pairs, -3, -2).reshape(*lead, 2 * d_ff)


def _ffn(x, g_pre, w_up, conv_w, conv_b, w_down, g_post, *, seq, tm, tf):
    t, d = x.shape
    d_ff = w_down.shape[0]
    nf = d_ff // tf
    halo = BF16_SUBLANES
    hb = tm // halo
    n_hblk = t // halo
    kern = functools.partial(_ffn_kernel, tiles_per_seq=seq // tm)
    return pl.pallas_call(
        kern,
        grid=(t // tm, nf),
        in_specs=[
            pl.BlockSpec((halo, d), lambda i, j: (jnp.maximum(i * hb - 1, 0), 0)),
            pl.BlockSpec((tm, d), lambda i, j: (i, 0)),
            pl.BlockSpec((halo, d), lambda i, j: (jnp.minimum((i + 1) * hb, n_hblk - 1), 0)),
            pl.BlockSpec((1, d), lambda i, j: (0, 0)),
            pl.BlockSpec((d, 2 * tf), lambda i, j: (0, j)),
            pl.BlockSpec((3, 2 * tf), lambda i, j: (0, j)),
            pl.BlockSpec((1, 2 * tf), lambda i, j: (0, j)),
            pl.BlockSpec((tf, d), lambda i, j: (j, 0)),
            pl.BlockSpec((1, d), lambda i, j: (0, 0)),
        ],
        out_specs=pl.BlockSpec((tm, d), lambda i, j: (i, 0)),
        out_shape=jax.ShapeDtypeStruct((t, d), F32),
        scratch_shapes=[
            pltpu.VMEM((tm + 2 * halo, d), BF16),
            pltpu.VMEM((FFN_SPLITS, tm + 2 * halo, 2 * tf // FFN_SPLITS), F32),
        ],
        compiler_params=_params("parallel", "arbitrary"),
        name="conv_ffn",
    )(x, x, x, g_pre, w_up, conv_w, conv_b, w_down, g_post)


def _head_lane_order():
    quarter = HEAD_DIM // 4
    blocks = (0, 2, 1, 3)
    return jnp.concatenate([jnp.arange(b * quarter, (b + 1) * quarter) for b in blocks])


def _rope_tables(seq):
    half = HEAD_DIM // 2
    pos = jnp.arange(seq)
    row = (pos // GRID_W).astype(F32)
    col = (pos % GRID_W).astype(F32)
    inv_freq = 1.0 / (ROPE_THETA ** (jnp.arange(0, half, 2, dtype=F32) / half))
    lane = _head_lane_order()
    freq = inv_freq[(lane % half) % (half // 2)]
    ids = jnp.where(lane[None, :] < half, row[:, None], col[:, None])
    ang = ids * freq[None, :]
    sign = jnp.where((lane % half) < half // 2, -1.0, 1.0)
    return jnp.cos(ang), jnp.sin(ang) * sign[None, :]


def _dft_tables(seq, gdim, fr):
    def cs(k, n, period):
        r = (k[..., None] * n) % period
        ang = r.astype(F32) * (2.0 * math.pi / period)
        return jnp.cos(ang), jnp.sin(ang)
    c = jnp.arange(gdim, dtype=jnp.int32)
    cc, sc = (t * gdim ** -0.5 for t in cs(c, c, gdim))
    rows = fr + FOUR_EXTRA_ROWS
    k = jnp.arange(seq // 2 // fr, dtype=jnp.int32)[:, None] * fr + jnp.arange(rows, dtype=jnp.int32)
    ca, sa = cs(k, GRID_W * jnp.arange(seq // GRID_W, dtype=jnp.int32), seq)
    cb, sb = cs(k, jnp.arange(GRID_W, dtype=jnp.int32), seq)
    scale = seq ** -0.5
    ch = ((ca[..., :, None] * cb[..., None, :] - sa[..., :, None] * sb[..., None, :]) * scale)
    sh = ((sa[..., :, None] * cb[..., None, :] + ca[..., :, None] * sb[..., None, :]) * scale)
    ch, sh = ch.reshape(*k.shape, seq), sh.reshape(*k.shape, seq)
    flip = (jnp.arange(rows)[None, :] == fr - jnp.arange(fr)[:, None]).astype(BF16)
    return (jnp.concatenate([cc, sc], axis=1).astype(BF16), ch.astype(BF16), sh.astype(BF16), flip)


def _tiles(seq, d_ff):
    def pick(n, pref):
        return pref if n % pref == 0 else n
    tf = next(c for c in (512, 256, 128) if d_ff % c == 0)
    return dict(
        qk_tm=pick(seq, 512), mixer_tm=pick(seq, 1024), uv_tn=768, gate_tn=1024,
        attn_tq=pick(seq, 512), attn_kc=pick(seq, 256),
        four_tm=pick(seq, 512),
        merge_tm=pick(seq, 256),
        xattn_tm=pick(seq, 512),
        ffn_tm=pick(seq, 512), ffn_tf=tf,
    )


def kernel(x, mem, mix_pre_g, w_in, q_norm_g, k_norm_g, w_attn_o, w_four_o, w_gate, b_gate, w_mix_o,
           mix_post_g, xa_pre_g, mem_norm_g, w_xq, w_xkv, w_xo, xa_post_g, ffn_pre_g, w_up, conv_w,
           conv_b, w_down, ffn_post_g):
    batch, seq, d = x.shape
    mem_tokens = mem.shape[1]
    depth = w_in.shape[0]
    q_width = w_attn_o.shape[1]
    f_width = w_four_o.shape[1]
    kv_width = (w_in.shape[2] - q_width - f_width) // 2
    n_heads = q_width // HEAD_DIM
    assert n_heads == N_KV * GROUP and kv_width == N_KV * HEAD_DIM
    assert seq % GRID_W == 0 and w_xq.shape[2] == X_HEADS * X_HEAD_DIM
    d_ff = w_down.shape[1]
    ts = _tiles(seq, d_ff)
    qk_width = q_width + kv_width
    u_col = q_width + 2 * kv_width
    tm = ts["mixer_tm"]
    assert (kv_width + f_width) % ts["uv_tn"] == 0 and w_gate.shape[2] % ts["gate_tn"] == 0

    cos_t, sin_t = _rope_tables(seq)
    assert seq % (2 * ts["merge_tm"]) == 0
    dft_c, dft_ch, dft_sh, dft_flip = _dft_tables(seq, f_width // N_FOURIER_GROUPS, ts["merge_tm"])
    order = _head_lane_order()
    qk_cols = (jnp.arange(qk_width // HEAD_DIM)[:, None] * HEAD_DIM + order[None, :]).reshape(-1)
    partner = jnp.roll(jnp.arange(HEAD_DIM), HEAD_DIM // 2)
    bias_spec = pl.BlockSpec((1, ts["gate_tn"]), lambda i, j: (0, j))

    row = lambda v: v.reshape(1, -1)
    xf = x.reshape(batch * seq, d)
    memf = mem.reshape(batch * mem_tokens, d)

    for l in range(depth):
        w_qk_t = w_in[l][:, qk_cols].T.astype(BF16)
        w_uv = jnp.concatenate([w_in[l][:, u_col:], w_in[l][:, qk_width:u_col]], axis=1).astype(BF16)
        q_gain = q_norm_g[l][order] * (HEAD_DIM ** -0.5 * math.log2(math.e))
        k_gain = k_norm_g[l][order]
        cg = jnp.concatenate([g[:, None] * cos_t.T for g in (q_gain, k_gain)], axis=0)
        sg = jnp.concatenate([g[partner][:, None] * sin_t.T for g in (q_gain, k_gain)], axis=0)
        g_pre = row(mix_pre_g[l])
        qk = _proj_qk_t(xf, g_pre, w_qk_t, cg, sg, seq=seq, n_q_heads=n_heads, tm=ts["qk_tm"])
        uv = _proj(xf, g_pre, w_uv, _epilogue_plain, (), (), tm=tm, tn=ts["uv_tn"], name="proj_uv")
        gates = _proj(xf, g_pre, w_gate[l].astype(BF16), _epilogue_gate, (row(b_gate[l]),),
                      (bias_spec,), tm=tm, tn=ts["gate_tn"], name="proj_gate")
        att = _attention(qk, uv, batch=batch, seq=seq, q_width=q_width, v_col=f_width,
                         tq=ts["attn_tq"], kc=ts["attn_kc"])
        ucs = _four_chan(uv, dft_c, batch=batch, seq=seq, f_width=f_width, tm=ts["four_tm"])
        four = _four_seq(dft_ch, dft_sh, dft_flip, ucs, batch=batch, seq=seq, f_width=f_width)
        xf = _merge(att, four.reshape(batch * seq, f_width), gates, xf, w_attn_o[l].astype(BF16),
                    w_four_o[l].astype(BF16), w_mix_o[l].astype(BF16), row(mix_post_g[l]),
                    seq=seq, tm=ts["merge_tm"])

        kv = _norm_matmul(memf, row(mem_norm_g[l]), w_xkv[l].astype(BF16), tm=mem_tokens)
        xf = _xattn(xf, row(xa_pre_g[l]), w_xq[l].astype(BF16), kv, w_xo[l].astype(BF16),
                    row(xa_post_g[l]), batch=batch, seq=seq, mem_tokens=mem_tokens, tm=ts["xattn_tm"])

        pair = functools.partial(_ffn_pair_columns, d_ff=d_ff, tf=ts["ffn_tf"])
        xf = _ffn(xf, row(ffn_pre_g[l]), pair(w_up[l]).astype(BF16), pair(conv_w[l]),
                  pair(row(conv_b[l])), w_down[l].astype(BF16), row(ffn_post_g[l]),
                  seq=seq, tm=ts["ffn_tm"], tf=ts["ffn_tf"])

    return xf.reshape(batch, seq, d)
```

```python
import functools
import math

import jax
import jax.numpy as jnp
from jax import lax
from jax.experimental import pallas as pl
from jax.experimental.pallas import tpu as pltpu

HEAD_DIM = 128
N_KV = 4
GROUP = 4
GRID_W = 64
ROPE_THETA = 10000.0
N_FOURIER_GROUPS = 4
X_HEADS = 4
X_HEAD_DIM = 128
EPS = 1e-6

V7X_VMEM_LIMIT_BYTES = 56 * 1024 * 1024
BF16_SUBLANES = 16
ATTN_MIN_DENOMINATOR = 2.0 ** -60
FOUR_EXTRA_ROWS = BF16_SUBLANES
FFN_NORM_ROWS = 128

F32 = jnp.float32
BF16 = jnp.bfloat16


def _params(*sem):
    return pltpu.CompilerParams(dimension_semantics=sem, vmem_limit_bytes=V7X_VMEM_LIMIT_BYTES)


def _rms(xf, g):
    ms = jnp.mean(xf * xf, axis=-1, keepdims=True)
    return xf * lax.rsqrt(ms + EPS) * g


def _dot(a, b):
    return jnp.dot(a, b, preferred_element_type=F32)


def _dot_nt(a, b):
    return lax.dot_general(a, b, (((1,), (1,)), ((), ())), preferred_element_type=F32)


def _epilogue_plain(z, o_ref):
    o_ref[...] = z.astype(BF16)


def _epilogue_gate(z, b_ref, o_ref):
    o_ref[...] = (0.5 * jnp.tanh(0.5 * (z + b_ref[...])) + 0.5).astype(BF16)


def _proj_qk_t_kernel(x_ref, g_ref, wt_ref, cg_ref, sg_ref, o_ref, *, n_q_heads):
    hn = _rms(x_ref[...], g_ref[...]).astype(BF16)
    zt = _dot_nt(wt_ref[...], hn)
    half = HEAD_DIM // 2
    for h in range(zt.shape[0] // HEAD_DIM):
        tab = slice(0, HEAD_DIM) if h < n_q_heads else slice(HEAD_DIM, 2 * HEAD_DIM)
        y = zt[h * HEAD_DIM:(h + 1) * HEAD_DIM, :]
        y = y * lax.rsqrt(jnp.mean(y * y, axis=0, keepdims=True) + EPS)
        rot = jnp.concatenate([y[half:], y[:half]], axis=0)
        o_ref[h * HEAD_DIM:(h + 1) * HEAD_DIM, :] = (y * cg_ref[tab, :] + rot * sg_ref[tab, :]).astype(BF16)


def _proj_qk_t(x, g_pre, w_t, cg, sg, *, layer, seq, n_q_heads, tm):
    t, d = x.shape
    n = w_t.shape[1]
    seq_tiles = seq // tm
    table_spec = pl.BlockSpec((2 * HEAD_DIM, tm), lambda i: (0, i % seq_tiles))
    return pl.pallas_call(
        functools.partial(_proj_qk_t_kernel, n_q_heads=n_q_heads),
        grid=(t // tm,),
        in_specs=[
            pl.BlockSpec((tm, d), lambda i: (i, 0)),
            _resident(g_pre.shape),
            _resident_layer(w_t, layer),
            table_spec,
            table_spec,
        ],
        out_specs=pl.BlockSpec((n, tm), lambda i: (0, i)),
        out_shape=jax.ShapeDtypeStruct((n, t), BF16),
        compiler_params=_params("parallel"),
        name="proj_qk",
    )(x, g_pre, w_t, cg, sg)


def _proj_kernel(x_ref, g_ref, w_ref, *rest, epilogue):
    hn_ref = rest[-1]

    @pl.when(pl.program_id(1) == 0)
    def _():
        hn_ref[...] = _rms(x_ref[...], g_ref[...]).astype(BF16)

    epilogue(_dot(hn_ref[...], w_ref[...]), *rest[:-1])


def _proj(x, g_pre, w, epilogue, extra, extra_specs, *, layer, tm, tn, name):
    t, d = x.shape
    n = w.shape[2]
    return pl.pallas_call(
        functools.partial(_proj_kernel, epilogue=epilogue),
        grid=(t // tm, n // tn),
        in_specs=[
            pl.BlockSpec((tm, d), lambda i, j: (i, 0)),
            pl.BlockSpec((1, d), lambda i, j: (0, 0)),
            _layer_spec((d, tn), lambda i, j: (0, j), layer),
            *extra_specs,
        ],
        out_specs=pl.BlockSpec((tm, tn), lambda i, j: (i, j)),
        out_shape=jax.ShapeDtypeStruct((t, n), BF16),
        scratch_shapes=[pltpu.VMEM((tm, d), BF16)],
        compiler_params=_params("parallel", "arbitrary"),
        name=name,
    )(x, g_pre, w, *extra)


def _attn_exact(q, ka_ref, va_ref, kc):
    rows = q.shape[0]
    m = jnp.full((rows, 1), -jnp.inf, F32)
    l = jnp.zeros((rows, 1), F32)
    acc = jnp.zeros((rows, HEAD_DIM), F32)
    for c in range(ka_ref.shape[0] // kc):
        ks = slice(c * kc, (c + 1) * kc)
        s = _dot_nt(q, ka_ref[ks, :HEAD_DIM])
        m_new = jnp.maximum(m, jnp.max(s, axis=-1, keepdims=True))
        alpha = jnp.exp2(m - m_new)
        p = jnp.exp2(s - m_new)
        l = alpha * l + jnp.sum(p, axis=-1, keepdims=True)
        acc = alpha * acc + _dot(p.astype(BF16), va_ref[ks, :HEAD_DIM])
        m = m_new
    return acc * (1.0 / l)


def _attn_kernel(qt_ref, kt_ref, v_ref, o_ref, ka_ref, va_ref, kmax_ref, *, kc):
    tq = qt_ref.shape[1]
    s_len = v_ref.shape[0]
    n_chunks = s_len // kc
    lane = lax.broadcasted_iota(jnp.int32, (1, HEAD_DIM), 1)

    @pl.when(pl.program_id(2) == 0)
    def _():
        ones_col = jnp.broadcast_to(jnp.where(lane == 0, 1.0, 0.0), (kc, HEAD_DIM)).astype(BF16)
        k2max = jnp.zeros((1, 1), F32)
        for c in range(n_chunks):
            ks = slice(c * kc, (c + 1) * kc)
            kt = kt_ref[:, ks]
            kf = kt.astype(F32)
            k2max = jnp.maximum(k2max, jnp.max(jnp.sum(kf * kf, axis=0, keepdims=True),
                                               axis=1, keepdims=True))
            ka_ref[ks, :HEAD_DIM] = kt.T
            ka_ref[ks, HEAD_DIM:] = ones_col
            va_ref[ks, :HEAD_DIM] = v_ref[ks, :]
            va_ref[ks, HEAD_DIM:] = ones_col
        kmax_ref[...] = jnp.broadcast_to(jnp.sqrt(k2max), kmax_ref.shape)

    q = jnp.concatenate(
        [qt_ref[g * HEAD_DIM:(g + 1) * HEAD_DIM, :].T for g in range(GROUP)], axis=0)
    qf = q.astype(F32)
    shift = jnp.sqrt(jnp.sum(qf * qf, axis=-1, keepdims=True)) * kmax_ref[0:1, 0:1]
    shift_col = (-shift) * jnp.where(lane == 0, 1.0, 0.0)
    q_aug = jnp.concatenate([q, shift_col.astype(BF16)], axis=1)
    acc = jnp.zeros((GROUP * tq, 2 * HEAD_DIM), F32)
    for c in range(n_chunks):
        ks = slice(c * kc, (c + 1) * kc)
        p = jnp.exp2(_dot_nt(q_aug, ka_ref[ks, :])).astype(BF16)
        acc = acc + _dot(p, va_ref[ks, :])
    l = acc[:, HEAD_DIM:HEAD_DIM + 1]
    accurate = jnp.min(l) >= ATTN_MIN_DENOMINATOR

    def write(o):
        for g in range(GROUP):
            o_ref[:, g * HEAD_DIM:(g + 1) * HEAD_DIM] = o[g * tq:(g + 1) * tq].astype(BF16)

    @pl.when(accurate)
    def _():
        write(acc[:, :HEAD_DIM] * (1.0 / l))

    @pl.when(jnp.logical_not(accurate))
    def _():
        write(_attn_exact(q, ka_ref, va_ref, kc))


def _attention(qk_t, vu, *, batch, seq, q_width, v_col, tq, kc):
    t = vu.shape[0]
    gw = GROUP * HEAD_DIM
    k_blk0 = q_width // HEAD_DIM
    v_blk0 = v_col // HEAD_DIM
    nq = seq // tq
    return pl.pallas_call(
        functools.partial(_attn_kernel, kc=kc),
        grid=(batch, N_KV, nq),
        in_specs=[
            pl.BlockSpec((gw, tq), lambda b, h, i: (h, b * nq + i)),
            pl.BlockSpec((HEAD_DIM, seq), lambda b, h, i: (k_blk0 + h, b)),
            pl.BlockSpec((seq, HEAD_DIM), lambda b, h, i: (b, v_blk0 + h)),
        ],
        out_specs=pl.BlockSpec((tq, gw), lambda b, h, i: (b * nq + i, h)),
        out_shape=jax.ShapeDtypeStruct((t, q_width), BF16),
        scratch_shapes=[
            pltpu.VMEM((seq, 2 * HEAD_DIM), BF16),
            pltpu.VMEM((seq, 2 * HEAD_DIM), BF16),
            pltpu.VMEM((8, HEAD_DIM), F32),
        ],
        compiler_params=_params("parallel", "parallel", "arbitrary"),
        name="attention",
    )(qk_t, qk_t, vu)


def _four_chan_kernel(u_ref, dft_ref, o_ref, *, gdim):
    for g in range(u_ref.shape[1] // gdim):
        sl = slice(g * gdim, (g + 1) * gdim)
        r = _dot(u_ref[:, sl], dft_ref[...])
        o_ref[0, 0, :, sl] = r[:, :gdim].astype(BF16)
        o_ref[0, 1, :, sl] = r[:, gdim:].astype(BF16)


def _four_chan(uv, dft_c, *, batch, seq, f_width, tm):
    gdim = f_width // N_FOURIER_GROUPS
    ns = seq // tm
    return pl.pallas_call(
        functools.partial(_four_chan_kernel, gdim=gdim),
        grid=(batch, ns),
        in_specs=[
            pl.BlockSpec((tm, f_width), lambda b, i: (b * ns + i, 0)),
            pl.BlockSpec((gdim, 2 * gdim), lambda b, i: (0, 0)),
        ],
        out_specs=pl.BlockSpec((1, 2, tm, f_width), lambda b, i: (b, 0, i, 0)),
        out_shape=jax.ShapeDtypeStruct((batch, 2, seq, f_width), BF16),
        compiler_params=_params("parallel", "parallel"),
        name="fourier_chan",
    )(uv, dft_c)


def _four_seq_kernel(ch_ref, sh_ref, uc_ref, us_ref, flip_ref, o_ref):
    fr = o_ref.shape[3]
    p = _dot(ch_ref[0], uc_ref[0, 0])
    q = _dot(sh_ref[0], us_ref[0, 0])
    o_ref[0, 0, 0] = (p - q)[:fr].astype(BF16)
    o_ref[0, 1, 0] = _dot(flip_ref[...], (p + q).astype(BF16)).astype(BF16)


def _four_seq(ch, sh, flip, ucs, *, batch, seq, f_width):
    nt, rows, _ = ch.shape
    fr = flip.shape[0]
    half_spec = pl.BlockSpec((1, rows, seq), lambda b, t: (t, 0, 0))
    return pl.pallas_call(
        _four_seq_kernel,
        grid=(batch, nt),
        in_specs=[
            half_spec,
            half_spec,
            pl.BlockSpec((1, 1, seq, f_width), lambda b, t: (b, 0, 0, 0), pipeline_mode=pl.Buffered(1)),
            pl.BlockSpec((1, 1, seq, f_width), lambda b, t: (b, 1, 0, 0), pipeline_mode=pl.Buffered(1)),
            _resident(flip.shape),
        ],
        out_specs=pl.BlockSpec((1, 2, 1, fr, f_width), lambda b, t: (b, 0, t, 0, 0)),
        out_shape=jax.ShapeDtypeStruct((batch, 2, nt, fr, f_width), BF16),
        compiler_params=_params("parallel", "arbitrary"),
        name="fourier_seq",
    )(ch, sh, ucs, ucs, flip)


def _merge_kernel(att_ref, y_ref, g1_ref, g2_ref, x_ref, wa_ref, wf_ref, wm_ref, gp_ref, o_ref):
    a = _dot(att_ref[...], wa_ref[...])
    f = _dot(y_ref[...], wf_ref[...])
    merged = g1_ref[...].astype(F32) * a + g2_ref[...].astype(F32) * f
    y = _dot(merged.astype(BF16), wm_ref[...])
    o_ref[...] = x_ref[...] + _rms(y, gp_ref[...])


def _resident(shape):
    return pl.BlockSpec(shape, lambda *_: (0,) * len(shape), pipeline_mode=pl.Buffered(1))


def _layer_spec(block, index_map, layer, **kwargs):
    return pl.BlockSpec((None, *block), lambda *idx: (layer, *index_map(*idx)), **kwargs)


def _resident_layer(w, layer):
    return _layer_spec(w.shape[1:], lambda *_: (0,) * (w.ndim - 1), layer,
                       pipeline_mode=pl.Buffered(1))


def _merge(att, four, gates, x, wa, wf, wm, g_post, *, layer, seq, tm):
    t, d = x.shape
    tiles = seq // tm

    def four_block(i):
        it = i % tiles
        return (i - it + jnp.where(it < tiles // 2, it, tiles + tiles // 2 - 1 - it), 0)

    return pl.pallas_call(
        _merge_kernel,
        grid=(t // tm,),
        in_specs=[
            pl.BlockSpec((tm, att.shape[1]), lambda i: (i, 0)),
            pl.BlockSpec((tm, four.shape[1]), four_block),
            pl.BlockSpec((tm, d), lambda i: (i, 0)),
            pl.BlockSpec((tm, d), lambda i: (i, 1)),
            pl.BlockSpec((tm, d), lambda i: (i, 0)),
            _resident_layer(wa, layer),
            _resident_layer(wf, layer),
            _resident_layer(wm, layer),
            _resident(g_post.shape),
        ],
        out_specs=pl.BlockSpec((tm, d), lambda i: (i, 0)),
        out_shape=jax.ShapeDtypeStruct((t, d), F32),
        compiler_params=_params("parallel"),
        name="merge",
    )(att, four, gates, gates, x, wa, wf, wm, g_post)


def _norm_matmul_kernel(x_ref, g_ref, w_ref, o_ref):
    hn = _rms(x_ref[...], g_ref[...]).astype(BF16)
    o_ref[...] = _dot(hn, w_ref[...]).astype(BF16)


def _norm_matmul(x, g, w, *, layer, tm):
    t, d = x.shape
    n = w.shape[2]
    return pl.pallas_call(
        _norm_matmul_kernel,
        grid=(t // tm,),
        in_specs=[
            pl.BlockSpec((tm, d), lambda i: (i, 0)),
            _resident(g.shape),
            _resident_layer(w, layer),
        ],
        out_specs=pl.BlockSpec((tm, n), lambda i: (i, 0)),
        out_shape=jax.ShapeDtypeStruct((t, n), BF16),
        compiler_params=_params("parallel"),
        name="mem_kv",
    )(x, g, w)


def _xattn_kernel(x_ref, gpre_ref, wq_ref, k_ref, v_ref, wo_ref, gpost_ref, o_ref):
    x = x_ref[...]
    hn = _rms(x, gpre_ref[...]).astype(BF16)
    q = (_dot(hn, wq_ref[...]) * (X_HEAD_DIM ** -0.5)).astype(BF16)
    outs = []
    for h in range(X_HEADS):
        sl = slice(h * X_HEAD_DIM, (h + 1) * X_HEAD_DIM)
        s = _dot_nt(q[:, sl], k_ref[:, sl])
        m = jnp.max(s, axis=-1, keepdims=True)
        p = jnp.exp(s - m)
        l = jnp.sum(p, axis=-1, keepdims=True)
        outs.append(_dot(p.astype(BF16), v_ref[:, sl]) * (1.0 / l))
    o = jnp.concatenate(outs, axis=1).astype(BF16)
    y = _dot(o, wo_ref[...])
    o_ref[...] = x + _rms(y, gpost_ref[...])


def _xattn(x, g_pre, wq, kv, wo, g_post, *, layer, batch, seq, mem_tokens, tm):
    t, d = x.shape
    xw = wq.shape[2]
    ns = seq // tm
    return pl.pallas_call(
        _xattn_kernel,
        grid=(batch, ns),
        in_specs=[
            pl.BlockSpec((tm, d), lambda b, i: (b * ns + i, 0)),
            _resident(g_pre.shape),
            _resident_layer(wq, layer),
            pl.BlockSpec((mem_tokens, xw), lambda b, i: (b, 0)),
            pl.BlockSpec((mem_tokens, xw), lambda b, i: (b, 1)),
            _resident_layer(wo, layer),
            _resident(g_post.shape),
        ],
        out_specs=pl.BlockSpec((tm, d), lambda b, i: (b * ns + i, 0)),
        out_shape=jax.ShapeDtypeStruct((t, d), F32),
        compiler_params=_params("parallel", "parallel"),
        name="xattn",
    )(x, g_pre, wq, kv, kv, wo, g_post)


def _gelu_tanh(x):
    return 0.5 * x * (1.0 + jnp.tanh(math.sqrt(2.0 / math.pi) * (x + 0.044715 * (x * x * x))))


def _ffn_kernel(xp_ref, x_ref, xn_ref, gpre_ref, wg_ref, wv_ref, cwg_ref, cwv_ref, cbg_ref, cbv_ref,
                wd_ref, gpost_ref, o_ref, hn_ref, ug_ref, uv_ref, *, tiles_per_seq):
    i = pl.program_id(0)
    j = pl.program_id(1)
    tm = x_ref.shape[0]
    halo = xp_ref.shape[0]

    @pl.when(j == 0)
    def _():
        g = gpre_ref[...]
        pos = i % tiles_per_seq
        hp = jnp.where(pos == 0, 0.0, _rms(xp_ref[...], g))
        hx = jnp.where(pos == tiles_per_seq - 1, 0.0, _rms(xn_ref[...], g))
        hn_ref[0:halo, :] = hp.astype(BF16)
        hn_ref[halo + tm:, :] = hx.astype(BF16)
        for r in range(0, tm, FFN_NORM_ROWS):
            rows = slice(r, r + FFN_NORM_ROWS)
            hn_ref[halo + r:halo + r + FFN_NORM_ROWS, :] = _rms(x_ref[rows, :], g).astype(BF16)
        o_ref[...] = jnp.zeros_like(o_ref)

    hn = hn_ref[...]
    ug_ref[...] = _dot(hn, wg_ref[...])
    uv_ref[...] = _dot(hn, wv_ref[...])

    def conv(u_ref, cw_ref, cb_ref):
        return (u_ref[halo - 1:halo - 1 + tm, :] * cw_ref[0:1, :]
                + u_ref[halo:halo + tm, :] * cw_ref[1:2, :]
                + u_ref[halo + 1:halo + 1 + tm, :] * cw_ref[2:3, :]
                + cb_ref[...])

    act = _gelu_tanh(conv(ug_ref, cwg_ref, cbg_ref)) * conv(uv_ref, cwv_ref, cbv_ref)
    o_ref[...] += _dot(act.astype(BF16), wd_ref[...])

    @pl.when(j == pl.num_programs(1) - 1)
    def _():
        for r in range(0, tm, FFN_NORM_ROWS):
            rows = slice(r, r + FFN_NORM_ROWS)
            o_ref[rows, :] = x_ref[rows, :] + _rms(o_ref[rows, :], gpost_ref[...])


def _ffn(x, g_pre, w_up, conv_w, conv_b, w_down, g_post, *, layer, seq, tm, tf):
    t, d = x.shape
    d_ff = w_down.shape[1]
    nf = d_ff // tf
    halo = BF16_SUBLANES
    hb = tm // halo
    n_hblk = t // halo
    kern = functools.partial(_ffn_kernel, tiles_per_seq=seq // tm)
    return pl.pallas_call(
        kern,
        grid=(t // tm, nf),
        in_specs=[
            pl.BlockSpec((halo, d), lambda i, j: (jnp.maximum(i * hb - 1, 0), 0)),
            pl.BlockSpec((tm, d), lambda i, j: (i, 0)),
            pl.BlockSpec((halo, d), lambda i, j: (jnp.minimum((i + 1) * hb, n_hblk - 1), 0)),
            pl.BlockSpec((1, d), lambda i, j: (0, 0)),
            _layer_spec((d, tf), lambda i, j: (0, j), layer),
            _layer_spec((d, tf), lambda i, j: (0, nf + j), layer),
            pl.BlockSpec((3, tf), lambda i, j: (0, j)),
            pl.BlockSpec((3, tf), lambda i, j: (0, nf + j)),
            pl.BlockSpec((1, tf), lambda i, j: (0, j)),
            pl.BlockSpec((1, tf), lambda i, j: (0, nf + j)),
            _layer_spec((tf, d), lambda i, j: (j, 0), layer),
            pl.BlockSpec((1, d), lambda i, j: (0, 0)),
        ],
        out_specs=pl.BlockSpec((tm, d), lambda i, j: (i, 0)),
        out_shape=jax.ShapeDtypeStruct((t, d), F32),
        scratch_shapes=[
            pltpu.VMEM((tm + 2 * halo, d), BF16),
            pltpu.VMEM((tm + 2 * halo, tf), F32),
            pltpu.VMEM((tm + 2 * halo, tf), F32),
        ],
        compiler_params=_params("parallel", "arbitrary"),
        name="conv_ffn",
    )(x, x, x, g_pre, w_up, w_up, conv_w, conv_w, conv_b, conv_b, w_down, g_post)


def _head_lane_order():
    quarter = HEAD_DIM // 4
    blocks = (0, 2, 1, 3)
    return jnp.concatenate([jnp.arange(b * quarter, (b + 1) * quarter) for b in blocks])


def _rope_tables(seq):
    half = HEAD_DIM // 2
    pos = jnp.arange(seq)
    row = (pos // GRID_W).astype(F32)
    col = (pos % GRID_W).astype(F32)
    inv_freq = 1.0 / (ROPE_THETA ** (jnp.arange(0, half, 2, dtype=F32) / half))
    lane = _head_lane_order()
    freq = inv_freq[(lane % half) % (half // 2)]
    ids = jnp.where(lane[None, :] < half, row[:, None], col[:, None])
    ang = ids * freq[None, :]
    sign = jnp.where((lane % half) < half // 2, -1.0, 1.0)
    return jnp.cos(ang), jnp.sin(ang) * sign[None, :]


def _dft_tables(seq, gdim, fr):
    def cs(k, n, period):
        r = (k[..., None] * n) % period
        ang = r.astype(F32) * (2.0 * math.pi / period)
        return jnp.cos(ang), jnp.sin(ang)
    c = jnp.arange(gdim, dtype=jnp.int32)
    cc, sc = (t * gdim ** -0.5 for t in cs(c, c, gdim))
    rows = fr + FOUR_EXTRA_ROWS
    k = jnp.arange(seq // 2 // fr, dtype=jnp.int32)[:, None] * fr + jnp.arange(rows, dtype=jnp.int32)
    ca, sa = cs(k, GRID_W * jnp.arange(seq // GRID_W, dtype=jnp.int32), seq)
    cb, sb = cs(k, jnp.arange(GRID_W, dtype=jnp.int32), seq)
    scale = seq ** -0.5
    ch = ((ca[..., :, None] * cb[..., None, :] - sa[..., :, None] * sb[..., None, :]) * scale)
    sh = ((sa[..., :, None] * cb[..., None, :] + ca[..., :, None] * sb[..., None, :]) * scale)
    ch, sh = ch.reshape(*k.shape, seq), sh.reshape(*k.shape, seq)
    flip = (jnp.arange(rows)[None, :] == fr - jnp.arange(fr)[:, None]).astype(BF16)
    return (jnp.concatenate([cc, sc], axis=1).astype(BF16), ch.astype(BF16), sh.astype(BF16), flip)


def _tiles(seq, d_ff):
    def pick(n, pref):
        return pref if n % pref == 0 else n
    tf = next(c for c in (512, 256, 128) if d_ff % c == 0)
    return dict(
        qk_tm=pick(seq, 512), mixer_tm=pick(seq, 1024), uv_tn=768, gate_tn=1024,
        attn_tq=pick(seq, 512), attn_kc=pick(seq, 256),
        four_tm=pick(seq, 512),
        merge_tm=pick(seq, 256),
        xattn_tm=pick(seq, 512),
        ffn_tm=pick(seq, 512), ffn_tf=tf,
    )


def kernel(x, mem, mix_pre_g, w_in, q_norm_g, k_norm_g, w_attn_o, w_four_o, w_gate, b_gate, w_mix_o,
           mix_post_g, xa_pre_g, mem_norm_g, w_xq, w_xkv, w_xo, xa_post_g, ffn_pre_g, w_up, conv_w,
           conv_b, w_down, ffn_post_g):
    batch, seq, d = x.shape
    mem_tokens = mem.shape[1]
    depth = w_in.shape[0]
    q_width = w_attn_o.shape[1]
    f_width = w_four_o.shape[1]
    kv_width = (w_in.shape[2] - q_width - f_width) // 2
    n_heads = q_width // HEAD_DIM
    assert n_heads == N_KV * GROUP and kv_width == N_KV * HEAD_DIM
    assert seq % GRID_W == 0 and w_xq.shape[2] == X_HEADS * X_HEAD_DIM
    d_ff = w_down.shape[1]
    ts = _tiles(seq, d_ff)
    qk_width = q_width + kv_width
    u_col = q_width + 2 * kv_width
    tm = ts["mixer_tm"]
    assert (kv_width + f_width) % ts["uv_tn"] == 0 and w_gate.shape[2] % ts["gate_tn"] == 0

    cos_t, sin_t = _rope_tables(seq)
    assert seq % (2 * ts["merge_tm"]) == 0
    dft_c, dft_ch, dft_sh, dft_flip = _dft_tables(seq, f_width // N_FOURIER_GROUPS, ts["merge_tm"])
    order = _head_lane_order()
    qk_cols = (jnp.arange(qk_width // HEAD_DIM)[:, None] * HEAD_DIM + order[None, :]).reshape(-1)
    partner = jnp.roll(jnp.arange(HEAD_DIM), HEAD_DIM // 2)
    bias_spec = pl.BlockSpec((1, ts["gate_tn"]), lambda i, j: (0, j))

    row = lambda v: v.reshape(1, -1)
    xf = x.reshape(batch * seq, d)
    memf = mem.reshape(batch * mem_tokens, d)

    w_qk_t = jnp.swapaxes(w_in[:, :, qk_cols], 1, 2).astype(BF16)
    w_uv = jnp.concatenate([w_in[:, :, u_col:], w_in[:, :, qk_width:u_col]], axis=2).astype(BF16)
    w_gate_b, w_attn_o_b, w_four_o_b, w_mix_o_b, w_xq_b, w_xkv_b, w_xo_b, w_up_b, w_down_b = (
        w.astype(BF16) for w in (w_gate, w_attn_o, w_four_o, w_mix_o, w_xq, w_xkv, w_xo, w_up, w_down))

    for l in range(depth):
        q_gain = q_norm_g[l][order] * (HEAD_DIM ** -0.5 * math.log2(math.e))
        k_gain = k_norm_g[l][order]
        cg = jnp.concatenate([g[:, None] * cos_t.T for g in (q_gain, k_gain)], axis=0)
        sg = jnp.concatenate([g[partner][:, None] * sin_t.T for g in (q_gain, k_gain)], axis=0)
        g_pre = row(mix_pre_g[l])
        qk = _proj_qk_t(xf, g_pre, w_qk_t, cg, sg, layer=l, seq=seq, n_q_heads=n_heads,
                        tm=ts["qk_tm"])
        uv = _proj(xf, g_pre, w_uv, _epilogue_plain, (), (), layer=l, tm=tm, tn=ts["uv_tn"],
                   name="proj_uv")
        gates = _proj(xf, g_pre, w_gate_b, _epilogue_gate, (row(b_gate[l]),), (bias_spec,),
                      layer=l, tm=tm, tn=ts["gate_tn"], name="proj_gate")
        att = _attention(qk, uv, batch=batch, seq=seq, q_width=q_width, v_col=f_width,
                         tq=ts["attn_tq"], kc=ts["attn_kc"])
        ucs = _four_chan(uv, dft_c, batch=batch, seq=seq, f_width=f_width, tm=ts["four_tm"])
        four = _four_seq(dft_ch, dft_sh, dft_flip, ucs, batch=batch, seq=seq, f_width=f_width)
        xf = _merge(att, four.reshape(batch * seq, f_width), gates, xf, w_attn_o_b, w_four_o_b,
                    w_mix_o_b, row(mix_post_g[l]), layer=l, seq=seq, tm=ts["merge_tm"])

        kv = _norm_matmul(memf, row(mem_norm_g[l]), w_xkv_b, layer=l, tm=mem_tokens)
        xf = _xattn(xf, row(xa_pre_g[l]), w_xq_b, kv, w_xo_b, row(xa_post_g[l]), layer=l,
                    batch=batch, seq=seq, mem_tokens=mem_tokens, tm=ts["xattn_tm"])

        xf = _ffn(xf, row(ffn_pre_g[l]), w_up_b, conv_w[l], row(conv_b[l]), w_down_b,
                  row(ffn_post_g[l]), layer=l, seq=seq, tm=ts["ffn_tm"], tf=ts["ffn_tf"])

    return xf.reshape(batch, seq, d)
```

```python
import functools
import math

import jax
import jax.numpy as jnp
from jax import lax
from jax.experimental import pallas as pl
from jax.experimental.pallas import tpu as pltpu

HEAD_DIM = 128
N_KV = 4
GROUP = 4
GRID_W = 64
ROPE_THETA = 10000.0
N_FOURIER_GROUPS = 4
X_HEADS = 4
X_HEAD_DIM = 128
EPS = 1e-6

V7X_VMEM_LIMIT_BYTES = 56 * 1024 * 1024
BF16_SUBLANES = 16
ATTN_MIN_DENOMINATOR = 2.0 ** -60
FOUR_EXTRA_ROWS = BF16_SUBLANES
FFN_NORM_ROWS = 128

F32 = jnp.float32
BF16 = jnp.bfloat16


def _params(*sem):
    return pltpu.CompilerParams(dimension_semantics=sem, vmem_limit_bytes=V7X_VMEM_LIMIT_BYTES)


def _rms(xf, g):
    ms = jnp.mean(xf * xf, axis=-1, keepdims=True)
    return xf * lax.rsqrt(ms + EPS) * g


def _dot(a, b):
    return jnp.dot(a, b, preferred_element_type=F32)


def _dot_nt(a, b):
    return lax.dot_general(a, b, (((1,), (1,)), ((), ())), preferred_element_type=F32)


def _epilogue_plain(z, o_ref):
    o_ref[...] = z.astype(BF16)


def _epilogue_gate(z, b_ref, o_ref):
    o_ref[...] = (0.5 * jnp.tanh(0.5 * (z + b_ref[...])) + 0.5).astype(BF16)


def _proj_qk_t_kernel(x_ref, g_ref, wt_ref, cg_ref, sg_ref, o_ref, *, n_q_heads):
    hn = _rms(x_ref[...], g_ref[...]).astype(BF16)
    zt = _dot_nt(wt_ref[...], hn)
    half = HEAD_DIM // 2
    for h in range(zt.shape[0] // HEAD_DIM):
        tab = slice(0, HEAD_DIM) if h < n_q_heads else slice(HEAD_DIM, 2 * HEAD_DIM)
        y = zt[h * HEAD_DIM:(h + 1) * HEAD_DIM, :]
        y = y * lax.rsqrt(jnp.mean(y * y, axis=0, keepdims=True) + EPS)
        rot = jnp.concatenate([y[half:], y[:half]], axis=0)
        o_ref[h * HEAD_DIM:(h + 1) * HEAD_DIM, :] = (y * cg_ref[tab, :] + rot * sg_ref[tab, :]).astype(BF16)


def _proj_qk_t(x, g_pre, w_t, cg, sg, *, layer, seq, n_q_heads, tm):
    t, d = x.shape
    n = w_t.shape[1]
    seq_tiles = seq // tm
    table_spec = pl.BlockSpec((2 * HEAD_DIM, tm), lambda i: (0, i % seq_tiles))
    return pl.pallas_call(
        functools.partial(_proj_qk_t_kernel, n_q_heads=n_q_heads),
        grid=(t // tm,),
        in_specs=[
            pl.BlockSpec((tm, d), lambda i: (i, 0)),
            _resident(g_pre.shape),
            _resident_layer(w_t, layer),
            table_spec,
            table_spec,
        ],
        out_specs=pl.BlockSpec((n, tm), lambda i: (0, i)),
        out_shape=jax.ShapeDtypeStruct((n, t), BF16),
        compiler_params=_params("parallel"),
        name="proj_qk",
    )(x, g_pre, w_t, cg, sg)


def _proj_kernel(x_ref, g_ref, w_ref, *rest, epilogue):
    hn_ref = rest[-1]

    @pl.when(pl.program_id(1) == 0)
    def _():
        hn_ref[...] = _rms(x_ref[...], g_ref[...]).astype(BF16)

    epilogue(_dot(hn_ref[...], w_ref[...]), *rest[:-1])


def _proj(x, g_pre, w, epilogue, extra, extra_specs, *, layer, tm, tn, name):
    t, d = x.shape
    n = w.shape[2]
    return pl.pallas_call(
        functools.partial(_proj_kernel, epilogue=epilogue),
        grid=(t // tm, n // tn),
        in_specs=[
            pl.BlockSpec((tm, d), lambda i, j: (i, 0)),
            pl.BlockSpec((1, d), lambda i, j: (0, 0)),
            _layer_spec((d, tn), lambda i, j: (0, j), layer),
            *extra_specs,
        ],
        out_specs=pl.BlockSpec((tm, tn), lambda i, j: (i, j)),
        out_shape=jax.ShapeDtypeStruct((t, n), BF16),
        scratch_shapes=[pltpu.VMEM((tm, d), BF16)],
        compiler_params=_params("parallel", "arbitrary"),
        name=name,
    )(x, g_pre, w, *extra)


def _attn_exact(q, ka_ref, va_ref, kc):
    rows = q.shape[0]
    m = jnp.full((rows, 1), -jnp.inf, F32)
    l = jnp.zeros((rows, 1), F32)
    acc = jnp.zeros((rows, HEAD_DIM), F32)
    for c in range(ka_ref.shape[0] // kc):
        ks = slice(c * kc, (c + 1) * kc)
        s = _dot_nt(q, ka_ref[ks, :HEAD_DIM])
        m_new = jnp.maximum(m, jnp.max(s, axis=-1, keepdims=True))
        alpha = jnp.exp2(m - m_new)
        p = jnp.exp2(s - m_new)
        l = alpha * l + jnp.sum(p, axis=-1, keepdims=True)
        acc = alpha * acc + _dot(p.astype(BF16), va_ref[ks, :HEAD_DIM])
        m = m_new
    return acc * (1.0 / l)


def _attn_kernel(qt_ref, kt_ref, v_ref, o_ref, ka_ref, va_ref, kmax_ref, *, kc):
    tq = qt_ref.shape[1]
    s_len = v_ref.shape[0]
    n_chunks = s_len // kc
    lane = lax.broadcasted_iota(jnp.int32, (1, HEAD_DIM), 1)

    @pl.when(pl.program_id(2) == 0)
    def _():
        ones_col = jnp.broadcast_to(jnp.where(lane == 0, 1.0, 0.0), (kc, HEAD_DIM)).astype(BF16)
        k2max = jnp.zeros((1, 1), F32)
        for c in range(n_chunks):
            ks = slice(c * kc, (c + 1) * kc)
            kt = kt_ref[:, ks]
            kf = kt.astype(F32)
            k2max = jnp.maximum(k2max, jnp.max(jnp.sum(kf * kf, axis=0, keepdims=True),
                                               axis=1, keepdims=True))
            ka_ref[ks, :HEAD_DIM] = kt.T
            ka_ref[ks, HEAD_DIM:] = ones_col
            va_ref[ks, :HEAD_DIM] = v_ref[ks, :]
            va_ref[ks, HEAD_DIM:] = ones_col
        kmax_ref[...] = jnp.broadcast_to(jnp.sqrt(k2max), kmax_ref.shape)

    q = jnp.concatenate(
        [qt_ref[g * HEAD_DIM:(g + 1) * HEAD_DIM, :].T for g in range(GROUP)], axis=0)
    qf = q.astype(F32)
    shift = jnp.sqrt(jnp.sum(qf * qf, axis=-1, keepdims=True)) * kmax_ref[0:1, 0:1]
    shift_col = (-shift) * jnp.where(lane == 0, 1.0, 0.0)
    q_aug = jnp.concatenate([q, shift_col.astype(BF16)], axis=1)
    acc = jnp.zeros((GROUP * tq, 2 * HEAD_DIM), F32)
    for c in range(n_chunks):
        ks = slice(c * kc, (c + 1) * kc)
        p = jnp.exp2(_dot_nt(q_aug, ka_ref[ks, :])).astype(BF16)
        acc = acc + _dot(p, va_ref[ks, :])
    l = acc[:, HEAD_DIM:HEAD_DIM + 1]
    accurate = jnp.min(l) >= ATTN_MIN_DENOMINATOR

    def write(o):
        for g in range(GROUP):
            o_ref[:, g * HEAD_DIM:(g + 1) * HEAD_DIM] = o[g * tq:(g + 1) * tq].astype(BF16)

    @pl.when(accurate)
    def _():
        write(acc[:, :HEAD_DIM] * (1.0 / l))

    @pl.when(jnp.logical_not(accurate))
    def _():
        write(_attn_exact(q, ka_ref, va_ref, kc))


def _attention(qk_t, vu, *, batch, seq, q_width, v_col, tq, kc):
    t = vu.shape[0]
    gw = GROUP * HEAD_DIM
    k_blk0 = q_width // HEAD_DIM
    v_blk0 = v_col // HEAD_DIM
    nq = seq // tq
    return pl.pallas_call(
        functools.partial(_attn_kernel, kc=kc),
        grid=(batch, N_KV, nq),
        in_specs=[
            pl.BlockSpec((gw, tq), lambda b, h, i: (h, b * nq + i)),
            pl.BlockSpec((HEAD_DIM, seq), lambda b, h, i: (k_blk0 + h, b)),
            pl.BlockSpec((seq, HEAD_DIM), lambda b, h, i: (b, v_blk0 + h)),
        ],
        out_specs=pl.BlockSpec((tq, gw), lambda b, h, i: (b * nq + i, h)),
        out_shape=jax.ShapeDtypeStruct((t, q_width), BF16),
        scratch_shapes=[
            pltpu.VMEM((seq, 2 * HEAD_DIM), BF16),
            pltpu.VMEM((seq, 2 * HEAD_DIM), BF16),
            pltpu.VMEM((8, HEAD_DIM), F32),
        ],
        compiler_params=_params("parallel", "parallel", "arbitrary"),
        name="attention",
    )(qk_t, qk_t, vu)


def _four_chan_kernel(u_ref, dft_ref, o_ref, *, gdim):
    for g in range(u_ref.shape[1] // gdim):
        sl = slice(g * gdim, (g + 1) * gdim)
        r = _dot(u_ref[:, sl], dft_ref[...])
        o_ref[0, 0, :, sl] = r[:, :gdim].astype(BF16)
        o_ref[0, 1, :, sl] = r[:, gdim:].astype(BF16)


def _four_chan(uv, dft_c, *, batch, seq, f_width, tm):
    gdim = f_width // N_FOURIER_GROUPS
    ns = seq // tm
    return pl.pallas_call(
        functools.partial(_four_chan_kernel, gdim=gdim),
        grid=(batch, ns),
        in_specs=[
            pl.BlockSpec((tm, f_width), lambda b, i: (b * ns + i, 0)),
            pl.BlockSpec((gdim, 2 * gdim), lambda b, i: (0, 0)),
        ],
        out_specs=pl.BlockSpec((1, 2, tm, f_width), lambda b, i: (b, 0, i, 0)),
        out_shape=jax.ShapeDtypeStruct((batch, 2, seq, f_width), BF16),
        compiler_params=_params("parallel", "parallel"),
        name="fourier_chan",
    )(uv, dft_c)


def _four_seq_kernel(ch_ref, sh_ref, uc_ref, us_ref, flip_ref, o_ref):
    fr = o_ref.shape[3]
    p = _dot(ch_ref[0], uc_ref[0, 0])
    q = _dot(sh_ref[0], us_ref[0, 0])
    o_ref[0, 0, 0] = (p - q)[:fr].astype(BF16)
    o_ref[0, 1, 0] = _dot(flip_ref[...], (p + q).astype(BF16)).astype(BF16)


def _four_seq(ch, sh, flip, ucs, *, batch, seq, f_width):
    nt, rows, _ = ch.shape
    fr = flip.shape[0]
    half_spec = pl.BlockSpec((1, rows, seq), lambda b, t: (t, 0, 0))
    return pl.pallas_call(
        _four_seq_kernel,
        grid=(batch, nt),
        in_specs=[
            half_spec,
            half_spec,
            pl.BlockSpec((1, 1, seq, f_width), lambda b, t: (b, 0, 0, 0), pipeline_mode=pl.Buffered(1)),
            pl.BlockSpec((1, 1, seq, f_width), lambda b, t: (b, 1, 0, 0), pipeline_mode=pl.Buffered(1)),
            _resident(flip.shape),
        ],
        out_specs=pl.BlockSpec((1, 2, 1, fr, f_width), lambda b, t: (b, 0, t, 0, 0)),
        out_shape=jax.ShapeDtypeStruct((batch, 2, nt, fr, f_width), BF16),
        compiler_params=_params("parallel", "arbitrary"),
        name="fourier_seq",
    )(ch, sh, ucs, ucs, flip)


def _merge_kernel(att_ref, y_ref, g1_ref, g2_ref, x_ref, wa_ref, wf_ref, wm_ref, gp_ref, o_ref):
    a = _dot(att_ref[...], wa_ref[...])
    f = _dot(y_ref[...], wf_ref[...])
    merged = g1_ref[...].astype(F32) * a + g2_ref[...].astype(F32) * f
    y = _dot(merged.astype(BF16), wm_ref[...])
    o_ref[...] = x_ref[...] + _rms(y, gp_ref[...])


def _resident(shape):
    return pl.BlockSpec(shape, lambda *_: (0,) * len(shape), pipeline_mode=pl.Buffered(1))


def _layer_spec(block, index_map, layer, **kwargs):
    return pl.BlockSpec((None, *block), lambda *idx: (layer, *index_map(*idx)), **kwargs)


def _resident_layer(w, layer):
    return _layer_spec(w.shape[1:], lambda *_: (0,) * (w.ndim - 1), layer,
                       pipeline_mode=pl.Buffered(1))


def _merge(att, four, gates, x, wa, wf, wm, g_post, *, layer, seq, tm):
    t, d = x.shape
    tiles = seq // tm

    def four_block(i):
        it = i % tiles
        return (i - it + jnp.where(it < tiles // 2, it, tiles + tiles // 2 - 1 - it), 0)

    return pl.pallas_call(
        _merge_kernel,
        grid=(t // tm,),
        in_specs=[
            pl.BlockSpec((tm, att.shape[1]), lambda i: (i, 0)),
            pl.BlockSpec((tm, four.shape[1]), four_block),
            pl.BlockSpec((tm, d), lambda i: (i, 0)),
            pl.BlockSpec((tm, d), lambda i: (i, 1)),
            pl.BlockSpec((tm, d), lambda i: (i, 0)),
            _resident_layer(wa, layer),
            _resident_layer(wf, layer),
            _resident_layer(wm, layer),
            _resident(g_post.shape),
        ],
        out_specs=pl.BlockSpec((tm, d), lambda i: (i, 0)),
        out_shape=jax.ShapeDtypeStruct((t, d), F32),
        compiler_params=_params("parallel"),
        name="merge",
    )(att, four, gates, gates, x, wa, wf, wm, g_post)


def _norm_matmul_kernel(x_ref, g_ref, w_ref, o_ref):
    hn = _rms(x_ref[...], g_ref[...]).astype(BF16)
    o_ref[...] = _dot(hn, w_ref[...]).astype(BF16)


def _norm_matmul(x, g, w, *, layer, tm):
    t, d = x.shape
    n = w.shape[2]
    return pl.pallas_call(
        _norm_matmul_kernel,
        grid=(t // tm,),
        in_specs=[
            pl.BlockSpec((tm, d), lambda i: (i, 0)),
            _resident(g.shape),
            _resident_layer(w, layer),
        ],
        out_specs=pl.BlockSpec((tm, n), lambda i: (i, 0)),
        out_shape=jax.ShapeDtypeStruct((t, n), BF16),
        compiler_params=_params("parallel"),
        name="mem_kv",
    )(x, g, w)


def _xattn_kernel(x_ref, gpre_ref, wq_ref, k_ref, v_ref, wo_ref, gpost_ref, o_ref):
    x = x_ref[...]
    hn = _rms(x, gpre_ref[...]).astype(BF16)
    q = (_dot(hn, wq_ref[...]) * (X_HEAD_DIM ** -0.5)).astype(BF16)
    outs = []
    for h in range(X_HEADS):
        sl = slice(h * X_HEAD_DIM, (h + 1) * X_HEAD_DIM)
        s = _dot_nt(q[:, sl], k_ref[:, sl])
        m = jnp.max(s, axis=-1, keepdims=True)
        p = jnp.exp(s - m)
        l = jnp.sum(p, axis=-1, keepdims=True)
        outs.append(_dot(p.astype(BF16), v_ref[:, sl]) * (1.0 / l))
    o = jnp.concatenate(outs, axis=1).astype(BF16)
    y = _dot(o, wo_ref[...])
    o_ref[...] = x + _rms(y, gpost_ref[...])


def _xattn(x, g_pre, wq, kv, wo, g_post, *, layer, batch, seq, mem_tokens, tm):
    t, d = x.shape
    xw = wq.shape[2]
    ns = seq // tm
    return pl.pallas_call(
        _xattn_kernel,
        grid=(batch, ns),
        in_specs=[
            pl.BlockSpec((tm, d), lambda b, i: (b * ns + i, 0)),
            _resident(g_pre.shape),
            _resident_layer(wq, layer),
            pl.BlockSpec((mem_tokens, xw), lambda b, i: (b, 0)),
            pl.BlockSpec((mem_tokens, xw), lambda b, i: (b, 1)),
            _resident_layer(wo, layer),
            _resident(g_post.shape),
        ],
        out_specs=pl.BlockSpec((tm, d), lambda b, i: (b * ns + i, 0)),
        out_shape=jax.ShapeDtypeStruct((t, d), F32),
        compiler_params=_params("parallel", "parallel"),
        name="xattn",
    )(x, g_pre, wq, kv, kv, wo, g_post)


def _gelu_tanh(x):
    return 0.5 * x * (1.0 + jnp.tanh(math.sqrt(2.0 / math.pi) * (x + 0.044715 * (x * x * x))))


def _ffn_kernel(xp_ref, x_ref, xn_ref, gpre_ref, wg_ref, wv_ref, cwg_ref, cwv_ref, cbg_ref, cbv_ref,
                wd_ref, gpost_ref, o_ref, hn_ref, ug_ref, uv_ref, *, tiles_per_seq):
    i = pl.program_id(0)
    j = pl.program_id(1)
    tm = x_ref.shape[0]
    halo = xp_ref.shape[0]

    @pl.when(j == 0)
    def _():
        g = gpre_ref[...]
        pos = i % tiles_per_seq
        hp = jnp.where(pos == 0, 0.0, _rms(xp_ref[...], g))
        hx = jnp.where(pos == tiles_per_seq - 1, 0.0, _rms(xn_ref[...], g))
        hn_ref[0:halo, :] = hp.astype(BF16)
        hn_ref[halo + tm:, :] = hx.astype(BF16)
        for r in range(0, tm, FFN_NORM_ROWS):
            rows = slice(r, r + FFN_NORM_ROWS)
            hn_ref[halo + r:halo + r + FFN_NORM_ROWS, :] = _rms(x_ref[rows, :], g).astype(BF16)
        o_ref[...] = jnp.zeros_like(o_ref)

    hn = hn_ref[...]
    ug_ref[...] = _dot(hn, wg_ref[...])
    uv_ref[...] = _dot(hn, wv_ref[...])

    def conv(u_ref, cw_ref, cb_ref):
        return (u_ref[halo - 1:halo - 1 + tm, :] * cw_ref[0:1, :]
                + u_ref[halo:halo + tm, :] * cw_ref[1:2, :]
                + u_ref[halo + 1:halo + 1 + tm, :] * cw_ref[2:3, :]
                + cb_ref[...])

    act = _gelu_tanh(conv(ug_ref, cwg_ref, cbg_ref)) * conv(uv_ref, cwv_ref, cbv_ref)
    o_ref[...] += _dot(act.astype(BF16), wd_ref[...])

    @pl.when(j == pl.num_programs(1) - 1)
    def _():
        for r in range(0, tm, FFN_NORM_ROWS):
            rows = slice(r, r + FFN_NORM_ROWS)
            o_ref[rows, :] = x_ref[rows, :] + _rms(o_ref[rows, :], gpost_ref[...])


def _ffn(x, g_pre, w_up, conv_w, conv_b, w_down, g_post, *, layer, seq, tm, tf):
    t, d = x.shape
    d_ff = w_down.shape[1]
    nf = d_ff // tf
    halo = BF16_SUBLANES
    hb = tm // halo
    n_hblk = t // halo
    kern = functools.partial(_ffn_kernel, tiles_per_seq=seq // tm)
    return pl.pallas_call(
        kern,
        grid=(t // tm, nf),
        in_specs=[
            pl.BlockSpec((halo, d), lambda i, j: (jnp.maximum(i * hb - 1, 0), 0)),
            pl.BlockSpec((tm, d), lambda i, j: (i, 0)),
            pl.BlockSpec((halo, d), lambda i, j: (jnp.minimum((i + 1) * hb, n_hblk - 1), 0)),
            pl.BlockSpec((1, d), lambda i, j: (0, 0)),
            _layer_spec((d, tf), lambda i, j: (0, j), layer),
            _layer_spec((d, tf), lambda i, j: (0, nf + j), layer),
            pl.BlockSpec((3, tf), lambda i, j: (0, j)),
            pl.BlockSpec((3, tf), lambda i, j: (0, nf + j)),
            pl.BlockSpec((1, tf), lambda i, j: (0, j)),
            pl.BlockSpec((1, tf), lambda i, j: (0, nf + j)),
            _layer_spec((tf, d), lambda i, j: (j, 0), layer),
            pl.BlockSpec((1, d), lambda i, j: (0, 0)),
        ],
        out_specs=pl.BlockSpec((tm, d), lambda i, j: (i, 0)),
        out_shape=jax.ShapeDtypeStruct((t, d), F32),
        scratch_shapes=[
            pltpu.VMEM((tm + 2 * halo, d), BF16),
            pltpu.VMEM((tm + 2 * halo, tf), F32),
            pltpu.VMEM((tm + 2 * halo, tf), F32),
        ],
        compiler_params=_params("parallel", "arbitrary"),
        name="conv_ffn",
    )(x, x, x, g_pre, w_up, w_up, conv_w, conv_w, conv_b, conv_b, w_down, g_post)


def _head_lane_order():
    quarter = HEAD_DIM // 4
    blocks = (0, 2, 1, 3)
    return jnp.concatenate([jnp.arange(b * quarter, (b + 1) * quarter) for b in blocks])


def _rope_tables(seq):
    half = HEAD_DIM // 2
    pos = jnp.arange(seq)
    row = (pos // GRID_W).astype(F32)
    col = (pos % GRID_W).astype(F32)
    inv_freq = 1.0 / (ROPE_THETA ** (jnp.arange(0, half, 2, dtype=F32) / half))
    lane = _head_lane_order()
    freq = inv_freq[(lane % half) % (half // 2)]
    ids = jnp.where(lane[None, :] < half, row[:, None], col[:, None])
    ang = ids * freq[None, :]
    sign = jnp.where((lane % half) < half // 2, -1.0, 1.0)
    return jnp.cos(ang), jnp.sin(ang) * sign[None, :]


def _dft_tables(seq, gdim, fr):
    def cs(k, n, period):
        r = (k[..., None] * n) % period
        ang = r.astype(F32) * (2.0 * math.pi / period)
        return jnp.cos(ang), jnp.sin(ang)
    c = jnp.arange(gdim, dtype=jnp.int32)
    cc, sc = (t * gdim ** -0.5 for t in cs(c, c, gdim))
    rows = fr + FOUR_EXTRA_ROWS
    k = jnp.arange(seq // 2 // fr, dtype=jnp.int32)[:, None] * fr + jnp.arange(rows, dtype=jnp.int32)
    ca, sa = cs(k, GRID_W * jnp.arange(seq // GRID_W, dtype=jnp.int32), seq)
    cb, sb = cs(k, jnp.arange(GRID_W, dtype=jnp.int32), seq)
    scale = seq ** -0.5
    ch = ((ca[..., :, None] * cb[..., None, :] - sa[..., :, None] * sb[..., None, :]) * scale)
    sh = ((sa[..., :, None] * cb[..., None, :] + ca[..., :, None] * sb[..., None, :]) * scale)
    ch, sh = ch.reshape(*k.shape, seq), sh.reshape(*k.shape, seq)
    flip = (jnp.arange(rows)[None, :] == fr - jnp.arange(fr)[:, None]).astype(BF16)
    return (jnp.concatenate([cc, sc], axis=1).astype(BF16), ch.astype(BF16), sh.astype(BF16), flip)


def _tiles(seq, d_ff):
    def pick(n, pref):
        return pref if n % pref == 0 else n
    tf = next(c for c in (512, 256, 128) if d_ff % c == 0)
    return dict(
        qk_tm=pick(seq, 512), mixer_tm=pick(seq, 1024), uv_tn=768, gate_tn=1024,
        attn_tq=pick(seq, 512), attn_kc=pick(seq, 256),
        four_tm=pick(seq, 512),
        merge_tm=pick(seq, 256),
        xattn_tm=pick(seq, 512),
        ffn_tm=pick(seq, 512), ffn_tf=tf,
    )


def kernel(x, mem, mix_pre_g, w_in, q_norm_g, k_norm_g, w_attn_o, w_four_o, w_gate, b_gate, w_mix_o,
           mix_post_g, xa_pre_g, mem_norm_g, w_xq, w_xkv, w_xo, xa_post_g, ffn_pre_g, w_up, conv_w,
           conv_b, w_down, ffn_post_g):
    batch, seq, d = x.shape
    mem_tokens = mem.shape[1]
    depth = w_in.shape[0]
    q_width = w_attn_o.shape[1]
    f_width = w_four_o.shape[1]
    kv_width = (w_in.shape[2] - q_width - f_width) // 2
    n_heads = q_width // HEAD_DIM
    assert n_heads == N_KV * GROUP and kv_width == N_KV * HEAD_DIM
    assert seq % GRID_W == 0 and w_xq.shape[2] == X_HEADS * X_HEAD_DIM
    d_ff = w_down.shape[1]
    ts = _tiles(seq, d_ff)
    qk_width = q_width + kv_width
    u_col = q_width + 2 * kv_width
    tm = ts["mixer_tm"]
    assert (kv_width + f_width) % ts["uv_tn"] == 0 and w_gate.shape[2] % ts["gate_tn"] == 0

    cos_t, sin_t = _rope_tables(seq)
    assert seq % (2 * ts["merge_tm"]) == 0
    dft_c, dft_ch, dft_sh, dft_flip = _dft_tables(seq, f_width // N_FOURIER_GROUPS, ts["merge_tm"])
    order = _head_lane_order()
    partner = jnp.roll(jnp.arange(HEAD_DIM), HEAD_DIM // 2)
    bias_spec = pl.BlockSpec((1, ts["gate_tn"]), lambda i, j: (0, j))

    row = lambda v: v.reshape(1, -1)
    xf = x.reshape(batch * seq, d)
    memf = mem.reshape(batch * mem_tokens, d)

    quarter = HEAD_DIM // 4
    w_qk_t = w_in[:, :, :qk_width].astype(BF16).reshape(depth, d, qk_width // HEAD_DIM, 2, 2, quarter)
    w_qk_t = w_qk_t.transpose(0, 2, 4, 3, 5, 1).reshape(depth, qk_width, d)
    w_uv = jnp.concatenate([w_in[:, :, u_col:], w_in[:, :, qk_width:u_col]], axis=2).astype(BF16)
    w_gate_b, w_attn_o_b, w_four_o_b, w_mix_o_b, w_xq_b, w_xkv_b, w_xo_b, w_up_b, w_down_b = (
        w.astype(BF16) for w in (w_gate, w_attn_o, w_four_o, w_mix_o, w_xq, w_xkv, w_xo, w_up, w_down))

    for l in range(depth):
        q_gain = q_norm_g[l][order] * (HEAD_DIM ** -0.5 * math.log2(math.e))
        k_gain = k_norm_g[l][order]
        cg = jnp.concatenate([g[:, None] * cos_t.T for g in (q_gain, k_gain)], axis=0)
        sg = jnp.concatenate([g[partner][:, None] * sin_t.T for g in (q_gain, k_gain)], axis=0)
        g_pre = row(mix_pre_g[l])
        qk = _proj_qk_t(xf, g_pre, w_qk_t, cg, sg, layer=l, seq=seq, n_q_heads=n_heads,
                        tm=ts["qk_tm"])
        uv = _proj(xf, g_pre, w_uv, _epilogue_plain, (), (), layer=l, tm=tm, tn=ts["uv_tn"],
                   name="proj_uv")
        gates = _proj(xf, g_pre, w_gate_b, _epilogue_gate, (row(b_gate[l]),), (bias_spec,),
                      layer=l, tm=tm, tn=ts["gate_tn"], name="proj_gate")
        att = _attention(qk, uv, batch=batch, seq=seq, q_width=q_width, v_col=f_width,
                         tq=ts["attn_tq"], kc=ts["attn_kc"])
        ucs = _four_chan(uv, dft_c, batch=batch, seq=seq, f_width=f_width, tm=ts["four_tm"])
        four = _four_seq(dft_ch, dft_sh, dft_flip, ucs, batch=batch, seq=seq, f_width=f_width)
        xf = _merge(att, four.reshape(batch * seq, f_width), gates, xf, w_attn_o_b, w_four_o_b,
                    w_mix_o_b, row(mix_post_g[l]), layer=l, seq=seq, tm=ts["merge_tm"])

        kv = _norm_matmul(memf, row(mem_norm_g[l]), w_xkv_b, layer=l, tm=mem_tokens)
        xf = _xattn(xf, row(xa_pre_g[l]), w_xq_b, kv, w_xo_b, row(xa_post_g[l]), layer=l,
                    batch=batch, seq=seq, mem_tokens=mem_tokens, tm=ts["xattn_tm"])

        xf = _ffn(xf, row(ffn_pre_g[l]), w_up_b, conv_w[l], row(conv_b[l]), w_down_b,
                  row(ffn_post_g[l]), layer=l, seq=seq, tm=ts["ffn_tm"], tf=ts["ffn_tf"])

    return xf.reshape(batch, seq, d)
```

```python
import functools
import math

import jax
import jax.numpy as jnp
from jax import lax
from jax.experimental import pallas as pl
from jax.experimental.pallas import tpu as pltpu

HEAD_DIM = 128
N_KV = 4
GROUP = 4
GRID_W = 64
ROPE_THETA = 10000.0
N_FOURIER_GROUPS = 4
X_HEADS = 4
X_HEAD_DIM = 128
EPS = 1e-6

V7X_VMEM_LIMIT_BYTES = 56 * 1024 * 1024
BF16_SUBLANES = 16
ATTN_MIN_DENOMINATOR = 2.0 ** -60
FOUR_EXTRA_ROWS = BF16_SUBLANES
FFN_NORM_ROWS = 128

F32 = jnp.float32
BF16 = jnp.bfloat16


def _params(*sem):
    return pltpu.CompilerParams(dimension_semantics=sem, vmem_limit_bytes=V7X_VMEM_LIMIT_BYTES)


def _rms(xf, g):
    ms = jnp.mean(xf * xf, axis=-1, keepdims=True)
    return xf * lax.rsqrt(ms + EPS) * g


def _dot(a, b):
    return jnp.dot(a, b, preferred_element_type=F32)


def _dot_nt(a, b):
    return lax.dot_general(a, b, (((1,), (1,)), ((), ())), preferred_element_type=F32)


def _epilogue_plain(z, o_ref):
    o_ref[...] = z.astype(BF16)


def _epilogue_gate(z, b_ref, o_ref):
    o_ref[...] = (0.5 * jnp.tanh(0.5 * (z + b_ref[...])) + 0.5).astype(BF16)


def _proj_qk_t_kernel(x_ref, g_ref, wt_ref, cg_ref, sg_ref, o_ref, *, n_q_heads):
    hn = _rms(x_ref[...], g_ref[...]).astype(BF16)
    zt = _dot_nt(wt_ref[...], hn)
    half = HEAD_DIM // 2
    for h in range(zt.shape[0] // HEAD_DIM):
        tab = slice(0, HEAD_DIM) if h < n_q_heads else slice(HEAD_DIM, 2 * HEAD_DIM)
        y = zt[h * HEAD_DIM:(h + 1) * HEAD_DIM, :]
        y = y * lax.rsqrt(jnp.mean(y * y, axis=0, keepdims=True) + EPS)
        rot = jnp.concatenate([y[half:], y[:half]], axis=0)
        o_ref[h * HEAD_DIM:(h + 1) * HEAD_DIM, :] = (y * cg_ref[tab, :] + rot * sg_ref[tab, :]).astype(BF16)


def _proj_qk_t(x, g_pre, w_t, cg, sg, *, layer, seq, n_q_heads, tm):
    t, d = x.shape
    n = w_t.shape[1]
    seq_tiles = seq // tm
    table_spec = pl.BlockSpec((2 * HEAD_DIM, tm), lambda i: (0, i % seq_tiles))
    return pl.pallas_call(
        functools.partial(_proj_qk_t_kernel, n_q_heads=n_q_heads),
        grid=(t // tm,),
        in_specs=[
            pl.BlockSpec((tm, d), lambda i: (i, 0)),
            _resident(g_pre.shape),
            _resident_layer(w_t, layer),
            table_spec,
            table_spec,
        ],
        out_specs=pl.BlockSpec((n, tm), lambda i: (0, i)),
        out_shape=jax.ShapeDtypeStruct((n, t), BF16),
        compiler_params=_params("parallel"),
        name="proj_qk",
    )(x, g_pre, w_t, cg, sg)


def _proj_kernel(x_ref, g_ref, w_ref, *rest, epilogue):
    hn_ref = rest[-1]

    @pl.when(pl.program_id(1) == 0)
    def _():
        hn_ref[...] = _rms(x_ref[...], g_ref[...]).astype(BF16)

    epilogue(_dot(hn_ref[...], w_ref[...]), *rest[:-1])


def _proj(x, g_pre, w, epilogue, extra, extra_specs, *, layer, tm, tn, name):
    t, d = x.shape
    n = w.shape[2]
    return pl.pallas_call(
        functools.partial(_proj_kernel, epilogue=epilogue),
        grid=(t // tm, n // tn),
        in_specs=[
            pl.BlockSpec((tm, d), lambda i, j: (i, 0)),
            pl.BlockSpec((1, d), lambda i, j: (0, 0)),
            _layer_spec((d, tn), lambda i, j: (0, j), layer),
            *extra_specs,
        ],
        out_specs=pl.BlockSpec((tm, tn), lambda i, j: (i, j)),
        out_shape=jax.ShapeDtypeStruct((t, n), BF16),
        scratch_shapes=[pltpu.VMEM((tm, d), BF16)],
        compiler_params=_params("parallel", "arbitrary"),
        name=name,
    )(x, g_pre, w, *extra)


def _attn_exact(q, ka_ref, va_ref, kc):
    rows = q.shape[0]
    m = jnp.full((rows, 1), -jnp.inf, F32)
    l = jnp.zeros((rows, 1), F32)
    acc = jnp.zeros((rows, HEAD_DIM), F32)
    for c in range(ka_ref.shape[0] // kc):
        ks = slice(c * kc, (c + 1) * kc)
        s = _dot_nt(q, ka_ref[ks, :HEAD_DIM])
        m_new = jnp.maximum(m, jnp.max(s, axis=-1, keepdims=True))
        alpha = jnp.exp2(m - m_new)
        p = jnp.exp2(s - m_new)
        l = alpha * l + jnp.sum(p, axis=-1, keepdims=True)
        acc = alpha * acc + _dot(p.astype(BF16), va_ref[ks, :HEAD_DIM])
        m = m_new
    return acc * (1.0 / l)


def _attn_kernel(qt_ref, kt_ref, v_ref, o_ref, ka_ref, va_ref, kmax_ref, *, kc):
    tq = qt_ref.shape[1]
    s_len = v_ref.shape[0]
    n_chunks = s_len // kc
    lane = lax.broadcasted_iota(jnp.int32, (1, HEAD_DIM), 1)

    @pl.when(pl.program_id(2) == 0)
    def _():
        ones_col = jnp.broadcast_to(jnp.where(lane == 0, 1.0, 0.0), (kc, HEAD_DIM)).astype(BF16)
        k2max = jnp.zeros((1, 1), F32)
        for c in range(n_chunks):
            ks = slice(c * kc, (c + 1) * kc)
            kt = kt_ref[:, ks]
            kf = kt.astype(F32)
            k2max = jnp.maximum(k2max, jnp.max(jnp.sum(kf * kf, axis=0, keepdims=True),
                                               axis=1, keepdims=True))
            ka_ref[ks, :HEAD_DIM] = kt.T
            ka_ref[ks, HEAD_DIM:] = ones_col
            va_ref[ks, :HEAD_DIM] = v_ref[ks, :]
            va_ref[ks, HEAD_DIM:] = ones_col
        kmax_ref[...] = jnp.broadcast_to(jnp.sqrt(k2max), kmax_ref.shape)

    q = jnp.concatenate(
        [qt_ref[g * HEAD_DIM:(g + 1) * HEAD_DIM, :].T for g in range(GROUP)], axis=0)
    qf = q.astype(F32)
    shift = jnp.sqrt(jnp.sum(qf * qf, axis=-1, keepdims=True)) * kmax_ref[0:1, 0:1]
    shift_col = (-shift) * jnp.where(lane == 0, 1.0, 0.0)
    q_aug = jnp.concatenate([q, shift_col.astype(BF16)], axis=1)
    acc = jnp.zeros((GROUP * tq, 2 * HEAD_DIM), F32)
    for c in range(n_chunks):
        ks = slice(c * kc, (c + 1) * kc)
        p = jnp.exp2(_dot_nt(q_aug, ka_ref[ks, :])).astype(BF16)
        acc = acc + _dot(p, va_ref[ks, :])
    l = acc[:, HEAD_DIM:HEAD_DIM + 1]
    accurate = jnp.min(l) >= ATTN_MIN_DENOMINATOR

    def write(o):
        for g in range(GROUP):
            o_ref[:, g * HEAD_DIM:(g + 1) * HEAD_DIM] = o[g * tq:(g + 1) * tq].astype(BF16)

    @pl.when(accurate)
    def _():
        write(acc[:, :HEAD_DIM] * (1.0 / l))

    @pl.when(jnp.logical_not(accurate))
    def _():
        write(_attn_exact(q, ka_ref, va_ref, kc))


def _attention(qk_t, vu, *, batch, seq, q_width, v_col, tq, kc):
    t = vu.shape[0]
    gw = GROUP * HEAD_DIM
    k_blk0 = q_width // HEAD_DIM
    v_blk0 = v_col // HEAD_DIM
    nq = seq // tq
    return pl.pallas_call(
        functools.partial(_attn_kernel, kc=kc),
        grid=(batch, N_KV, nq),
        in_specs=[
            pl.BlockSpec((gw, tq), lambda b, h, i: (h, b * nq + i)),
            pl.BlockSpec((HEAD_DIM, seq), lambda b, h, i: (k_blk0 + h, b)),
            pl.BlockSpec((seq, HEAD_DIM), lambda b, h, i: (b, v_blk0 + h)),
        ],
        out_specs=pl.BlockSpec((tq, gw), lambda b, h, i: (b * nq + i, h)),
        out_shape=jax.ShapeDtypeStruct((t, q_width), BF16),
        scratch_shapes=[
            pltpu.VMEM((seq, 2 * HEAD_DIM), BF16),
            pltpu.VMEM((seq, 2 * HEAD_DIM), BF16),
            pltpu.VMEM((8, HEAD_DIM), F32),
        ],
        compiler_params=_params("parallel", "parallel", "arbitrary"),
        name="attention",
    )(qk_t, qk_t, vu)


def _four_chan_kernel(u_ref, p_ref, e_ref, m_ref, dft_ref, flip_ref, o_ref, mid_ref, *, gdim):
    i = pl.program_id(1)
    ext = jnp.where(i == 0, 0.0, e_ref[...].astype(F32)).astype(BF16)
    partner = _dot(flip_ref[...], jnp.concatenate([p_ref[...], ext], axis=0))
    u = u_ref[...].astype(F32)
    even = (u + partner).astype(BF16)
    odd = (u - partner).astype(BF16)
    for g in range(u_ref.shape[1] // gdim):
        sl = slice(g * gdim, (g + 1) * gdim)
        o_ref[0, 0, :, sl] = _dot(even[:, sl], dft_ref[:, :gdim]).astype(BF16)
        o_ref[0, 1, :, sl] = _dot(odd[:, sl], dft_ref[:, gdim:]).astype(BF16)

    @pl.when(i == 0)
    def _():
        for g in range(u_ref.shape[1] // gdim):
            sl = slice(g * gdim, (g + 1) * gdim)
            mid_ref[0, :, sl] = _dot(m_ref[:, sl], dft_ref[:, :gdim])


def _four_chan(uv, dft_c, flip, *, batch, seq, f_width):
    gdim = f_width // N_FOURIER_GROUPS
    fr = flip.shape[0]
    nb = seq // fr
    ext = FOUR_EXTRA_ROWS
    last_ext = uv.shape[0] // ext - 1
    return pl.pallas_call(
        functools.partial(_four_chan_kernel, gdim=gdim),
        grid=(batch, nb // 2),
        in_specs=[
            pl.BlockSpec((fr, f_width), lambda b, i: (b * nb + i, 0)),
            pl.BlockSpec((fr, f_width), lambda b, i: (b * nb + nb - 1 - i, 0)),
            pl.BlockSpec((ext, f_width),
                         lambda b, i: (jnp.minimum(((b + 1) * nb - i) * (fr // ext), last_ext), 0)),
            pl.BlockSpec((ext, f_width), lambda b, i: ((2 * b + 1) * (seq // 2 // ext), 0)),
            _resident(dft_c.shape),
            _resident(flip.shape),
        ],
        out_specs=[
            pl.BlockSpec((1, 2, fr, f_width), lambda b, i: (b, 0, i, 0)),
            pl.BlockSpec((1, ext, f_width), lambda b, i: (b, 0, 0)),
        ],
        out_shape=[
            jax.ShapeDtypeStruct((batch, 2, seq // 2, f_width), BF16),
            jax.ShapeDtypeStruct((batch, ext, f_width), F32),
        ],
        compiler_params=_params("parallel", "arbitrary"),
        name="fourier_chan",
    )(uv, uv, uv, uv, dft_c, flip)


def _four_seq_kernel(ch_ref, sh_ref, uc_ref, us_ref, mid_ref, flip_ref, o_ref, *, seq):
    fr = o_ref.shape[3]
    rows = ch_ref.shape[1]
    p = _dot(ch_ref[0], uc_ref[0, 0])
    q = _dot(sh_ref[0], us_ref[0, 0])
    k = pl.program_id(1) * fr + lax.broadcasted_iota(jnp.int32, (rows, 1), 0)
    p = p + jnp.where(k % 2 == 0, seq ** -0.5, -(seq ** -0.5)) * mid_ref[0, 0:1, :]
    o_ref[0, 0, 0] = (p - q)[:fr].astype(BF16)
    o_ref[0, 1, 0] = _dot(flip_ref[...], (p + q).astype(BF16)).astype(BF16)


def _four_seq(ch, sh, flip, ucs, mid, *, batch, seq, f_width):
    nt, rows, half = ch.shape
    fr = flip.shape[0]
    table_spec = pl.BlockSpec((1, rows, half), lambda b, t: (t, 0, 0))
    return pl.pallas_call(
        functools.partial(_four_seq_kernel, seq=seq),
        grid=(batch, nt),
        in_specs=[
            table_spec,
            table_spec,
            pl.BlockSpec((1, 1, half, f_width), lambda b, t: (b, 0, 0, 0)),
            pl.BlockSpec((1, 1, half, f_width), lambda b, t: (b, 1, 0, 0)),
            pl.BlockSpec((1, FOUR_EXTRA_ROWS, f_width), lambda b, t: (b, 0, 0)),
            _resident(flip.shape),
        ],
        out_specs=pl.BlockSpec((1, 2, 1, fr, f_width), lambda b, t: (b, 0, t, 0, 0)),
        out_shape=jax.ShapeDtypeStruct((batch, 2, nt, fr, f_width), BF16),
        compiler_params=_params("parallel", "arbitrary"),
        name="fourier_seq",
    )(ch, sh, ucs, ucs, mid, flip)


def _merge_kernel(att_ref, y_ref, g1_ref, g2_ref, x_ref, wa_ref, wf_ref, wm_ref, gp_ref, o_ref):
    a = _dot(att_ref[...], wa_ref[...])
    f = _dot(y_ref[...], wf_ref[...])
    merged = g1_ref[...].astype(F32) * a + g2_ref[...].astype(F32) * f
    y = _dot(merged.astype(BF16), wm_ref[...])
    o_ref[...] = x_ref[...] + _rms(y, gp_ref[...])


def _resident(shape):
    return pl.BlockSpec(shape, lambda *_: (0,) * len(shape), pipeline_mode=pl.Buffered(1))


def _layer_spec(block, index_map, layer, **kwargs):
    return pl.BlockSpec((None, *block), lambda *idx: (layer, *index_map(*idx)), **kwargs)


def _resident_layer(w, layer):
    return _layer_spec(w.shape[1:], lambda *_: (0,) * (w.ndim - 1), layer,
                       pipeline_mode=pl.Buffered(1))


def _merge(att, four, gates, x, wa, wf, wm, g_post, *, layer, seq, tm):
    t, d = x.shape
    tiles = seq // tm

    def four_block(i):
        it = i % tiles
        return (i - it + jnp.where(it < tiles // 2, it, tiles + tiles // 2 - 1 - it), 0)

    return pl.pallas_call(
        _merge_kernel,
        grid=(t // tm,),
        in_specs=[
            pl.BlockSpec((tm, att.shape[1]), lambda i: (i, 0)),
            pl.BlockSpec((tm, four.shape[1]), four_block),
            pl.BlockSpec((tm, d), lambda i: (i, 0)),
            pl.BlockSpec((tm, d), lambda i: (i, 1)),
            pl.BlockSpec((tm, d), lambda i: (i, 0)),
            _resident_layer(wa, layer),
            _resident_layer(wf, layer),
            _resident_layer(wm, layer),
            _resident(g_post.shape),
        ],
        out_specs=pl.BlockSpec((tm, d), lambda i: (i, 0)),
        out_shape=jax.ShapeDtypeStruct((t, d), F32),
        compiler_params=_params("parallel"),
        name="merge",
    )(att, four, gates, gates, x, wa, wf, wm, g_post)


def _norm_matmul_kernel(x_ref, g_ref, w_ref, o_ref):
    hn = _rms(x_ref[...], g_ref[...]).astype(BF16)
    o_ref[...] = _dot(hn, w_ref[...]).astype(BF16)


def _norm_matmul(x, g, w, *, layer, tm):
    t, d = x.shape
    n = w.shape[2]
    return pl.pallas_call(
        _norm_matmul_kernel,
        grid=(t // tm,),
        in_specs=[
            pl.BlockSpec((tm, d), lambda i: (i, 0)),
            _resident(g.shape),
            _resident_layer(w, layer),
        ],
        out_specs=pl.BlockSpec((tm, n), lambda i: (i, 0)),
        out_shape=jax.ShapeDtypeStruct((t, n), BF16),
        compiler_params=_params("parallel"),
        name="mem_kv",
    )(x, g, w)


def _xattn_kernel(x_ref, gpre_ref, wq_ref, k_ref, v_ref, wo_ref, gpost_ref, o_ref):
    x = x_ref[...]
    hn = _rms(x, gpre_ref[...]).astype(BF16)
    q = (_dot(hn, wq_ref[...]) * (X_HEAD_DIM ** -0.5)).astype(BF16)
    outs = []
    for h in range(X_HEADS):
        sl = slice(h * X_HEAD_DIM, (h + 1) * X_HEAD_DIM)
        s = _dot_nt(q[:, sl], k_ref[:, sl])
        m = jnp.max(s, axis=-1, keepdims=True)
        p = jnp.exp(s - m)
        l = jnp.sum(p, axis=-1, keepdims=True)
        outs.append(_dot(p.astype(BF16), v_ref[:, sl]) * (1.0 / l))
    o = jnp.concatenate(outs, axis=1).astype(BF16)
    y = _dot(o, wo_ref[...])
    o_ref[...] = x + _rms(y, gpost_ref[...])


def _xattn(x, g_pre, wq, kv, wo, g_post, *, layer, batch, seq, mem_tokens, tm):
    t, d = x.shape
    xw = wq.shape[2]
    ns = seq // tm
    return pl.pallas_call(
        _xattn_kernel,
        grid=(batch, ns),
        in_specs=[
            pl.BlockSpec((tm, d), lambda b, i: (b * ns + i, 0)),
            _resident(g_pre.shape),
            _resident_layer(wq, layer),
            pl.BlockSpec((mem_tokens, xw), lambda b, i: (b, 0)),
            pl.BlockSpec((mem_tokens, xw), lambda b, i: (b, 1)),
            _resident_layer(wo, layer),
            _resident(g_post.shape),
        ],
        out_specs=pl.BlockSpec((tm, d), lambda b, i: (b * ns + i, 0)),
        out_shape=jax.ShapeDtypeStruct((t, d), F32),
        compiler_params=_params("parallel", "parallel"),
        name="xattn",
    )(x, g_pre, wq, kv, kv, wo, g_post)


def _gelu_tanh(x):
    return 0.5 * x * (1.0 + jnp.tanh(math.sqrt(2.0 / math.pi) * (x + 0.044715 * (x * x * x))))


def _ffn_kernel(xp_ref, x_ref, xn_ref, gpre_ref, wg_ref, wv_ref, cwg_ref, cwv_ref, cbg_ref, cbv_ref,
                wd_ref, gpost_ref, o_ref, hn_ref, ug_ref, uv_ref, *, tiles_per_seq):
    i = pl.program_id(0)
    j = pl.program_id(1)
    tm = x_ref.shape[0]
    halo = xp_ref.shape[0]

    @pl.when(j == 0)
    def _():
        g = gpre_ref[...]
        pos = i % tiles_per_seq
        hp = jnp.where(pos == 0, 0.0, _rms(xp_ref[...], g))
        hx = jnp.where(pos == tiles_per_seq - 1, 0.0, _rms(xn_ref[...], g))
        hn_ref[0:halo, :] = hp.astype(BF16)
        hn_ref[halo + tm:, :] = hx.astype(BF16)
        for r in range(0, tm, FFN_NORM_ROWS):
            rows = slice(r, r + FFN_NORM_ROWS)
            hn_ref[halo + r:halo + r + FFN_NORM_ROWS, :] = _rms(x_ref[rows, :], g).astype(BF16)
        o_ref[...] = jnp.zeros_like(o_ref)

    hn = hn_ref[...]
    ug_ref[...] = _dot(hn, wg_ref[...])
    uv_ref[...] = _dot(hn, wv_ref[...])

    def conv(u_ref, cw_ref, cb_ref):
        return (u_ref[halo - 1:halo - 1 + tm, :] * cw_ref[0:1, :]
                + u_ref[halo:halo + tm, :] * cw_ref[1:2, :]
                + u_ref[halo + 1:halo + 1 + tm, :] * cw_ref[2:3, :]
                + cb_ref[...])

    act = _gelu_tanh(conv(ug_ref, cwg_ref, cbg_ref)) * conv(uv_ref, cwv_ref, cbv_ref)
    o_ref[...] += _dot(act.astype(BF16), wd_ref[...])

    @pl.when(j == pl.num_programs(1) - 1)
    def _():
        for r in range(0, tm, FFN_NORM_ROWS):
            rows = slice(r, r + FFN_NORM_ROWS)
            o_ref[rows, :] = x_ref[rows, :] + _rms(o_ref[rows, :], gpost_ref[...])


def _ffn(x, g_pre, w_up, conv_w, conv_b, w_down, g_post, *, layer, seq, tm, tf):
    t, d = x.shape
    d_ff = w_down.shape[1]
    nf = d_ff // tf
    halo = BF16_SUBLANES
    hb = tm // halo
    n_hblk = t // halo
    kern = functools.partial(_ffn_kernel, tiles_per_seq=seq // tm)
    return pl.pallas_call(
        kern,
        grid=(t // tm, nf),
        in_specs=[
            pl.BlockSpec((halo, d), lambda i, j: (jnp.maximum(i * hb - 1, 0), 0)),
            pl.BlockSpec((tm, d), lambda i, j: (i, 0)),
            pl.BlockSpec((halo, d), lambda i, j: (jnp.minimum((i + 1) * hb, n_hblk - 1), 0)),
            pl.BlockSpec((1, d), lambda i, j: (0, 0)),
            _layer_spec((d, tf), lambda i, j: (0, j), layer),
            _layer_spec((d, tf), lambda i, j: (0, nf + j), layer),
            pl.BlockSpec((3, tf), lambda i, j: (0, j)),
            pl.BlockSpec((3, tf), lambda i, j: (0, nf + j)),
            pl.BlockSpec((1, tf), lambda i, j: (0, j)),
            pl.BlockSpec((1, tf), lambda i, j: (0, nf + j)),
            _layer_spec((tf, d), lambda i, j: (j, 0), layer),
            pl.BlockSpec((1, d), lambda i, j: (0, 0)),
        ],
        out_specs=pl.BlockSpec((tm, d), lambda i, j: (i, 0)),
        out_shape=jax.ShapeDtypeStruct((t, d), F32),
        scratch_shapes=[
            pltpu.VMEM((tm + 2 * halo, d), BF16),
            pltpu.VMEM((tm + 2 * halo, tf), F32),
            pltpu.VMEM((tm + 2 * halo, tf), F32),
        ],
        compiler_params=_params("parallel", "arbitrary"),
        name="conv_ffn",
    )(x, x, x, g_pre, w_up, w_up, conv_w, conv_w, conv_b, conv_b, w_down, g_post)


def _head_lane_order():
    quarter = HEAD_DIM // 4
    blocks = (0, 2, 1, 3)
    return jnp.concatenate([jnp.arange(b * quarter, (b + 1) * quarter) for b in blocks])


def _rope_tables(seq):
    half = HEAD_DIM // 2
    pos = jnp.arange(seq)
    row = (pos // GRID_W).astype(F32)
    col = (pos % GRID_W).astype(F32)
    inv_freq = 1.0 / (ROPE_THETA ** (jnp.arange(0, half, 2, dtype=F32) / half))
    lane = _head_lane_order()
    freq = inv_freq[(lane % half) % (half // 2)]
    ids = jnp.where(lane[None, :] < half, row[:, None], col[:, None])
    ang = ids * freq[None, :]
    sign = jnp.where((lane % half) < half // 2, -1.0, 1.0)
    return jnp.cos(ang), jnp.sin(ang) * sign[None, :]


def _dft_tables(seq, gdim, fr):
    def cs(k, n, period):
        r = (k[..., None] * n) % period
        ang = r.astype(F32) * (2.0 * math.pi / period)
        return jnp.cos(ang), jnp.sin(ang)
    c = jnp.arange(gdim, dtype=jnp.int32)
    cc, sc = (t * gdim ** -0.5 for t in cs(c, c, gdim))
    rows = fr + FOUR_EXTRA_ROWS
    k = jnp.arange(seq // 2 // fr, dtype=jnp.int32)[:, None] * fr + jnp.arange(rows, dtype=jnp.int32)
    ca, sa = cs(k, GRID_W * jnp.arange(seq // 2 // GRID_W, dtype=jnp.int32), seq)
    cb, sb = cs(k, jnp.arange(GRID_W, dtype=jnp.int32), seq)
    scale = seq ** -0.5
    ch = ((ca[..., :, None] * cb[..., None, :] - sa[..., :, None] * sb[..., None, :]) * scale)
    sh = ((sa[..., :, None] * cb[..., None, :] + ca[..., :, None] * sb[..., None, :]) * scale)
    ch, sh = ch.reshape(*k.shape, seq // 2), sh.reshape(*k.shape, seq // 2)
    flip = (jnp.arange(rows)[None, :] == fr - jnp.arange(fr)[:, None]).astype(BF16)
    return (jnp.concatenate([cc, sc], axis=1).astype(BF16), ch.astype(BF16), sh.astype(BF16), flip)


def _tiles(seq, d_ff):
    def pick(n, pref):
        return pref if n % pref == 0 else n
    tf = next(c for c in (512, 256, 128) if d_ff % c == 0)
    return dict(
        qk_tm=pick(seq, 512), mixer_tm=pick(seq, 1024), uv_tn=768, gate_tn=1024,
        attn_tq=pick(seq, 512), attn_kc=pick(seq, 256),
        merge_tm=pick(seq, 256),
        xattn_tm=pick(seq, 512),
        ffn_tm=pick(seq, 512), ffn_tf=tf,
    )


def kernel(x, mem, mix_pre_g, w_in, q_norm_g, k_norm_g, w_attn_o, w_four_o, w_gate, b_gate, w_mix_o,
           mix_post_g, xa_pre_g, mem_norm_g, w_xq, w_xkv, w_xo, xa_post_g, ffn_pre_g, w_up, conv_w,
           conv_b, w_down, ffn_post_g):
    batch, seq, d = x.shape
    mem_tokens = mem.shape[1]
    depth = w_in.shape[0]
    q_width = w_attn_o.shape[1]
    f_width = w_four_o.shape[1]
    kv_width = (w_in.shape[2] - q_width - f_width) // 2
    n_heads = q_width // HEAD_DIM
    assert n_heads == N_KV * GROUP and kv_width == N_KV * HEAD_DIM
    assert seq % GRID_W == 0 and w_xq.shape[2] == X_HEADS * X_HEAD_DIM
    d_ff = w_down.shape[1]
    ts = _tiles(seq, d_ff)
    qk_width = q_width + kv_width
    u_col = q_width + 2 * kv_width
    tm = ts["mixer_tm"]
    assert (kv_width + f_width) % ts["uv_tn"] == 0 and w_gate.shape[2] % ts["gate_tn"] == 0

    cos_t, sin_t = _rope_tables(seq)
    assert seq % (2 * ts["merge_tm"]) == 0
    dft_c, dft_ch, dft_sh, dft_flip = _dft_tables(seq, f_width // N_FOURIER_GROUPS, ts["merge_tm"])
    order = _head_lane_order()
    partner = jnp.roll(jnp.arange(HEAD_DIM), HEAD_DIM // 2)
    bias_spec = pl.BlockSpec((1, ts["gate_tn"]), lambda i, j: (0, j))

    row = lambda v: v.reshape(1, -1)
    xf = x.reshape(batch * seq, d)
    memf = mem.reshape(batch * mem_tokens, d)

    quarter = HEAD_DIM // 4
    w_qk_t = w_in[:, :, :qk_width].astype(BF16).reshape(depth, d, qk_width // HEAD_DIM, 2, 2, quarter)
    w_qk_t = w_qk_t.transpose(0, 2, 4, 3, 5, 1).reshape(depth, qk_width, d)
    w_uv = jnp.concatenate([w_in[:, :, u_col:], w_in[:, :, qk_width:u_col]], axis=2).astype(BF16)
    w_gate_b, w_attn_o_b, w_four_o_b, w_mix_o_b, w_xq_b, w_xkv_b, w_xo_b, w_up_b, w_down_b = (
        w.astype(BF16) for w in (w_gate, w_attn_o, w_four_o, w_mix_o, w_xq, w_xkv, w_xo, w_up, w_down))

    for l in range(depth):
        q_gain = q_norm_g[l][order] * (HEAD_DIM ** -0.5 * math.log2(math.e))
        k_gain = k_norm_g[l][order]
        cg = jnp.concatenate([g[:, None] * cos_t.T for g in (q_gain, k_gain)], axis=0)
        sg = jnp.concatenate([g[partner][:, None] * sin_t.T for g in (q_gain, k_gain)], axis=0)
        g_pre = row(mix_pre_g[l])
        qk = _proj_qk_t(xf, g_pre, w_qk_t, cg, sg, layer=l, seq=seq, n_q_heads=n_heads,
                        tm=ts["qk_tm"])
        uv = _proj(xf, g_pre, w_uv, _epilogue_plain, (), (), layer=l, tm=tm, tn=ts["uv_tn"],
                   name="proj_uv")
        gates = _proj(xf, g_pre, w_gate_b, _epilogue_gate, (row(b_gate[l]),), (bias_spec,),
                      layer=l, tm=tm, tn=ts["gate_tn"], name="proj_gate")
        att = _attention(qk, uv, batch=batch, seq=seq, q_width=q_width, v_col=f_width,
                         tq=ts["attn_tq"], kc=ts["attn_kc"])
        ucs, mid = _four_chan(uv, dft_c, dft_flip, batch=batch, seq=seq, f_width=f_width)
        four = _four_seq(dft_ch, dft_sh, dft_flip, ucs, mid, batch=batch, seq=seq, f_width=f_width)
        xf = _merge(att, four.reshape(batch * seq, f_width), gates, xf, w_attn_o_b, w_four_o_b,
                    w_mix_o_b, row(mix_post_g[l]), layer=l, seq=seq, tm=ts["merge_tm"])

        kv = _norm_matmul(memf, row(mem_norm_g[l]), w_xkv_b, layer=l, tm=mem_tokens)
        xf = _xattn(xf, row(xa_pre_g[l]), w_xq_b, kv, w_xo_b, row(xa_post_g[l]), layer=l,
                    batch=batch, seq=seq, mem_tokens=mem_tokens, tm=ts["xattn_tm"])

        xf = _ffn(xf, row(ffn_pre_g[l]), w_up_b, conv_w[l], row(conv_b[l]), w_down_b,
                  row(ffn_post_g[l]), layer=l, seq=seq, tm=ts["ffn_tm"], tf=ts["ffn_tf"])

    return xf.reshape(batch, seq, d)
```

```python
import functools
import math

import jax
import jax.numpy as jnp
from jax import lax
from jax.experimental import pallas as pl
from jax.experimental.pallas import tpu as pltpu

HEAD_DIM = 128
N_KV = 4
GROUP = 4
GRID_W = 64
ROPE_THETA = 10000.0
N_FOURIER_GROUPS = 4
X_HEADS = 4
X_HEAD_DIM = 128
EPS = 1e-6

V7X_VMEM_LIMIT_BYTES = 56 * 1024 * 1024
BF16_SUBLANES = 16
ATTN_MIN_DENOMINATOR = 2.0 ** -60
ATTN_SHIFT_MARGIN = 1.0 + 2.0 ** -6
FOUR_EXTRA_ROWS = BF16_SUBLANES
FFN_NORM_ROWS = 128

F32 = jnp.float32
BF16 = jnp.bfloat16


def _params(*sem):
    return pltpu.CompilerParams(dimension_semantics=sem, vmem_limit_bytes=V7X_VMEM_LIMIT_BYTES)


def _rms(xf, g):
    ms = jnp.mean(xf * xf, axis=-1, keepdims=True)
    return xf * lax.rsqrt(ms + EPS) * g


def _dot(a, b):
    return jnp.dot(a, b, preferred_element_type=F32)


def _dot_nt(a, b):
    return lax.dot_general(a, b, (((1,), (1,)), ((), ())), preferred_element_type=F32)


def _epilogue_plain(z, o_ref):
    o_ref[...] = z.astype(BF16)


def _epilogue_gate(z, b_ref, o_ref):
    o_ref[...] = (0.5 * jnp.tanh(0.5 * (z + b_ref[...])) + 0.5).astype(BF16)


def _proj_qk_t_kernel(x_ref, g_ref, wt_ref, cg_ref, sg_ref, q_ref, kt_ref, *, n_q_heads):
    hn = _rms(x_ref[...], g_ref[...]).astype(BF16)
    zt = _dot_nt(wt_ref[...], hn)
    half = HEAD_DIM // 2
    lane0 = jnp.where(lax.broadcasted_iota(jnp.int32, (1, HEAD_DIM), 1) == 0, 1.0, 0.0)
    for h in range(zt.shape[0] // HEAD_DIM):
        tab = slice(0, HEAD_DIM) if h < n_q_heads else slice(HEAD_DIM, 2 * HEAD_DIM)
        y = zt[h * HEAD_DIM:(h + 1) * HEAD_DIM, :]
        y = y * lax.rsqrt(jnp.mean(y * y, axis=0, keepdims=True) + EPS)
        rot = jnp.concatenate([y[half:], y[:half]], axis=0)
        out = (y * cg_ref[tab, :] + rot * sg_ref[tab, :]).astype(BF16)
        if h < n_q_heads:
            q = out.T
            qf = q.astype(F32)
            norm = jnp.sqrt(jnp.sum(qf * qf, axis=-1, keepdims=True))
            q_ref[:, 2 * h * HEAD_DIM:(2 * h + 1) * HEAD_DIM] = q
            q_ref[:, (2 * h + 1) * HEAD_DIM:(2 * h + 2) * HEAD_DIM] = (-norm * lane0).astype(BF16)
        else:
            kt_ref[(h - n_q_heads) * HEAD_DIM:(h - n_q_heads + 1) * HEAD_DIM, :] = out


def _proj_qk_t(x, g_pre, w_t, cg, sg, *, layer, seq, n_q_heads, tm):
    t, d = x.shape
    n = w_t.shape[1]
    seq_tiles = seq // tm
    table_spec = pl.BlockSpec((2 * HEAD_DIM, tm), lambda i: (0, i % seq_tiles))
    return pl.pallas_call(
        functools.partial(_proj_qk_t_kernel, n_q_heads=n_q_heads),
        grid=(t // tm,),
        in_specs=[
            pl.BlockSpec((tm, d), lambda i: (i, 0)),
            _resident(g_pre.shape),
            _resident_layer(w_t, layer),
            table_spec,
            table_spec,
        ],
        out_specs=[
            pl.BlockSpec((tm, 2 * n_q_heads * HEAD_DIM), lambda i: (i, 0)),
            pl.BlockSpec((n - n_q_heads * HEAD_DIM, tm), lambda i: (0, i)),
        ],
        out_shape=[
            jax.ShapeDtypeStruct((t, 2 * n_q_heads * HEAD_DIM), BF16),
            jax.ShapeDtypeStruct((n - n_q_heads * HEAD_DIM, t), BF16),
        ],
        compiler_params=_params("parallel"),
        name="proj_qk",
    )(x, g_pre, w_t, cg, sg)


def _proj_kernel(x_ref, g_ref, w_ref, *rest, epilogue):
    hn_ref = rest[-1]

    @pl.when(pl.program_id(1) == 0)
    def _():
        hn_ref[...] = _rms(x_ref[...], g_ref[...]).astype(BF16)

    epilogue(_dot(hn_ref[...], w_ref[...]), *rest[:-1])


def _proj(x, g_pre, w, epilogue, extra, extra_specs, *, layer, tm, tn, name):
    t, d = x.shape
    n = w.shape[2]
    return pl.pallas_call(
        functools.partial(_proj_kernel, epilogue=epilogue),
        grid=(t // tm, n // tn),
        in_specs=[
            pl.BlockSpec((tm, d), lambda i, j: (i, 0)),
            pl.BlockSpec((1, d), lambda i, j: (0, 0)),
            _layer_spec((d, tn), lambda i, j: (0, j), layer),
            *extra_specs,
        ],
        out_specs=pl.BlockSpec((tm, tn), lambda i, j: (i, j)),
        out_shape=jax.ShapeDtypeStruct((t, n), BF16),
        scratch_shapes=[pltpu.VMEM((tm, d), BF16)],
        compiler_params=_params("parallel", "arbitrary"),
        name=name,
    )(x, g_pre, w, *extra)


def _attn_exact(q, ka_ref, va_ref, kc):
    rows = q.shape[0]
    m = jnp.full((rows, 1), -jnp.inf, F32)
    l = jnp.zeros((rows, 1), F32)
    acc = jnp.zeros((rows, HEAD_DIM), F32)
    for c in range(ka_ref.shape[0] // kc):
        ks = slice(c * kc, (c + 1) * kc)
        s = _dot_nt(q, ka_ref[ks, :HEAD_DIM])
        m_new = jnp.maximum(m, jnp.max(s, axis=-1, keepdims=True))
        alpha = jnp.exp2(m - m_new)
        p = jnp.exp2(s - m_new)
        l = alpha * l + jnp.sum(p, axis=-1, keepdims=True)
        acc = alpha * acc + _dot(p.astype(BF16), va_ref[ks, :HEAD_DIM])
        m = m_new
    return acc * (1.0 / l)


def _attn_kernel(q_ref, kt_ref, v_ref, o_ref, ka_ref, va_ref, *, kc):
    tq = q_ref.shape[0]
    s_len = v_ref.shape[0]
    n_chunks = s_len // kc
    lane = lax.broadcasted_iota(jnp.int32, (1, HEAD_DIM), 1)

    @pl.when(pl.program_id(2) == 0)
    def _():
        lane0 = jnp.where(lane == 0, 1.0, 0.0)
        k2max = jnp.zeros((1, 1), F32)
        for c in range(n_chunks):
            kf = kt_ref[:, c * kc:(c + 1) * kc].astype(F32)
            k2max = jnp.maximum(k2max, jnp.max(jnp.sum(kf * kf, axis=0, keepdims=True),
                                               axis=1, keepdims=True))
        kmax_col = jnp.broadcast_to(jnp.sqrt(k2max) * ATTN_SHIFT_MARGIN * lane0, (kc, HEAD_DIM))
        ones_col = jnp.broadcast_to(lane0, (kc, HEAD_DIM)).astype(BF16)
        for c in range(n_chunks):
            ks = slice(c * kc, (c + 1) * kc)
            ka_ref[ks, :HEAD_DIM] = kt_ref[:, ks].T
            ka_ref[ks, HEAD_DIM:] = kmax_col.astype(BF16)
            va_ref[ks, :HEAD_DIM] = v_ref[ks, :]
            va_ref[ks, HEAD_DIM:] = ones_col

    q_aug = jnp.concatenate(
        [q_ref[:, 2 * g * HEAD_DIM:2 * (g + 1) * HEAD_DIM] for g in range(GROUP)], axis=0)
    acc = jnp.zeros((GROUP * tq, 2 * HEAD_DIM), F32)
    for c in range(n_chunks):
        ks = slice(c * kc, (c + 1) * kc)
        p = jnp.exp2(_dot_nt(q_aug, ka_ref[ks, :])).astype(BF16)
        acc = acc + _dot(p, va_ref[ks, :])
    l = acc[:, HEAD_DIM:HEAD_DIM + 1]
    accurate = jnp.min(l) >= ATTN_MIN_DENOMINATOR

    def write(o):
        for g in range(GROUP):
            o_ref[:, g * HEAD_DIM:(g + 1) * HEAD_DIM] = o[g * tq:(g + 1) * tq].astype(BF16)

    write(acc[:, :HEAD_DIM] * (1.0 / l))

    @pl.when(jnp.logical_not(accurate))
    def _():
        write(_attn_exact(q_aug[:, :HEAD_DIM], ka_ref, va_ref, kc))


def _attention(q_aug, k_t, vu, *, batch, seq, q_width, v_col, tq, kc):
    t = vu.shape[0]
    gw = GROUP * HEAD_DIM
    v_blk0 = v_col // HEAD_DIM
    nq = seq // tq
    return pl.pallas_call(
        functools.partial(_attn_kernel, kc=kc),
        grid=(batch, N_KV, nq),
        in_specs=[
            pl.BlockSpec((tq, 2 * gw), lambda b, h, i: (b * nq + i, h)),
            pl.BlockSpec((HEAD_DIM, seq), lambda b, h, i: (h, b)),
            pl.BlockSpec((seq, HEAD_DIM), lambda b, h, i: (b, v_blk0 + h)),
        ],
        out_specs=pl.BlockSpec((tq, gw), lambda b, h, i: (b * nq + i, h)),
        out_shape=jax.ShapeDtypeStruct((t, q_width), BF16),
        scratch_shapes=[
            pltpu.VMEM((seq, 2 * HEAD_DIM), BF16),
            pltpu.VMEM((seq, 2 * HEAD_DIM), BF16),
        ],
        compiler_params=_params("parallel", "parallel", "arbitrary"),
        name="attention",
    )(q_aug, k_t, vu)


def _four_chan_kernel(u_ref, p_ref, e_ref, m_ref, dft_ref, flip_ref, o_ref, mid_ref, *, gdim):
    i = pl.program_id(1)
    ext = jnp.where(i == 0, 0.0, e_ref[...].astype(F32)).astype(BF16)
    partner = _dot(flip_ref[...], jnp.concatenate([p_ref[...], ext], axis=0))
    u = u_ref[...].astype(F32)
    even = (u + partner).astype(BF16)
    odd = (u - partner).astype(BF16)
    for g in range(u_ref.shape[1] // gdim):
        sl = slice(g * gdim, (g + 1) * gdim)
        o_ref[0, 0, :, sl] = _dot(even[:, sl], dft_ref[:, :gdim]).astype(BF16)
        o_ref[0, 1, :, sl] = _dot(odd[:, sl], dft_ref[:, gdim:]).astype(BF16)

    @pl.when(i == 0)
    def _():
        for g in range(u_ref.shape[1] // gdim):
            sl = slice(g * gdim, (g + 1) * gdim)
            mid_ref[0, :, sl] = _dot(m_ref[:, sl], dft_ref[:, :gdim])


def _four_chan(uv, dft_c, flip, *, batch, seq, f_width):
    gdim = f_width // N_FOURIER_GROUPS
    fr = flip.shape[0]
    nb = seq // fr
    ext = FOUR_EXTRA_ROWS
    last_ext = uv.shape[0] // ext - 1
    return pl.pallas_call(
        functools.partial(_four_chan_kernel, gdim=gdim),
        grid=(batch, nb // 2),
        in_specs=[
            pl.BlockSpec((fr, f_width), lambda b, i: (b * nb + i, 0)),
            pl.BlockSpec((fr, f_width), lambda b, i: (b * nb + nb - 1 - i, 0)),
            pl.BlockSpec((ext, f_width),
                         lambda b, i: (jnp.minimum(((b + 1) * nb - i) * (fr // ext), last_ext), 0)),
            pl.BlockSpec((ext, f_width), lambda b, i: ((2 * b + 1) * (seq // 2 // ext), 0)),
            _resident(dft_c.shape),
            _resident(flip.shape),
        ],
        out_specs=[
            pl.BlockSpec((1, 2, fr, f_width), lambda b, i: (b, 0, i, 0)),
            pl.BlockSpec((1, ext, f_width), lambda b, i: (b, 0, 0)),
        ],
        out_shape=[
            jax.ShapeDtypeStruct((batch, 2, seq // 2, f_width), BF16),
            jax.ShapeDtypeStruct((batch, ext, f_width), F32),
        ],
        compiler_params=_params("parallel", "arbitrary"),
        name="fourier_chan",
    )(uv, uv, uv, uv, dft_c, flip)


def _four_seq_kernel(ch_ref, sh_ref, uc_ref, us_ref, mid_ref, flip_ref, o_ref, *, seq):
    fr = o_ref.shape[3]
    rows = ch_ref.shape[1]
    p = _dot(ch_ref[0], uc_ref[0, 0])
    q = _dot(sh_ref[0], us_ref[0, 0])
    k = pl.program_id(1) * fr + lax.broadcasted_iota(jnp.int32, (rows, 1), 0)
    p = p + jnp.where(k % 2 == 0, seq ** -0.5, -(seq ** -0.5)) * mid_ref[0, 0:1, :]
    o_ref[0, 0, 0] = (p - q)[:fr].astype(BF16)
    o_ref[0, 1, 0] = _dot(flip_ref[...], (p + q).astype(BF16)).astype(BF16)


def _four_seq(ch, sh, flip, ucs, mid, *, batch, seq, f_width):
    nt, rows, half = ch.shape
    fr = flip.shape[0]
    table_spec = pl.BlockSpec((1, rows, half), lambda b, t: (t, 0, 0))
    return pl.pallas_call(
        functools.partial(_four_seq_kernel, seq=seq),
        grid=(batch, nt),
        in_specs=[
            table_spec,
            table_spec,
            pl.BlockSpec((1, 1, half, f_width), lambda b, t: (b, 0, 0, 0)),
            pl.BlockSpec((1, 1, half, f_width), lambda b, t: (b, 1, 0, 0)),
            pl.BlockSpec((1, FOUR_EXTRA_ROWS, f_width), lambda b, t: (b, 0, 0)),
            _resident(flip.shape),
        ],
        out_specs=pl.BlockSpec((1, 2, 1, fr, f_width), lambda b, t: (b, 0, t, 0, 0)),
        out_shape=jax.ShapeDtypeStruct((batch, 2, nt, fr, f_width), BF16),
        compiler_params=_params("parallel", "arbitrary"),
        name="fourier_seq",
    )(ch, sh, ucs, ucs, mid, flip)


def _merge_kernel(att_ref, y_ref, g1_ref, g2_ref, x_ref, wa_ref, wf_ref, wm_ref, gp_ref, o_ref):
    a = _dot(att_ref[...], wa_ref[...])
    f = _dot(y_ref[...], wf_ref[...])
    merged = g1_ref[...].astype(F32) * a + g2_ref[...].astype(F32) * f
    y = _dot(merged.astype(BF16), wm_ref[...])
    o_ref[...] = x_ref[...] + _rms(y, gp_ref[...])


def _resident(shape):
    return pl.BlockSpec(shape, lambda *_: (0,) * len(shape), pipeline_mode=pl.Buffered(1))


def _layer_spec(block, index_map, layer, **kwargs):
    return pl.BlockSpec((None, *block), lambda *idx: (layer, *index_map(*idx)), **kwargs)


def _resident_layer(w, layer):
    return _layer_spec(w.shape[1:], lambda *_: (0,) * (w.ndim - 1), layer,
                       pipeline_mode=pl.Buffered(1))


def _merge(att, four, gates, x, wa, wf, wm, g_post, *, layer, seq, tm):
    t, d = x.shape
    tiles = seq // tm

    def four_block(i):
        it = i % tiles
        return (i - it + jnp.where(it < tiles // 2, it, tiles + tiles // 2 - 1 - it), 0)

    return pl.pallas_call(
        _merge_kernel,
        grid=(t // tm,),
        in_specs=[
            pl.BlockSpec((tm, att.shape[1]), lambda i: (i, 0)),
            pl.BlockSpec((tm, four.shape[1]), four_block),
            pl.BlockSpec((tm, d), lambda i: (i, 0)),
            pl.BlockSpec((tm, d), lambda i: (i, 1)),
            pl.BlockSpec((tm, d), lambda i: (i, 0)),
            _resident_layer(wa, layer),
            _resident_layer(wf, layer),
            _resident_layer(wm, layer),
            _resident(g_post.shape),
        ],
        out_specs=pl.BlockSpec((tm, d), lambda i: (i, 0)),
        out_shape=jax.ShapeDtypeStruct((t, d), F32),
        compiler_params=_params("parallel"),
        name="merge",
    )(att, four, gates, gates, x, wa, wf, wm, g_post)


def _norm_matmul_kernel(x_ref, g_ref, w_ref, o_ref):
    hn = _rms(x_ref[...], g_ref[...]).astype(BF16)
    o_ref[...] = _dot(hn, w_ref[...]).astype(BF16)


def _norm_matmul(x, g, w, *, layer, tm):
    t, d = x.shape
    n = w.shape[2]
    return pl.pallas_call(
        _norm_matmul_kernel,
        grid=(t // tm,),
        in_specs=[
            pl.BlockSpec((tm, d), lambda i: (i, 0)),
            _resident(g.shape),
            _resident_layer(w, layer),
        ],
        out_specs=pl.BlockSpec((tm, n), lambda i: (i, 0)),
        out_shape=jax.ShapeDtypeStruct((t, n), BF16),
        compiler_params=_params("parallel"),
        name="mem_kv",
    )(x, g, w)


def _xattn_kernel(x_ref, gpre_ref, wq_ref, k_ref, v_ref, wo_ref, gpost_ref, o_ref):
    x = x_ref[...]
    hn = _rms(x, gpre_ref[...]).astype(BF16)
    q = (_dot(hn, wq_ref[...]) * (X_HEAD_DIM ** -0.5)).astype(BF16)
    outs = []
    for h in range(X_HEADS):
        sl = slice(h * X_HEAD_DIM, (h + 1) * X_HEAD_DIM)
        s = _dot_nt(q[:, sl], k_ref[:, sl])
        m = jnp.max(s, axis=-1, keepdims=True)
        p = jnp.exp(s - m)
        l = jnp.sum(p, axis=-1, keepdims=True)
        outs.append(_dot(p.astype(BF16), v_ref[:, sl]) * (1.0 / l))
    o = jnp.concatenate(outs, axis=1).astype(BF16)
    y = _dot(o, wo_ref[...])
    o_ref[...] = x + _rms(y, gpost_ref[...])


def _xattn(x, g_pre, wq, kv, wo, g_post, *, layer, batch, seq, mem_tokens, tm):
    t, d = x.shape
    xw = wq.shape[2]
    ns = seq // tm
    return pl.pallas_call(
        _xattn_kernel,
        grid=(batch, ns),
        in_specs=[
            pl.BlockSpec((tm, d), lambda b, i: (b * ns + i, 0)),
            _resident(g_pre.shape),
            _resident_layer(wq, layer),
            pl.BlockSpec((mem_tokens, xw), lambda b, i: (b, 0)),
            pl.BlockSpec((mem_tokens, xw), lambda b, i: (b, 1)),
            _resident_layer(wo, layer),
            _resident(g_post.shape),
        ],
        out_specs=pl.BlockSpec((tm, d), lambda b, i: (b * ns + i, 0)),
        out_shape=jax.ShapeDtypeStruct((t, d), F32),
        compiler_params=_params("parallel", "parallel"),
        name="xattn",
    )(x, g_pre, wq, kv, kv, wo, g_post)


def _gelu_tanh(x):
    return 0.5 * x * (1.0 + jnp.tanh(math.sqrt(2.0 / math.pi) * (x + 0.044715 * (x * x * x))))


def _ffn_kernel(xp_ref, x_ref, xn_ref, gpre_ref, wg_ref, wv_ref, cwg_ref, cwv_ref, cbg_ref, cbv_ref,
                wd_ref, gpost_ref, o_ref, hn_ref, ug_ref, uv_ref, *, tiles_per_seq):
    i = pl.program_id(0)
    j = pl.program_id(1)
    tm = x_ref.shape[0]
    halo = xp_ref.shape[0]

    @pl.when(j == 0)
    def _():
        g = gpre_ref[...]
        pos = i % tiles_per_seq
        hp = jnp.where(pos == 0, 0.0, _rms(xp_ref[...], g))
        hx = jnp.where(pos == tiles_per_seq - 1, 0.0, _rms(xn_ref[...], g))
        hn_ref[0:halo, :] = hp.astype(BF16)
        hn_ref[halo + tm:, :] = hx.astype(BF16)
        for r in range(0, tm, FFN_NORM_ROWS):
            rows = slice(r, r + FFN_NORM_ROWS)
            hn_ref[halo + r:halo + r + FFN_NORM_ROWS, :] = _rms(x_ref[rows, :], g).astype(BF16)
        o_ref[...] = jnp.zeros_like(o_ref)

    hn = hn_ref[...]
    ug_ref[...] = _dot(hn, wg_ref[...])
    uv_ref[...] = _dot(hn, wv_ref[...])

    def conv(u_ref, cw_ref, cb_ref):
        return (u_ref[halo - 1:halo - 1 + tm, :] * cw_ref[0:1, :]
                + u_ref[halo:halo + tm, :] * cw_ref[1:2, :]
                + u_ref[halo + 1:halo + 1 + tm, :] * cw_ref[2:3, :]
                + cb_ref[...])

    act = _gelu_tanh(conv(ug_ref, cwg_ref, cbg_ref)) * conv(uv_ref, cwv_ref, cbv_ref)
    o_ref[...] += _dot(act.astype(BF16), wd_ref[...])

    @pl.when(j == pl.num_programs(1) - 1)
    def _():
        for r in range(0, tm, FFN_NORM_ROWS):
            rows = slice(r, r + FFN_NORM_ROWS)
            o_ref[rows, :] = x_ref[rows, :] + _rms(o_ref[rows, :], gpost_ref[...])


def _ffn(x, g_pre, w_up, conv_w, conv_b, w_down, g_post, *, layer, seq, tm, tf):
    t, d = x.shape
    d_ff = w_down.shape[1]
    nf = d_ff // tf
    halo = BF16_SUBLANES
    hb = tm // halo
    n_hblk = t // halo
    kern = functools.partial(_ffn_kernel, tiles_per_seq=seq // tm)
    return pl.pallas_call(
        kern,
        grid=(t // tm, nf),
        in_specs=[
            pl.BlockSpec((halo, d), lambda i, j: (jnp.maximum(i * hb - 1, 0), 0)),
            pl.BlockSpec((tm, d), lambda i, j: (i, 0)),
            pl.BlockSpec((halo, d), lambda i, j: (jnp.minimum((i + 1) * hb, n_hblk - 1), 0)),
            pl.BlockSpec((1, d), lambda i, j: (0, 0)),
            _layer_spec((d, tf), lambda i, j: (0, j), layer),
            _layer_spec((d, tf), lambda i, j: (0, nf + j), layer),
            pl.BlockSpec((3, tf), lambda i, j: (0, j)),
            pl.BlockSpec((3, tf), lambda i, j: (0, nf + j)),
            pl.BlockSpec((1, tf), lambda i, j: (0, j)),
            pl.BlockSpec((1, tf), lambda i, j: (0, nf + j)),
            _layer_spec((tf, d), lambda i, j: (j, 0), layer),
            pl.BlockSpec((1, d), lambda i, j: (0, 0)),
        ],
        out_specs=pl.BlockSpec((tm, d), lambda i, j: (i, 0)),
        out_shape=jax.ShapeDtypeStruct((t, d), F32),
        scratch_shapes=[
            pltpu.VMEM((tm + 2 * halo, d), BF16),
            pltpu.VMEM((tm + 2 * halo, tf), F32),
            pltpu.VMEM((tm + 2 * halo, tf), F32),
        ],
        compiler_params=_params("parallel", "arbitrary"),
        name="conv_ffn",
    )(x, x, x, g_pre, w_up, w_up, conv_w, conv_w, conv_b, conv_b, w_down, g_post)


def _head_lane_order():
    quarter = HEAD_DIM // 4
    blocks = (0, 2, 1, 3)
    return jnp.concatenate([jnp.arange(b * quarter, (b + 1) * quarter) for b in blocks])


def _rope_tables(seq):
    half = HEAD_DIM // 2
    pos = jnp.arange(seq)
    row = (pos // GRID_W).astype(F32)
    col = (pos % GRID_W).astype(F32)
    inv_freq = 1.0 / (ROPE_THETA ** (jnp.arange(0, half, 2, dtype=F32) / half))
    lane = _head_lane_order()
    freq = inv_freq[(lane % half) % (half // 2)]
    ids = jnp.where(lane[None, :] < half, row[:, None], col[:, None])
    ang = ids * freq[None, :]
    sign = jnp.where((lane % half) < half // 2, -1.0, 1.0)
    return jnp.cos(ang), jnp.sin(ang) * sign[None, :]


def _dft_tables(seq, gdim, fr):
    def cs(k, n, period):
        r = (k[..., None] * n) % period
        ang = r.astype(F32) * (2.0 * math.pi / period)
        return jnp.cos(ang), jnp.sin(ang)
    c = jnp.arange(gdim, dtype=jnp.int32)
    cc, sc = (t * gdim ** -0.5 for t in cs(c, c, gdim))
    rows = fr + FOUR_EXTRA_ROWS
    k = jnp.arange(seq // 2 // fr, dtype=jnp.int32)[:, None] * fr + jnp.arange(rows, dtype=jnp.int32)
    ca, sa = cs(k, GRID_W * jnp.arange(seq // 2 // GRID_W, dtype=jnp.int32), seq)
    cb, sb = cs(k, jnp.arange(GRID_W, dtype=jnp.int32), seq)
    scale = seq ** -0.5
    ch = ((ca[..., :, None] * cb[..., None, :] - sa[..., :, None] * sb[..., None, :]) * scale)
    sh = ((sa[..., :, None] * cb[..., None, :] + ca[..., :, None] * sb[..., None, :]) * scale)
    ch, sh = ch.reshape(*k.shape, seq // 2), sh.reshape(*k.shape, seq // 2)
    flip = (jnp.arange(rows)[None, :] == fr - jnp.arange(fr)[:, None]).astype(BF16)
    return (jnp.concatenate([cc, sc], axis=1).astype(BF16), ch.astype(BF16), sh.astype(BF16), flip)


def _tiles(seq, d_ff):
    def pick(n, pref):
        return pref if n % pref == 0 else n
    tf = next(c for c in (512, 256, 128) if d_ff % c == 0)
    return dict(
        qk_tm=pick(seq, 512), mixer_tm=pick(seq, 1024), uv_tn=768, gate_tn=1024,
        attn_tq=pick(seq, 512), attn_kc=pick(seq, 256),
        merge_tm=pick(seq, 256),
        xattn_tm=pick(seq, 512),
        ffn_tm=pick(seq, 512), ffn_tf=tf,
    )


def kernel(x, mem, mix_pre_g, w_in, q_norm_g, k_norm_g, w_attn_o, w_four_o, w_gate, b_gate, w_mix_o,
           mix_post_g, xa_pre_g, mem_norm_g, w_xq, w_xkv, w_xo, xa_post_g, ffn_pre_g, w_up, conv_w,
           conv_b, w_down, ffn_post_g):
    batch, seq, d = x.shape
    mem_tokens = mem.shape[1]
    depth = w_in.shape[0]
    q_width = w_attn_o.shape[1]
    f_width = w_four_o.shape[1]
    kv_width = (w_in.shape[2] - q_width - f_width) // 2
    n_heads = q_width // HEAD_DIM
    assert n_heads == N_KV * GROUP and kv_width == N_KV * HEAD_DIM
    assert seq % GRID_W == 0 and w_xq.shape[2] == X_HEADS * X_HEAD_DIM
    d_ff = w_down.shape[1]
    ts = _tiles(seq, d_ff)
    qk_width = q_width + kv_width
    u_col = q_width + 2 * kv_width
    tm = ts["mixer_tm"]
    assert (kv_width + f_width) % ts["uv_tn"] == 0 and w_gate.shape[2] % ts["gate_tn"] == 0

    cos_t, sin_t = _rope_tables(seq)
    assert seq % (2 * ts["merge_tm"]) == 0
    dft_c, dft_ch, dft_sh, dft_flip = _dft_tables(seq, f_width // N_FOURIER_GROUPS, ts["merge_tm"])
    order = _head_lane_order()
    partner = jnp.roll(jnp.arange(HEAD_DIM), HEAD_DIM // 2)
    bias_spec = pl.BlockSpec((1, ts["gate_tn"]), lambda i, j: (0, j))

    row = lambda v: v.reshape(1, -1)
    xf = x.reshape(batch * seq, d)
    memf = mem.reshape(batch * mem_tokens, d)

    quarter = HEAD_DIM // 4
    w_qk_t = w_in[:, :, :qk_width].astype(BF16).reshape(depth, d, qk_width // HEAD_DIM, 2, 2, quarter)
    w_qk_t = w_qk_t.transpose(0, 2, 4, 3, 5, 1).reshape(depth, qk_width, d)
    w_uv = jnp.concatenate([w_in[:, :, u_col:], w_in[:, :, qk_width:u_col]], axis=2).astype(BF16)
    w_gate_b, w_attn_o_b, w_four_o_b, w_mix_o_b, w_xq_b, w_xkv_b, w_xo_b, w_up_b, w_down_b = (
        w.astype(BF16) for w in (w_gate, w_attn_o, w_four_o, w_mix_o, w_xq, w_xkv, w_xo, w_up, w_down))

    for l in range(depth):
        q_gain = q_norm_g[l][order] * (HEAD_DIM ** -0.5 * math.log2(math.e))
        k_gain = k_norm_g[l][order]
        cg = jnp.concatenate([g[:, None] * cos_t.T for g in (q_gain, k_gain)], axis=0)
        sg = jnp.concatenate([g[partner][:, None] * sin_t.T for g in (q_gain, k_gain)], axis=0)
        g_pre = row(mix_pre_g[l])
        q_aug, k_t = _proj_qk_t(xf, g_pre, w_qk_t, cg, sg, layer=l, seq=seq, n_q_heads=n_heads,
                                tm=ts["qk_tm"])
        uv = _proj(xf, g_pre, w_uv, _epilogue_plain, (), (), layer=l, tm=tm, tn=ts["uv_tn"],
                   name="proj_uv")
        gates = _proj(xf, g_pre, w_gate_b, _epilogue_gate, (row(b_gate[l]),), (bias_spec,),
                      layer=l, tm=tm, tn=ts["gate_tn"], name="proj_gate")
        att = _attention(q_aug, k_t, uv, batch=batch, seq=seq, q_width=q_width, v_col=f_width,
                         tq=ts["attn_tq"], kc=ts["attn_kc"])
        ucs, mid = _four_chan(uv, dft_c, dft_flip, batch=batch, seq=seq, f_width=f_width)
        four = _four_seq(dft_ch, dft_sh, dft_flip, ucs, mid, batch=batch, seq=seq, f_width=f_width)
        xf = _merge(att, four.reshape(batch * seq, f_width), gates, xf, w_attn_o_b, w_four_o_b,
                    w_mix_o_b, row(mix_post_g[l]), layer=l, seq=seq, tm=ts["merge_tm"])

        kv = _norm_matmul(memf, row(mem_norm_g[l]), w_xkv_b, layer=l, tm=mem_tokens)
        xf = _xattn(xf, row(xa_pre_g[l]), w_xq_b, kv, w_xo_b, row(xa_post_g[l]), layer=l,
                    batch=batch, seq=seq, mem_tokens=mem_tokens, tm=ts["xattn_tm"])

        xf = _ffn(xf, row(ffn_pre_g[l]), w_up_b, conv_w[l], row(conv_b[l]), w_down_b,
                  row(ffn_post_g[l]), layer=l, seq=seq, tm=ts["ffn_tm"], tf=ts["ffn_tf"])

    return xf.reshape(batch, seq, d)
```

```python
import functools
import math

import jax
import jax.numpy as jnp
from jax import lax
from jax.experimental import pallas as pl
from jax.experimental.pallas import tpu as pltpu

HEAD_DIM = 128
N_KV = 4
GROUP = 4
GRID_W = 64
ROPE_THETA = 10000.0
N_FOURIER_GROUPS = 4
X_HEADS = 4
X_HEAD_DIM = 128
EPS = 1e-6

V7X_VMEM_LIMIT_BYTES = 56 * 1024 * 1024
BF16_SUBLANES = 16
ATTN_MIN_DENOMINATOR = 2.0 ** -60
ATTN_SHIFT_MARGIN = 1.0 + 2.0 ** -6
FOUR_EXTRA_ROWS = BF16_SUBLANES
FFN_NORM_ROWS = 128

F32 = jnp.float32
BF16 = jnp.bfloat16


def _params(*sem):
    return pltpu.CompilerParams(dimension_semantics=sem, vmem_limit_bytes=V7X_VMEM_LIMIT_BYTES)


def _rms(xf, g):
    ms = jnp.mean(xf * xf, axis=-1, keepdims=True)
    return xf * lax.rsqrt(ms + EPS) * g


def _dot(a, b):
    return jnp.dot(a, b, preferred_element_type=F32)


def _dot_nt(a, b):
    return lax.dot_general(a, b, (((1,), (1,)), ((), ())), preferred_element_type=F32)


def _epilogue_plain(z, o_ref):
    o_ref[...] = z.astype(BF16)


def _epilogue_gate(z, b_ref, o_ref):
    o_ref[...] = (0.5 * jnp.tanh(0.5 * (z + b_ref[...])) + 0.5).astype(BF16)


def _proj_qk_t_kernel(x_ref, g_ref, wt_ref, cg_ref, sg_ref, q_ref, kt_ref, *, n_q_heads):
    hn = _rms(x_ref[...], g_ref[...]).astype(BF16)
    zt = _dot_nt(wt_ref[...], hn)
    half = HEAD_DIM // 2
    for h in range(zt.shape[0] // HEAD_DIM):
        tab = slice(0, HEAD_DIM) if h < n_q_heads else slice(HEAD_DIM, 2 * HEAD_DIM)
        y = zt[h * HEAD_DIM:(h + 1) * HEAD_DIM, :]
        y = y * lax.rsqrt(jnp.mean(y * y, axis=0, keepdims=True) + EPS)
        rot = jnp.concatenate([y[half:], y[:half]], axis=0)
        out = (y * cg_ref[tab, :] + rot * sg_ref[tab, :]).astype(BF16)
        if h < n_q_heads:
            q_ref[:, h * HEAD_DIM:(h + 1) * HEAD_DIM] = out.T
        else:
            kt_ref[(h - n_q_heads) * HEAD_DIM:(h - n_q_heads + 1) * HEAD_DIM, :] = out


def _proj_qk_t(x, g_pre, w_t, cg, sg, *, layer, seq, n_q_heads, tm):
    t, d = x.shape
    n = w_t.shape[1]
    seq_tiles = seq // tm
    table_spec = pl.BlockSpec((2 * HEAD_DIM, tm), lambda i: (0, i % seq_tiles))
    return pl.pallas_call(
        functools.partial(_proj_qk_t_kernel, n_q_heads=n_q_heads),
        grid=(t // tm,),
        in_specs=[
            pl.BlockSpec((tm, d), lambda i: (i, 0)),
            _resident(g_pre.shape),
            _resident_layer(w_t, layer),
            table_spec,
            table_spec,
        ],
        out_specs=[
            pl.BlockSpec((tm, n_q_heads * HEAD_DIM), lambda i: (i, 0)),
            pl.BlockSpec((n - n_q_heads * HEAD_DIM, tm), lambda i: (0, i)),
        ],
        out_shape=[
            jax.ShapeDtypeStruct((t, n_q_heads * HEAD_DIM), BF16),
            jax.ShapeDtypeStruct((n - n_q_heads * HEAD_DIM, t), BF16),
        ],
        compiler_params=_params("parallel"),
        name="proj_qk",
    )(x, g_pre, w_t, cg, sg)


def _proj_kernel(x_ref, g_ref, w_ref, *rest, epilogue):
    hn_ref = rest[-1]

    @pl.when(pl.program_id(1) == 0)
    def _():
        hn_ref[...] = _rms(x_ref[...], g_ref[...]).astype(BF16)

    epilogue(_dot(hn_ref[...], w_ref[...]), *rest[:-1])


def _proj(x, g_pre, w, epilogue, extra, extra_specs, *, layer, tm, tn, name):
    t, d = x.shape
    n = w.shape[2]
    return pl.pallas_call(
        functools.partial(_proj_kernel, epilogue=epilogue),
        grid=(t // tm, n // tn),
        in_specs=[
            pl.BlockSpec((tm, d), lambda i, j: (i, 0)),
            pl.BlockSpec((1, d), lambda i, j: (0, 0)),
            _layer_spec((d, tn), lambda i, j: (0, j), layer),
            *extra_specs,
        ],
        out_specs=pl.BlockSpec((tm, tn), lambda i, j: (i, j)),
        out_shape=jax.ShapeDtypeStruct((t, n), BF16),
        scratch_shapes=[pltpu.VMEM((tm, d), BF16)],
        compiler_params=_params("parallel", "arbitrary"),
        name=name,
    )(x, g_pre, w, *extra)


def _attn_exact(q, ka_ref, va_ref, kc):
    rows = q.shape[0]
    m = jnp.full((rows, 1), -jnp.inf, F32)
    l = jnp.zeros((rows, 1), F32)
    acc = jnp.zeros((rows, HEAD_DIM), F32)
    for c in range(ka_ref.shape[0] // kc):
        ks = slice(c * kc, (c + 1) * kc)
        s = _dot_nt(q, ka_ref[ks, :HEAD_DIM])
        m_new = jnp.maximum(m, jnp.max(s, axis=-1, keepdims=True))
        alpha = jnp.exp2(m - m_new)
        p = jnp.exp2(s - m_new)
        l = alpha * l + jnp.sum(p, axis=-1, keepdims=True)
        acc = alpha * acc + _dot(p.astype(BF16), va_ref[ks, :HEAD_DIM])
        m = m_new
    return acc * (1.0 / l)


def _attn_kernel(q_ref, kt_ref, v_ref, qb_ref, o_ref, ka_ref, va_ref, *, kc):
    tq = q_ref.shape[0]
    s_len = v_ref.shape[0]
    n_chunks = s_len // kc
    lane = lax.broadcasted_iota(jnp.int32, (1, HEAD_DIM), 1)

    @pl.when(pl.program_id(2) == 0)
    def _():
        lane0 = jnp.where(lane == 0, 1.0, 0.0)
        k2max = jnp.zeros((1, 1), F32)
        for c in range(n_chunks):
            kf = kt_ref[:, c * kc:(c + 1) * kc].astype(F32)
            k2max = jnp.maximum(k2max, jnp.max(jnp.sum(kf * kf, axis=0, keepdims=True),
                                               axis=1, keepdims=True))
        kmax_col = jnp.broadcast_to(jnp.sqrt(k2max) * ATTN_SHIFT_MARGIN * lane0, (kc, HEAD_DIM))
        ones_col = jnp.broadcast_to(lane0, (kc, HEAD_DIM)).astype(BF16)
        for c in range(n_chunks):
            ks = slice(c * kc, (c + 1) * kc)
            ka_ref[ks, :HEAD_DIM] = kt_ref[:, ks].T
            ka_ref[ks, HEAD_DIM:] = kmax_col.astype(BF16)
            va_ref[ks, :HEAD_DIM] = v_ref[ks, :]
            va_ref[ks, HEAD_DIM:] = ones_col

    bound_col = jnp.broadcast_to(qb_ref[...], (tq, HEAD_DIM)).astype(BF16)
    q_aug = jnp.concatenate(
        [jnp.concatenate([q_ref[:, g * HEAD_DIM:(g + 1) * HEAD_DIM], bound_col], axis=1)
         for g in range(GROUP)], axis=0)
    acc = jnp.zeros((GROUP * tq, 2 * HEAD_DIM), F32)
    for c in range(n_chunks):
        ks = slice(c * kc, (c + 1) * kc)
        p = jnp.exp2(_dot_nt(q_aug, ka_ref[ks, :])).astype(BF16)
        acc = acc + _dot(p, va_ref[ks, :])
    l = acc[:, HEAD_DIM:HEAD_DIM + 1]
    accurate = jnp.min(l) >= ATTN_MIN_DENOMINATOR

    def write(o):
        for g in range(GROUP):
            o_ref[:, g * HEAD_DIM:(g + 1) * HEAD_DIM] = o[g * tq:(g + 1) * tq].astype(BF16)

    write(acc[:, :HEAD_DIM] * (1.0 / l))

    @pl.when(jnp.logical_not(accurate))
    def _():
        write(_attn_exact(q_aug[:, :HEAD_DIM], ka_ref, va_ref, kc))


def _attention(q, k_t, vu, q_bound, *, batch, seq, q_width, v_col, tq, kc):
    t = vu.shape[0]
    gw = GROUP * HEAD_DIM
    v_blk0 = v_col // HEAD_DIM
    nq = seq // tq
    return pl.pallas_call(
        functools.partial(_attn_kernel, kc=kc),
        grid=(batch, N_KV, nq),
        in_specs=[
            pl.BlockSpec((tq, gw), lambda b, h, i: (b * nq + i, h)),
            pl.BlockSpec((HEAD_DIM, seq), lambda b, h, i: (h, b)),
            pl.BlockSpec((seq, HEAD_DIM), lambda b, h, i: (b, v_blk0 + h)),
            _resident(q_bound.shape),
        ],
        out_specs=pl.BlockSpec((tq, gw), lambda b, h, i: (b * nq + i, h)),
        out_shape=jax.ShapeDtypeStruct((t, q_width), BF16),
        scratch_shapes=[
            pltpu.VMEM((seq, 2 * HEAD_DIM), BF16),
            pltpu.VMEM((seq, 2 * HEAD_DIM), BF16),
        ],
        compiler_params=_params("parallel", "parallel", "arbitrary"),
        name="attention",
    )(q, k_t, vu, q_bound)


def _four_chan_kernel(u_ref, p_ref, e_ref, m_ref, dft_ref, flip_ref, o_ref, mid_ref, *, gdim):
    i = pl.program_id(1)
    ext = jnp.where(i == 0, 0.0, e_ref[...].astype(F32)).astype(BF16)
    partner = _dot(flip_ref[...], jnp.concatenate([p_ref[...], ext], axis=0))
    u = u_ref[...].astype(F32)
    even = (u + partner).astype(BF16)
    odd = (u - partner).astype(BF16)
    for g in range(u_ref.shape[1] // gdim):
        sl = slice(g * gdim, (g + 1) * gdim)
        o_ref[0, 0, :, sl] = _dot(even[:, sl], dft_ref[:, :gdim]).astype(BF16)
        o_ref[0, 1, :, sl] = _dot(odd[:, sl], dft_ref[:, gdim:]).astype(BF16)

    @pl.when(i == 0)
    def _():
        for g in range(u_ref.shape[1] // gdim):
            sl = slice(g * gdim, (g + 1) * gdim)
            mid_ref[0, :, sl] = _dot(m_ref[:, sl], dft_ref[:, :gdim])


def _four_chan(uv, dft_c, flip, *, batch, seq, f_width):
    gdim = f_width // N_FOURIER_GROUPS
    fr = flip.shape[0]
    nb = seq // fr
    ext = FOUR_EXTRA_ROWS
    last_ext = uv.shape[0] // ext - 1
    return pl.pallas_call(
        functools.partial(_four_chan_kernel, gdim=gdim),
        grid=(batch, nb // 2),
        in_specs=[
            pl.BlockSpec((fr, f_width), lambda b, i: (b * nb + i, 0)),
            pl.BlockSpec((fr, f_width), lambda b, i: (b * nb + nb - 1 - i, 0)),
            pl.BlockSpec((ext, f_width),
                         lambda b, i: (jnp.minimum(((b + 1) * nb - i) * (fr // ext), last_ext), 0)),
            pl.BlockSpec((ext, f_width), lambda b, i: ((2 * b + 1) * (seq // 2 // ext), 0)),
            _resident(dft_c.shape),
            _resident(flip.shape),
        ],
        out_specs=[
            pl.BlockSpec((1, 2, fr, f_width), lambda b, i: (b, 0, i, 0)),
            pl.BlockSpec((1, ext, f_width), lambda b, i: (b, 0, 0)),
        ],
        out_shape=[
            jax.ShapeDtypeStruct((batch, 2, seq // 2, f_width), BF16),
            jax.ShapeDtypeStruct((batch, ext, f_width), F32),
        ],
        compiler_params=_params("parallel", "arbitrary"),
        name="fourier_chan",
    )(uv, uv, uv, uv, dft_c, flip)


def _four_seq_kernel(ch_ref, sh_ref, uc_ref, us_ref, mid_ref, flip_ref, o_ref, *, seq):
    fr = o_ref.shape[3]
    rows = ch_ref.shape[1]
    p = _dot(ch_ref[0], uc_ref[0, 0])
    q = _dot(sh_ref[0], us_ref[0, 0])
    k = pl.program_id(1) * fr + lax.broadcasted_iota(jnp.int32, (rows, 1), 0)
    p = p + jnp.where(k % 2 == 0, seq ** -0.5, -(seq ** -0.5)) * mid_ref[0, 0:1, :]
    o_ref[0, 0, 0] = (p - q)[:fr].astype(BF16)
    o_ref[0, 1, 0] = _dot(flip_ref[...], (p + q).astype(BF16)).astype(BF16)


def _four_seq(ch, sh, flip, ucs, mid, *, batch, seq, f_width):
    nt, rows, half = ch.shape
    fr = flip.shape[0]
    table_spec = pl.BlockSpec((1, rows, half), lambda b, t: (t, 0, 0))
    return pl.pallas_call(
        functools.partial(_four_seq_kernel, seq=seq),
        grid=(batch, nt),
        in_specs=[
            table_spec,
            table_spec,
            pl.BlockSpec((1, 1, half, f_width), lambda b, t: (b, 0, 0, 0)),
            pl.BlockSpec((1, 1, half, f_width), lambda b, t: (b, 1, 0, 0)),
            pl.BlockSpec((1, FOUR_EXTRA_ROWS, f_width), lambda b, t: (b, 0, 0)),
            _resident(flip.shape),
        ],
        out_specs=pl.BlockSpec((1, 2, 1, fr, f_width), lambda b, t: (b, 0, t, 0, 0)),
        out_shape=jax.ShapeDtypeStruct((batch, 2, nt, fr, f_width), BF16),
        compiler_params=_params("parallel", "arbitrary"),
        name="fourier_seq",
    )(ch, sh, ucs, ucs, mid, flip)


def _merge_kernel(att_ref, y_ref, g1_ref, g2_ref, x_ref, wa_ref, wf_ref, wm_ref, gp_ref, o_ref):
    a = _dot(att_ref[...], wa_ref[...])
    f = _dot(y_ref[...], wf_ref[...])
    merged = g1_ref[...].astype(F32) * a + g2_ref[...].astype(F32) * f
    y = _dot(merged.astype(BF16), wm_ref[...])
    o_ref[...] = x_ref[...] + _rms(y, gp_ref[...])


def _resident(shape):
    return pl.BlockSpec(shape, lambda *_: (0,) * len(shape), pipeline_mode=pl.Buffered(1))


def _layer_spec(block, index_map, layer, **kwargs):
    return pl.BlockSpec((None, *block), lambda *idx: (layer, *index_map(*idx)), **kwargs)


def _resident_layer(w, layer):
    return _layer_spec(w.shape[1:], lambda *_: (0,) * (w.ndim - 1), layer,
                       pipeline_mode=pl.Buffered(1))


def _merge(att, four, gates, x, wa, wf, wm, g_post, *, layer, seq, tm):
    t, d = x.shape
    tiles = seq // tm

    def four_block(i):
        it = i % tiles
        return (i - it + jnp.where(it < tiles // 2, it, tiles + tiles // 2 - 1 - it), 0)

    return pl.pallas_call(
        _merge_kernel,
        grid=(t // tm,),
        in_specs=[
            pl.BlockSpec((tm, att.shape[1]), lambda i: (i, 0)),
            pl.BlockSpec((tm, four.shape[1]), four_block),
            pl.BlockSpec((tm, d), lambda i: (i, 0)),
            pl.BlockSpec((tm, d), lambda i: (i, 1)),
            pl.BlockSpec((tm, d), lambda i: (i, 0)),
            _resident_layer(wa, layer),
            _resident_layer(wf, layer),
            _resident_layer(wm, layer),
            _resident(g_post.shape),
        ],
        out_specs=pl.BlockSpec((tm, d), lambda i: (i, 0)),
        out_shape=jax.ShapeDtypeStruct((t, d), F32),
        compiler_params=_params("parallel"),
        name="merge",
    )(att, four, gates, gates, x, wa, wf, wm, g_post)


def _norm_matmul_kernel(x_ref, g_ref, w_ref, o_ref):
    hn = _rms(x_ref[...], g_ref[...]).astype(BF16)
    o_ref[...] = _dot(hn, w_ref[...]).astype(BF16)


def _norm_matmul(x, g, w, *, layer, tm):
    t, d = x.shape
    n = w.shape[2]
    return pl.pallas_call(
        _norm_matmul_kernel,
        grid=(t // tm,),
        in_specs=[
            pl.BlockSpec((tm, d), lambda i: (i, 0)),
            _resident(g.shape),
            _resident_layer(w, layer),
        ],
        out_specs=pl.BlockSpec((tm, n), lambda i: (i, 0)),
        out_shape=jax.ShapeDtypeStruct((t, n), BF16),
        compiler_params=_params("parallel"),
        name="mem_kv",
    )(x, g, w)


def _xattn_kernel(x_ref, gpre_ref, wq_ref, k_ref, v_ref, wo_ref, gpost_ref, o_ref):
    x = x_ref[...]
    hn = _rms(x, gpre_ref[...]).astype(BF16)
    q = (_dot(hn, wq_ref[...]) * (X_HEAD_DIM ** -0.5)).astype(BF16)
    outs = []
    for h in range(X_HEADS):
        sl = slice(h * X_HEAD_DIM, (h + 1) * X_HEAD_DIM)
        s = _dot_nt(q[:, sl], k_ref[:, sl])
        m = jnp.max(s, axis=-1, keepdims=True)
        p = jnp.exp(s - m)
        l = jnp.sum(p, axis=-1, keepdims=True)
        outs.append(_dot(p.astype(BF16), v_ref[:, sl]) * (1.0 / l))
    o = jnp.concatenate(outs, axis=1).astype(BF16)
    y = _dot(o, wo_ref[...])
    o_ref[...] = x + _rms(y, gpost_ref[...])


def _xattn(x, g_pre, wq, kv, wo, g_post, *, layer, batch, seq, mem_tokens, tm):
    t, d = x.shape
    xw = wq.shape[2]
    ns = seq // tm
    return pl.pallas_call(
        _xattn_kernel,
        grid=(batch, ns),
        in_specs=[
            pl.BlockSpec((tm, d), lambda b, i: (b * ns + i, 0)),
            _resident(g_pre.shape),
            _resident_layer(wq, layer),
            pl.BlockSpec((mem_tokens, xw), lambda b, i: (b, 0)),
            pl.BlockSpec((mem_tokens, xw), lambda b, i: (b, 1)),
            _resident_layer(wo, layer),
            _resident(g_post.shape),
        ],
        out_specs=pl.BlockSpec((tm, d), lambda b, i: (b * ns + i, 0)),
        out_shape=jax.ShapeDtypeStruct((t, d), F32),
        compiler_params=_params("parallel", "parallel"),
        name="xattn",
    )(x, g_pre, wq, kv, kv, wo, g_post)


def _gelu_tanh(x):
    return 0.5 * x * (1.0 + jnp.tanh(math.sqrt(2.0 / math.pi) * (x + 0.044715 * (x * x * x))))


def _ffn_kernel(xp_ref, x_ref, xn_ref, gpre_ref, wg_ref, wv_ref, cwg_ref, cwv_ref, cbg_ref, cbv_ref,
                wd_ref, gpost_ref, o_ref, hn_ref, ug_ref, uv_ref, *, tiles_per_seq):
    i = pl.program_id(0)
    j = pl.program_id(1)
    tm = x_ref.shape[0]
    halo = xp_ref.shape[0]

    @pl.when(j == 0)
    def _():
        g = gpre_ref[...]
        pos = i % tiles_per_seq
        hp = jnp.where(pos == 0, 0.0, _rms(xp_ref[...], g))
        hx = jnp.where(pos == tiles_per_seq - 1, 0.0, _rms(xn_ref[...], g))
        hn_ref[0:halo, :] = hp.astype(BF16)
        hn_ref[halo + tm:, :] = hx.astype(BF16)
        for r in range(0, tm, FFN_NORM_ROWS):
            rows = slice(r, r + FFN_NORM_ROWS)
            hn_ref[halo + r:halo + r + FFN_NORM_ROWS, :] = _rms(x_ref[rows, :], g).astype(BF16)
        o_ref[...] = jnp.zeros_like(o_ref)

    hn = hn_ref[...]
    ug_ref[...] = _dot(hn, wg_ref[...])
    uv_ref[...] = _dot(hn, wv_ref[...])

    def conv(u_ref, cw_ref, cb_ref):
        return (u_ref[halo - 1:halo - 1 + tm, :] * cw_ref[0:1, :]
                + u_ref[halo:halo + tm, :] * cw_ref[1:2, :]
                + u_ref[halo + 1:halo + 1 + tm, :] * cw_ref[2:3, :]
                + cb_ref[...])

    act = _gelu_tanh(conv(ug_ref, cwg_ref, cbg_ref)) * conv(uv_ref, cwv_ref, cbv_ref)
    o_ref[...] += _dot(act.astype(BF16), wd_ref[...])

    @pl.when(j == pl.num_programs(1) - 1)
    def _():
        for r in range(0, tm, FFN_NORM_ROWS):
            rows = slice(r, r + FFN_NORM_ROWS)
            o_ref[rows, :] = x_ref[rows, :] + _rms(o_ref[rows, :], gpost_ref[...])


def _ffn(x, g_pre, w_up, conv_w, conv_b, w_down, g_post, *, layer, seq, tm, tf):
    t, d = x.shape
    d_ff = w_down.shape[1]
    nf = d_ff // tf
    halo = BF16_SUBLANES
    hb = tm // halo
    n_hblk = t // halo
    kern = functools.partial(_ffn_kernel, tiles_per_seq=seq // tm)
    return pl.pallas_call(
        kern,
        grid=(t // tm, nf),
        in_specs=[
            pl.BlockSpec((halo, d), lambda i, j: (jnp.maximum(i * hb - 1, 0), 0)),
            pl.BlockSpec((tm, d), lambda i, j: (i, 0)),
            pl.BlockSpec((halo, d), lambda i, j: (jnp.minimum((i + 1) * hb, n_hblk - 1), 0)),
            pl.BlockSpec((1, d), lambda i, j: (0, 0)),
            _layer_spec((d, tf), lambda i, j: (0, j), layer),
            _layer_spec((d, tf), lambda i, j: (0, nf + j), layer),
            pl.BlockSpec((3, tf), lambda i, j: (0, j)),
            pl.BlockSpec((3, tf), lambda i, j: (0, nf + j)),
            pl.BlockSpec((1, tf), lambda i, j: (0, j)),
            pl.BlockSpec((1, tf), lambda i, j: (0, nf + j)),
            _layer_spec((tf, d), lambda i, j: (j, 0), layer),
            pl.BlockSpec((1, d), lambda i, j: (0, 0)),
        ],
        out_specs=pl.BlockSpec((tm, d), lambda i, j: (i, 0)),
        out_shape=jax.ShapeDtypeStruct((t, d), F32),
        scratch_shapes=[
            pltpu.VMEM((tm + 2 * halo, d), BF16),
            pltpu.VMEM((tm + 2 * halo, tf), F32),
            pltpu.VMEM((tm + 2 * halo, tf), F32),
        ],
        compiler_params=_params("parallel", "arbitrary"),
        name="conv_ffn",
    )(x, x, x, g_pre, w_up, w_up, conv_w, conv_w, conv_b, conv_b, w_down, g_post)


def _head_lane_order():
    quarter = HEAD_DIM // 4
    blocks = (0, 2, 1, 3)
    return jnp.concatenate([jnp.arange(b * quarter, (b + 1) * quarter) for b in blocks])


def _rope_tables(seq):
    half = HEAD_DIM // 2
    pos = jnp.arange(seq)
    row = (pos // GRID_W).astype(F32)
    col = (pos % GRID_W).astype(F32)
    inv_freq = 1.0 / (ROPE_THETA ** (jnp.arange(0, half, 2, dtype=F32) / half))
    lane = _head_lane_order()
    freq = inv_freq[(lane % half) % (half // 2)]
    ids = jnp.where(lane[None, :] < half, row[:, None], col[:, None])
    ang = ids * freq[None, :]
    sign = jnp.where((lane % half) < half // 2, -1.0, 1.0)
    return jnp.cos(ang), jnp.sin(ang) * sign[None, :]


def _dft_tables(seq, gdim, fr):
    def cs(k, n, period):
        r = (k[..., None] * n) % period
        ang = r.astype(F32) * (2.0 * math.pi / period)
        return jnp.cos(ang), jnp.sin(ang)
    c = jnp.arange(gdim, dtype=jnp.int32)
    cc, sc = (t * gdim ** -0.5 for t in cs(c, c, gdim))
    rows = fr + FOUR_EXTRA_ROWS
    k = jnp.arange(seq // 2 // fr, dtype=jnp.int32)[:, None] * fr + jnp.arange(rows, dtype=jnp.int32)
    ca, sa = cs(k, GRID_W * jnp.arange(seq // 2 // GRID_W, dtype=jnp.int32), seq)
    cb, sb = cs(k, jnp.arange(GRID_W, dtype=jnp.int32), seq)
    scale = seq ** -0.5
    ch = ((ca[..., :, None] * cb[..., None, :] - sa[..., :, None] * sb[..., None, :]) * scale)
    sh = ((sa[..., :, None] * cb[..., None, :] + ca[..., :, None] * sb[..., None, :]) * scale)
    ch, sh = ch.reshape(*k.shape, seq // 2), sh.reshape(*k.shape, seq // 2)
    flip = (jnp.arange(rows)[None, :] == fr - jnp.arange(fr)[:, None]).astype(BF16)
    return (jnp.concatenate([cc, sc], axis=1).astype(BF16), ch.astype(BF16), sh.astype(BF16), flip)


def _tiles(seq, d_ff):
    def pick(n, pref):
        return pref if n % pref == 0 else n
    tf = next(c for c in (512, 256, 128) if d_ff % c == 0)
    return dict(
        qk_tm=pick(seq, 512), mixer_tm=pick(seq, 1024), uv_tn=768, gate_tn=1024,
        attn_tq=pick(seq, 512), attn_kc=pick(seq, 256),
        merge_tm=pick(seq, 256),
        xattn_tm=pick(seq, 512),
        ffn_tm=pick(seq, 512), ffn_tf=tf,
    )


def kernel(x, mem, mix_pre_g, w_in, q_norm_g, k_norm_g, w_attn_o, w_four_o, w_gate, b_gate, w_mix_o,
           mix_post_g, xa_pre_g, mem_norm_g, w_xq, w_xkv, w_xo, xa_post_g, ffn_pre_g, w_up, conv_w,
           conv_b, w_down, ffn_post_g):
    batch, seq, d = x.shape
    mem_tokens = mem.shape[1]
    depth = w_in.shape[0]
    q_width = w_attn_o.shape[1]
    f_width = w_four_o.shape[1]
    kv_width = (w_in.shape[2] - q_width - f_width) // 2
    n_heads = q_width // HEAD_DIM
    assert n_heads == N_KV * GROUP and kv_width == N_KV * HEAD_DIM
    assert seq % GRID_W == 0 and w_xq.shape[2] == X_HEADS * X_HEAD_DIM
    d_ff = w_down.shape[1]
    ts = _tiles(seq, d_ff)
    qk_width = q_width + kv_width
    u_col = q_width + 2 * kv_width
    tm = ts["mixer_tm"]
    assert (kv_width + f_width) % ts["uv_tn"] == 0 and w_gate.shape[2] % ts["gate_tn"] == 0

    cos_t, sin_t = _rope_tables(seq)
    assert seq % (2 * ts["merge_tm"]) == 0
    dft_c, dft_ch, dft_sh, dft_flip = _dft_tables(seq, f_width // N_FOURIER_GROUPS, ts["merge_tm"])
    order = _head_lane_order()
    partner = jnp.roll(jnp.arange(HEAD_DIM), HEAD_DIM // 2)
    bias_spec = pl.BlockSpec((1, ts["gate_tn"]), lambda i, j: (0, j))

    row = lambda v: v.reshape(1, -1)
    xf = x.reshape(batch * seq, d)
    memf = mem.reshape(batch * mem_tokens, d)

    quarter = HEAD_DIM // 4
    w_qk_t = w_in[:, :, :qk_width].astype(BF16).reshape(depth, d, qk_width // HEAD_DIM, 2, 2, quarter)
    w_qk_t = w_qk_t.transpose(0, 2, 4, 3, 5, 1).reshape(depth, qk_width, d)
    w_uv = jnp.concatenate([w_in[:, :, u_col:], w_in[:, :, qk_width:u_col]], axis=2).astype(BF16)
    w_gate_b, w_attn_o_b, w_four_o_b, w_mix_o_b, w_xq_b, w_xkv_b, w_xo_b, w_up_b, w_down_b = (
        w.astype(BF16) for w in (w_gate, w_attn_o, w_four_o, w_mix_o, w_xq, w_xkv, w_xo, w_up, w_down))

    for l in range(depth):
        q_gain = q_norm_g[l][order] * (HEAD_DIM ** -0.5 * math.log2(math.e))
        k_gain = k_norm_g[l][order]
        cg = jnp.concatenate([g[:, None] * cos_t.T for g in (q_gain, k_gain)], axis=0)
        sg = jnp.concatenate([g[partner][:, None] * sin_t.T for g in (q_gain, k_gain)], axis=0)
        g_pre = row(mix_pre_g[l])
        q, k_t = _proj_qk_t(xf, g_pre, w_qk_t, cg, sg, layer=l, seq=seq, n_q_heads=n_heads,
                            tm=ts["qk_tm"])
        q_bound = (-(HEAD_DIM ** 0.5) * jnp.max(jnp.abs(q_gain))) * (jnp.arange(HEAD_DIM) == 0)
        uv = _proj(xf, g_pre, w_uv, _epilogue_plain, (), (), layer=l, tm=tm, tn=ts["uv_tn"],
                   name="proj_uv")
        gates = _proj(xf, g_pre, w_gate_b, _epilogue_gate, (row(b_gate[l]),), (bias_spec,),
                      layer=l, tm=tm, tn=ts["gate_tn"], name="proj_gate")
        att = _attention(q, k_t, uv, row(q_bound.astype(F32)), batch=batch, seq=seq, q_width=q_width, v_col=f_width,
                         tq=ts["attn_tq"], kc=ts["attn_kc"])
        ucs, mid = _four_chan(uv, dft_c, dft_flip, batch=batch, seq=seq, f_width=f_width)
        four = _four_seq(dft_ch, dft_sh, dft_flip, ucs, mid, batch=batch, seq=seq, f_width=f_width)
        xf = _merge(att, four.reshape(batch * seq, f_width), gates, xf, w_attn_o_b, w_four_o_b,
                    w_mix_o_b, row(mix_post_g[l]), layer=l, seq=seq, tm=ts["merge_tm"])

        kv = _norm_matmul(memf, row(mem_norm_g[l]), w_xkv_b, layer=l, tm=mem_tokens)
        xf = _xattn(xf, row(xa_pre_g[l]), w_xq_b, kv, w_xo_b, row(xa_post_g[l]), layer=l,
                    batch=batch, seq=seq, mem_tokens=mem_tokens, tm=ts["xattn_tm"])

        xf = _ffn(xf, row(ffn_pre_g[l]), w_up_b, conv_w[l], row(conv_b[l]), w_down_b,
                  row(ffn_post_g[l]), layer=l, seq=seq, tm=ts["ffn_tm"], tf=ts["ffn_tf"])

    return xf.reshape(batch, seq, d)
```

```python
import functools
import math

import jax
import jax.numpy as jnp
from jax import lax
from jax.experimental import pallas as pl
from jax.experimental.pallas import tpu as pltpu

HEAD_DIM = 128
N_KV = 4
GROUP = 4
GRID_W = 64
ROPE_THETA = 10000.0
N_FOURIER_GROUPS = 4
X_HEADS = 4
X_HEAD_DIM = 128
EPS = 1e-6

V7X_VMEM_LIMIT_BYTES = 56 * 1024 * 1024
BF16_SUBLANES = 16
ATTN_MIN_DENOMINATOR = 2.0 ** -60
ATTN_SHIFT_MARGIN = 1.0 + 2.0 ** -6
FOUR_EXTRA_ROWS = BF16_SUBLANES
FFN_NORM_ROWS = 128

F32 = jnp.float32
BF16 = jnp.bfloat16


def _params(*sem):
    return pltpu.CompilerParams(dimension_semantics=sem, vmem_limit_bytes=V7X_VMEM_LIMIT_BYTES)


def _rms(xf, g):
    ms = jnp.mean(xf * xf, axis=-1, keepdims=True)
    return xf * lax.rsqrt(ms + EPS) * g


def _dot(a, b):
    return jnp.dot(a, b, preferred_element_type=F32)


def _dot_nt(a, b):
    return lax.dot_general(a, b, (((1,), (1,)), ((), ())), preferred_element_type=F32)


def _epilogue_plain(z, o_ref):
    o_ref[...] = z.astype(BF16)


def _epilogue_gate(z, b_ref, o_ref):
    o_ref[...] = (0.5 * jnp.tanh(0.5 * (z + b_ref[...])) + 0.5).astype(BF16)


def _proj_qk_t_kernel(x_ref, g_ref, wt_ref, cg_ref, sg_ref, q_ref, kt_ref, *, n_q_heads):
    hn = _rms(x_ref[...], g_ref[...]).astype(BF16)
    zt = _dot_nt(wt_ref[...], hn)
    half = HEAD_DIM // 2
    for h in range(zt.shape[0] // HEAD_DIM):
        tab = slice(0, HEAD_DIM) if h < n_q_heads else slice(HEAD_DIM, 2 * HEAD_DIM)
        y = zt[h * HEAD_DIM:(h + 1) * HEAD_DIM, :]
        y = y * lax.rsqrt(jnp.mean(y * y, axis=0, keepdims=True) + EPS)
        rot = jnp.concatenate([y[half:], y[:half]], axis=0)
        out = (y * cg_ref[tab, :] + rot * sg_ref[tab, :]).astype(BF16)
        if h < n_q_heads:
            q_ref[:, h * HEAD_DIM:(h + 1) * HEAD_DIM] = out.T
        else:
            kt_ref[(h - n_q_heads) * HEAD_DIM:(h - n_q_heads + 1) * HEAD_DIM, :] = out


def _proj_qk_t(x, g_pre, w_t, cg, sg, *, layer, seq, n_q_heads, tm):
    t, d = x.shape
    n = w_t.shape[1]
    seq_tiles = seq // tm
    table_spec = pl.BlockSpec((2 * HEAD_DIM, tm), lambda i: (0, i % seq_tiles))
    return pl.pallas_call(
        functools.partial(_proj_qk_t_kernel, n_q_heads=n_q_heads),
        grid=(t // tm,),
        in_specs=[
            pl.BlockSpec((tm, d), lambda i: (i, 0)),
            _resident(g_pre.shape),
            _resident_layer(w_t, layer),
            table_spec,
            table_spec,
        ],
        out_specs=[
            pl.BlockSpec((tm, n_q_heads * HEAD_DIM), lambda i: (i, 0)),
            pl.BlockSpec((n - n_q_heads * HEAD_DIM, tm), lambda i: (0, i)),
        ],
        out_shape=[
            jax.ShapeDtypeStruct((t, n_q_heads * HEAD_DIM), BF16),
            jax.ShapeDtypeStruct((n - n_q_heads * HEAD_DIM, t), BF16),
        ],
        compiler_params=_params("parallel"),
        name="proj_qk",
    )(x, g_pre, w_t, cg, sg)


def _proj_kernel(x_ref, g_ref, w_ref, *rest, epilogue):
    hn_ref = rest[-1]

    @pl.when(pl.program_id(1) == 0)
    def _():
        hn_ref[...] = _rms(x_ref[...], g_ref[...]).astype(BF16)

    epilogue(_dot(hn_ref[...], w_ref[...]), *rest[:-1])


def _proj(x, g_pre, w, epilogue, extra, extra_specs, *, layer, tm, tn, name):
    t, d = x.shape
    n = w.shape[2]
    return pl.pallas_call(
        functools.partial(_proj_kernel, epilogue=epilogue),
        grid=(t // tm, n // tn),
        in_specs=[
            pl.BlockSpec((tm, d), lambda i, j: (i, 0)),
            pl.BlockSpec((1, d), lambda i, j: (0, 0)),
            _resident_layer(w, layer) if tn == n else _layer_spec((d, tn), lambda i, j: (0, j), layer),
            *extra_specs,
        ],
        out_specs=pl.BlockSpec((tm, tn), lambda i, j: (i, j)),
        out_shape=jax.ShapeDtypeStruct((t, n), BF16),
        scratch_shapes=[pltpu.VMEM((tm, d), BF16)],
        compiler_params=_params("parallel", "arbitrary"),
        name=name,
    )(x, g_pre, w, *extra)


def _attn_exact(q, ka_ref, va_ref, kc):
    rows = q.shape[0]
    m = jnp.full((rows, 1), -jnp.inf, F32)
    l = jnp.zeros((rows, 1), F32)
    acc = jnp.zeros((rows, HEAD_DIM), F32)
    for c in range(ka_ref.shape[0] // kc):
        ks = slice(c * kc, (c + 1) * kc)
        s = _dot_nt(q, ka_ref[ks, :HEAD_DIM])
        m_new = jnp.maximum(m, jnp.max(s, axis=-1, keepdims=True))
        alpha = jnp.exp2(m - m_new)
        p = jnp.exp2(s - m_new)
        l = alpha * l + jnp.sum(p, axis=-1, keepdims=True)
        acc = alpha * acc + _dot(p.astype(BF16), va_ref[ks, :HEAD_DIM])
        m = m_new
    return acc * (1.0 / l)


def _attn_kernel(q_ref, kt_ref, v_ref, qb_ref, o_ref, ka_ref, va_ref, *, kc):
    tq = q_ref.shape[0]
    s_len = v_ref.shape[0]
    n_chunks = s_len // kc
    lane = lax.broadcasted_iota(jnp.int32, (1, HEAD_DIM), 1)

    @pl.when(pl.program_id(2) == 0)
    def _():
        lane0 = jnp.where(lane == 0, 1.0, 0.0)
        k2max = jnp.zeros((1, 1), F32)
        for c in range(n_chunks):
            kf = kt_ref[:, c * kc:(c + 1) * kc].astype(F32)
            k2max = jnp.maximum(k2max, jnp.max(jnp.sum(kf * kf, axis=0, keepdims=True),
                                               axis=1, keepdims=True))
        kmax_col = jnp.broadcast_to(jnp.sqrt(k2max) * ATTN_SHIFT_MARGIN * lane0, (kc, HEAD_DIM))
        ones_col = jnp.broadcast_to(lane0, (kc, HEAD_DIM)).astype(BF16)
        for c in range(n_chunks):
            ks = slice(c * kc, (c + 1) * kc)
            ka_ref[ks, :HEAD_DIM] = kt_ref[:, ks].T
            ka_ref[ks, HEAD_DIM:] = kmax_col.astype(BF16)
            va_ref[ks, :HEAD_DIM] = v_ref[ks, :]
            va_ref[ks, HEAD_DIM:] = ones_col

    bound_col = jnp.broadcast_to(qb_ref[...], (tq, HEAD_DIM)).astype(BF16)
    q_aug = jnp.concatenate(
        [jnp.concatenate([q_ref[:, g * HEAD_DIM:(g + 1) * HEAD_DIM], bound_col], axis=1)
         for g in range(GROUP)], axis=0)
    acc = jnp.zeros((GROUP * tq, 2 * HEAD_DIM), F32)
    for c in range(n_chunks):
        ks = slice(c * kc, (c + 1) * kc)
        p = jnp.exp2(_dot_nt(q_aug, ka_ref[ks, :])).astype(BF16)
        acc = acc + _dot(p, va_ref[ks, :])
    l = acc[:, HEAD_DIM:HEAD_DIM + 1]
    accurate = jnp.min(l) >= ATTN_MIN_DENOMINATOR

    def write(o):
        for g in range(GROUP):
            o_ref[:, g * HEAD_DIM:(g + 1) * HEAD_DIM] = o[g * tq:(g + 1) * tq].astype(BF16)

    write(acc[:, :HEAD_DIM] * (1.0 / l))

    @pl.when(jnp.logical_not(accurate))
    def _():
        write(_attn_exact(q_aug[:, :HEAD_DIM], ka_ref, va_ref, kc))


def _attention(q, k_t, vu, q_bound, *, batch, seq, q_width, v_col, tq, kc):
    t = vu.shape[0]
    gw = GROUP * HEAD_DIM
    v_blk0 = v_col // HEAD_DIM
    nq = seq // tq
    return pl.pallas_call(
        functools.partial(_attn_kernel, kc=kc),
        grid=(batch, N_KV, nq),
        in_specs=[
            pl.BlockSpec((tq, gw), lambda b, h, i: (b * nq + i, h)),
            pl.BlockSpec((HEAD_DIM, seq), lambda b, h, i: (h, b)),
            pl.BlockSpec((seq, HEAD_DIM), lambda b, h, i: (b, v_blk0 + h)),
            _resident(q_bound.shape),
        ],
        out_specs=pl.BlockSpec((tq, gw), lambda b, h, i: (b * nq + i, h)),
        out_shape=jax.ShapeDtypeStruct((t, q_width), BF16),
        scratch_shapes=[
            pltpu.VMEM((seq, 2 * HEAD_DIM), BF16),
            pltpu.VMEM((seq, 2 * HEAD_DIM), BF16),
        ],
        compiler_params=_params("parallel", "parallel", "arbitrary"),
        name="attention",
    )(q, k_t, vu, q_bound)


def _four_chan_kernel(u_ref, p_ref, e_ref, m_ref, dft_ref, flip_ref, o_ref, mid_ref, *, gdim):
    i = pl.program_id(1)
    ext = jnp.where(i == 0, 0.0, e_ref[...].astype(F32)).astype(BF16)
    partner = _dot(flip_ref[...], jnp.concatenate([p_ref[...], ext], axis=0))
    u = u_ref[...].astype(F32)
    even = (u + partner).astype(BF16)
    odd = (u - partner).astype(BF16)
    for g in range(u_ref.shape[1] // gdim):
        sl = slice(g * gdim, (g + 1) * gdim)
        o_ref[0, 0, :, sl] = _dot(even[:, sl], dft_ref[:, :gdim]).astype(BF16)
        o_ref[0, 1, :, sl] = _dot(odd[:, sl], dft_ref[:, gdim:]).astype(BF16)

    @pl.when(i == 0)
    def _():
        for g in range(u_ref.shape[1] // gdim):
            sl = slice(g * gdim, (g + 1) * gdim)
            mid_ref[0, :, sl] = _dot(m_ref[:, sl], dft_ref[:, :gdim])


def _four_chan(uv, dft_c, flip, *, batch, seq, f_width):
    gdim = f_width // N_FOURIER_GROUPS
    fr = flip.shape[0]
    nb = seq // fr
    ext = FOUR_EXTRA_ROWS
    last_ext = uv.shape[0] // ext - 1
    return pl.pallas_call(
        functools.partial(_four_chan_kernel, gdim=gdim),
        grid=(batch, nb // 2),
        in_specs=[
            pl.BlockSpec((fr, f_width), lambda b, i: (b * nb + i, 0)),
            pl.BlockSpec((fr, f_width), lambda b, i: (b * nb + nb - 1 - i, 0)),
            pl.BlockSpec((ext, f_width),
                         lambda b, i: (jnp.minimum(((b + 1) * nb - i) * (fr // ext), last_ext), 0)),
            pl.BlockSpec((ext, f_width), lambda b, i: ((2 * b + 1) * (seq // 2 // ext), 0)),
            _resident(dft_c.shape),
            _resident(flip.shape),
        ],
        out_specs=[
            pl.BlockSpec((1, 2, fr, f_width), lambda b, i: (b, 0, i, 0)),
            pl.BlockSpec((1, ext, f_width), lambda b, i: (b, 0, 0)),
        ],
        out_shape=[
            jax.ShapeDtypeStruct((batch, 2, seq // 2, f_width), BF16),
            jax.ShapeDtypeStruct((batch, ext, f_width), F32),
        ],
        compiler_params=_params("parallel", "arbitrary"),
        name="fourier_chan",
    )(uv, uv, uv, uv, dft_c, flip)


def _four_seq_kernel(ch_ref, sh_ref, uc_ref, us_ref, mid_ref, flip_ref, o_ref, *, seq):
    fr = o_ref.shape[3]
    rows = ch_ref.shape[1]
    p = _dot(ch_ref[0], uc_ref[0, 0])
    q = _dot(sh_ref[0], us_ref[0, 0])
    k = pl.program_id(1) * fr + lax.broadcasted_iota(jnp.int32, (rows, 1), 0)
    p = p + jnp.where(k % 2 == 0, seq ** -0.5, -(seq ** -0.5)) * mid_ref[0, 0:1, :]
    o_ref[0, 0, 0] = (p - q)[:fr].astype(BF16)
    o_ref[0, 1, 0] = _dot(flip_ref[...], (p + q).astype(BF16)).astype(BF16)


def _four_seq(ch, sh, flip, ucs, mid, *, batch, seq, f_width):
    nt, rows, half = ch.shape
    fr = flip.shape[0]
    table_spec = pl.BlockSpec((1, rows, half), lambda b, t: (t, 0, 0))
    return pl.pallas_call(
        functools.partial(_four_seq_kernel, seq=seq),
        grid=(batch, nt),
        in_specs=[
            table_spec,
            table_spec,
            pl.BlockSpec((1, 1, half, f_width), lambda b, t: (b, 0, 0, 0)),
            pl.BlockSpec((1, 1, half, f_width), lambda b, t: (b, 1, 0, 0)),
            pl.BlockSpec((1, FOUR_EXTRA_ROWS, f_width), lambda b, t: (b, 0, 0)),
            _resident(flip.shape),
        ],
        out_specs=pl.BlockSpec((1, 2, 1, fr, f_width), lambda b, t: (b, 0, t, 0, 0)),
        out_shape=jax.ShapeDtypeStruct((batch, 2, nt, fr, f_width), BF16),
        compiler_params=_params("parallel", "arbitrary"),
        name="fourier_seq",
    )(ch, sh, ucs, ucs, mid, flip)


def _merge_kernel(att_ref, y_ref, g1_ref, g2_ref, x_ref, wa_ref, wf_ref, wm_ref, gp_ref, o_ref):
    a = _dot(att_ref[...], wa_ref[...])
    f = _dot(y_ref[...], wf_ref[...])
    merged = g1_ref[...].astype(F32) * a + g2_ref[...].astype(F32) * f
    y = _dot(merged.astype(BF16), wm_ref[...])
    o_ref[...] = x_ref[...] + _rms(y, gp_ref[...])


def _resident(shape):
    return pl.BlockSpec(shape, lambda *_: (0,) * len(shape), pipeline_mode=pl.Buffered(1))


def _layer_spec(block, index_map, layer, **kwargs):
    return pl.BlockSpec((None, *block), lambda *idx: (layer, *index_map(*idx)), **kwargs)


def _resident_layer(w, layer):
    return _layer_spec(w.shape[1:], lambda *_: (0,) * (w.ndim - 1), layer,
                       pipeline_mode=pl.Buffered(1))


def _merge(att, four, gates, x, wa, wf, wm, g_post, *, layer, seq, tm):
    t, d = x.shape
    tiles = seq // tm

    def four_block(i):
        it = i % tiles
        return (i - it + jnp.where(it < tiles // 2, it, tiles + tiles // 2 - 1 - it), 0)

    return pl.pallas_call(
        _merge_kernel,
        grid=(t // tm,),
        in_specs=[
            pl.BlockSpec((tm, att.shape[1]), lambda i: (i, 0)),
            pl.BlockSpec((tm, four.shape[1]), four_block),
            pl.BlockSpec((tm, d), lambda i: (i, 0)),
            pl.BlockSpec((tm, d), lambda i: (i, 1)),
            pl.BlockSpec((tm, d), lambda i: (i, 0)),
            _resident_layer(wa, layer),
            _resident_layer(wf, layer),
            _resident_layer(wm, layer),
            _resident(g_post.shape),
        ],
        out_specs=pl.BlockSpec((tm, d), lambda i: (i, 0)),
        out_shape=jax.ShapeDtypeStruct((t, d), F32),
        compiler_params=_params("parallel"),
        name="merge",
    )(att, four, gates, gates, x, wa, wf, wm, g_post)


def _norm_matmul_kernel(x_ref, g_ref, w_ref, o_ref):
    hn = _rms(x_ref[...], g_ref[...]).astype(BF16)
    o_ref[...] = _dot(hn, w_ref[...]).astype(BF16)


def _norm_matmul(x, g, w, *, layer, tm):
    t, d = x.shape
    n = w.shape[2]
    return pl.pallas_call(
        _norm_matmul_kernel,
        grid=(t // tm,),
        in_specs=[
            pl.BlockSpec((tm, d), lambda i: (i, 0)),
            _resident(g.shape),
            _resident_layer(w, layer),
        ],
        out_specs=pl.BlockSpec((tm, n), lambda i: (i, 0)),
        out_shape=jax.ShapeDtypeStruct((t, n), BF16),
        compiler_params=_params("parallel"),
        name="mem_kv",
    )(x, g, w)


def _xattn_kernel(x_ref, gpre_ref, wq_ref, k_ref, v_ref, wo_ref, gpost_ref, o_ref):
    x = x_ref[...]
    hn = _rms(x, gpre_ref[...]).astype(BF16)
    q = (_dot(hn, wq_ref[...]) * (X_HEAD_DIM ** -0.5)).astype(BF16)
    outs = []
    for h in range(X_HEADS):
        sl = slice(h * X_HEAD_DIM, (h + 1) * X_HEAD_DIM)
        s = _dot_nt(q[:, sl], k_ref[:, sl])
        m = jnp.max(s, axis=-1, keepdims=True)
        p = jnp.exp(s - m)
        l = jnp.sum(p, axis=-1, keepdims=True)
        outs.append(_dot(p.astype(BF16), v_ref[:, sl]) * (1.0 / l))
    o = jnp.concatenate(outs, axis=1).astype(BF16)
    y = _dot(o, wo_ref[...])
    o_ref[...] = x + _rms(y, gpost_ref[...])


def _xattn(x, g_pre, wq, kv, wo, g_post, *, layer, batch, seq, mem_tokens, tm):
    t, d = x.shape
    xw = wq.shape[2]
    ns = seq // tm
    return pl.pallas_call(
        _xattn_kernel,
        grid=(batch, ns),
        in_specs=[
            pl.BlockSpec((tm, d), lambda b, i: (b * ns + i, 0)),
            _resident(g_pre.shape),
            _resident_layer(wq, layer),
            pl.BlockSpec((mem_tokens, xw), lambda b, i: (b, 0)),
            pl.BlockSpec((mem_tokens, xw), lambda b, i: (b, 1)),
            _resident_layer(wo, layer),
            _resident(g_post.shape),
        ],
        out_specs=pl.BlockSpec((tm, d), lambda b, i: (b * ns + i, 0)),
        out_shape=jax.ShapeDtypeStruct((t, d), F32),
        compiler_params=_params("parallel", "parallel"),
        name="xattn",
    )(x, g_pre, wq, kv, kv, wo, g_post)


def _gelu_tanh(x):
    return 0.5 * x * (1.0 + jnp.tanh(math.sqrt(2.0 / math.pi) * (x + 0.044715 * (x * x * x))))


def _ffn_kernel(xp_ref, x_ref, xn_ref, gpre_ref, wg_ref, wv_ref, cwg_ref, cwv_ref, cbg_ref, cbv_ref,
                wd_ref, gpost_ref, o_ref, hn_ref, ug_ref, uv_ref, *, tiles_per_seq):
    i = pl.program_id(0)
    j = pl.program_id(1)
    tm = x_ref.shape[0]
    halo = xp_ref.shape[0]

    @pl.when(j == 0)
    def _():
        g = gpre_ref[...]
        pos = i % tiles_per_seq
        hp = jnp.where(pos == 0, 0.0, _rms(xp_ref[...], g))
        hx = jnp.where(pos == tiles_per_seq - 1, 0.0, _rms(xn_ref[...], g))
        hn_ref[0:halo, :] = hp.astype(BF16)
        hn_ref[halo + tm:, :] = hx.astype(BF16)
        for r in range(0, tm, FFN_NORM_ROWS):
            rows = slice(r, r + FFN_NORM_ROWS)
            hn_ref[halo + r:halo + r + FFN_NORM_ROWS, :] = _rms(x_ref[rows, :], g).astype(BF16)
        o_ref[...] = jnp.zeros_like(o_ref)

    hn = hn_ref[...]
    ug_ref[...] = _dot(hn, wg_ref[...])
    uv_ref[...] = _dot(hn, wv_ref[...])

    def conv(u_ref, cw_ref, cb_ref):
        return (u_ref[halo - 1:halo - 1 + tm, :] * cw_ref[0:1, :]
                + u_ref[halo:halo + tm, :] * cw_ref[1:2, :]
                + u_ref[halo + 1:halo + 1 + tm, :] * cw_ref[2:3, :]
                + cb_ref[...])

    act = _gelu_tanh(conv(ug_ref, cwg_ref, cbg_ref)) * conv(uv_ref, cwv_ref, cbv_ref)
    o_ref[...] += _dot(act.astype(BF16), wd_ref[...])

    @pl.when(j == pl.num_programs(1) - 1)
    def _():
        for r in range(0, tm, FFN_NORM_ROWS):
            rows = slice(r, r + FFN_NORM_ROWS)
            o_ref[rows, :] = x_ref[rows, :] + _rms(o_ref[rows, :], gpost_ref[...])


def _ffn(x, g_pre, w_up, conv_w, conv_b, w_down, g_post, *, layer, seq, tm, tf):
    t, d = x.shape
    d_ff = w_down.shape[1]
    nf = d_ff // tf
    halo = BF16_SUBLANES
    hb = tm // halo
    n_hblk = t // halo
    kern = functools.partial(_ffn_kernel, tiles_per_seq=seq // tm)
    return pl.pallas_call(
        kern,
        grid=(t // tm, nf),
        in_specs=[
            pl.BlockSpec((halo, d), lambda i, j: (jnp.maximum(i * hb - 1, 0), 0)),
            pl.BlockSpec((tm, d), lambda i, j: (i, 0)),
            pl.BlockSpec((halo, d), lambda i, j: (jnp.minimum((i + 1) * hb, n_hblk - 1), 0)),
            pl.BlockSpec((1, d), lambda i, j: (0, 0)),
            _layer_spec((d, tf), lambda i, j: (0, j), layer),
            _layer_spec((d, tf), lambda i, j: (0, nf + j), layer),
            pl.BlockSpec((3, tf), lambda i, j: (0, j)),
            pl.BlockSpec((3, tf), lambda i, j: (0, nf + j)),
            pl.BlockSpec((1, tf), lambda i, j: (0, j)),
            pl.BlockSpec((1, tf), lambda i, j: (0, nf + j)),
            _layer_spec((tf, d), lambda i, j: (j, 0), layer),
            pl.BlockSpec((1, d), lambda i, j: (0, 0)),
        ],
        out_specs=pl.BlockSpec((tm, d), lambda i, j: (i, 0)),
        out_shape=jax.ShapeDtypeStruct((t, d), F32),
        scratch_shapes=[
            pltpu.VMEM((tm + 2 * halo, d), BF16),
            pltpu.VMEM((tm + 2 * halo, tf), F32),
            pltpu.VMEM((tm + 2 * halo, tf), F32),
        ],
        compiler_params=_params("parallel", "arbitrary"),
        name="conv_ffn",
    )(x, x, x, g_pre, w_up, w_up, conv_w, conv_w, conv_b, conv_b, w_down, g_post)


def _head_lane_order():
    quarter = HEAD_DIM // 4
    blocks = (0, 2, 1, 3)
    return jnp.concatenate([jnp.arange(b * quarter, (b + 1) * quarter) for b in blocks])


def _rope_tables(seq):
    half = HEAD_DIM // 2
    pos = jnp.arange(seq)
    row = (pos // GRID_W).astype(F32)
    col = (pos % GRID_W).astype(F32)
    inv_freq = 1.0 / (ROPE_THETA ** (jnp.arange(0, half, 2, dtype=F32) / half))
    lane = _head_lane_order()
    freq = inv_freq[(lane % half) % (half // 2)]
    ids = jnp.where(lane[None, :] < half, row[:, None], col[:, None])
    ang = ids * freq[None, :]
    sign = jnp.where((lane % half) < half // 2, -1.0, 1.0)
    return jnp.cos(ang), jnp.sin(ang) * sign[None, :]


def _dft_tables(seq, gdim, fr):
    def cs(k, n, period):
        r = (k[..., None] * n) % period
        ang = r.astype(F32) * (2.0 * math.pi / period)
        return jnp.cos(ang), jnp.sin(ang)
    c = jnp.arange(gdim, dtype=jnp.int32)
    cc, sc = (t * gdim ** -0.5 for t in cs(c, c, gdim))
    rows = fr + FOUR_EXTRA_ROWS
    k = jnp.arange(seq // 2 // fr, dtype=jnp.int32)[:, None] * fr + jnp.arange(rows, dtype=jnp.int32)
    ca, sa = cs(k, GRID_W * jnp.arange(seq // 2 // GRID_W, dtype=jnp.int32), seq)
    cb, sb = cs(k, jnp.arange(GRID_W, dtype=jnp.int32), seq)
    scale = seq ** -0.5
    ch = ((ca[..., :, None] * cb[..., None, :] - sa[..., :, None] * sb[..., None, :]) * scale)
    sh = ((sa[..., :, None] * cb[..., None, :] + ca[..., :, None] * sb[..., None, :]) * scale)
    ch, sh = ch.reshape(*k.shape, seq // 2), sh.reshape(*k.shape, seq // 2)
    flip = (jnp.arange(rows)[None, :] == fr - jnp.arange(fr)[:, None]).astype(BF16)
    return (jnp.concatenate([cc, sc], axis=1).astype(BF16), ch.astype(BF16), sh.astype(BF16), flip)


def _tiles(seq, d_ff):
    def pick(n, pref):
        return pref if n % pref == 0 else n
    tf = next(c for c in (512, 256, 128) if d_ff % c == 0)
    return dict(
        qk_tm=pick(seq, 512), mixer_tm=pick(seq, 512), uv_tn=1536, gate_tn=4096,
        attn_tq=pick(seq, 512), attn_kc=pick(seq, 256),
        merge_tm=pick(seq, 256),
        xattn_tm=pick(seq, 512),
        ffn_tm=pick(seq, 512), ffn_tf=tf,
    )


def kernel(x, mem, mix_pre_g, w_in, q_norm_g, k_norm_g, w_attn_o, w_four_o, w_gate, b_gate, w_mix_o,
           mix_post_g, xa_pre_g, mem_norm_g, w_xq, w_xkv, w_xo, xa_post_g, ffn_pre_g, w_up, conv_w,
           conv_b, w_down, ffn_post_g):
    batch, seq, d = x.shape
    mem_tokens = mem.shape[1]
    depth = w_in.shape[0]
    q_width = w_attn_o.shape[1]
    f_width = w_four_o.shape[1]
    kv_width = (w_in.shape[2] - q_width - f_width) // 2
    n_heads = q_width // HEAD_DIM
    assert n_heads == N_KV * GROUP and kv_width == N_KV * HEAD_DIM
    assert seq % GRID_W == 0 and w_xq.shape[2] == X_HEADS * X_HEAD_DIM
    d_ff = w_down.shape[1]
    ts = _tiles(seq, d_ff)
    qk_width = q_width + kv_width
    u_col = q_width + 2 * kv_width
    tm = ts["mixer_tm"]
    assert (kv_width + f_width) % ts["uv_tn"] == 0 and w_gate.shape[2] % ts["gate_tn"] == 0

    cos_t, sin_t = _rope_tables(seq)
    assert seq % (2 * ts["merge_tm"]) == 0
    dft_c, dft_ch, dft_sh, dft_flip = _dft_tables(seq, f_width // N_FOURIER_GROUPS, ts["merge_tm"])
    order = _head_lane_order()
    partner = jnp.roll(jnp.arange(HEAD_DIM), HEAD_DIM // 2)
    bias_spec = pl.BlockSpec((1, ts["gate_tn"]), lambda i, j: (0, j))

    row = lambda v: v.reshape(1, -1)
    xf = x.reshape(batch * seq, d)
    memf = mem.reshape(batch * mem_tokens, d)

    quarter = HEAD_DIM // 4
    w_qk_t = w_in[:, :, :qk_width].astype(BF16).reshape(depth, d, qk_width // HEAD_DIM, 2, 2, quarter)
    w_qk_t = w_qk_t.transpose(0, 2, 4, 3, 5, 1).reshape(depth, qk_width, d)
    w_uv = jnp.concatenate([w_in[:, :, u_col:], w_in[:, :, qk_width:u_col]], axis=2).astype(BF16)
    w_gate_b, w_attn_o_b, w_four_o_b, w_mix_o_b, w_xq_b, w_xkv_b, w_xo_b, w_up_b, w_down_b = (
        w.astype(BF16) for w in (w_gate, w_attn_o, w_four_o, w_mix_o, w_xq, w_xkv, w_xo, w_up, w_down))

    for l in range(depth):
        q_gain = q_norm_g[l][order] * (HEAD_DIM ** -0.5 * math.log2(math.e))
        k_gain = k_norm_g[l][order]
        cg = jnp.concatenate([g[:, None] * cos_t.T for g in (q_gain, k_gain)], axis=0)
        sg = jnp.concatenate([g[partner][:, None] * sin_t.T for g in (q_gain, k_gain)], axis=0)
        g_pre = row(mix_pre_g[l])
        q, k_t = _proj_qk_t(xf, g_pre, w_qk_t, cg, sg, layer=l, seq=seq, n_q_heads=n_heads,
                            tm=ts["qk_tm"])
        q_bound = (-(HEAD_DIM ** 0.5) * jnp.max(jnp.abs(q_gain))) * (jnp.arange(HEAD_DIM) == 0)
        uv = _proj(xf, g_pre, w_uv, _epilogue_plain, (), (), layer=l, tm=tm, tn=ts["uv_tn"],
                   name="proj_uv")
        gates = _proj(xf, g_pre, w_gate_b, _epilogue_gate, (row(b_gate[l]),), (bias_spec,),
                      layer=l, tm=tm, tn=ts["gate_tn"], name="proj_gate")
        att = _attention(q, k_t, uv, row(q_bound.astype(F32)), batch=batch, seq=seq, q_width=q_width, v_col=f_width,
                         tq=ts["attn_tq"], kc=ts["attn_kc"])
        ucs, mid = _four_chan(uv, dft_c, dft_flip, batch=batch, seq=seq, f_width=f_width)
        four = _four_seq(dft_ch, dft_sh, dft_flip, ucs, mid, batch=batch, seq=seq, f_width=f_width)
        xf = _merge(att, four.reshape(batch * seq, f_width), gates, xf, w_attn_o_b, w_four_o_b,
                    w_mix_o_b, row(mix_post_g[l]), layer=l, seq=seq, tm=ts["merge_tm"])

        kv = _norm_matmul(memf, row(mem_norm_g[l]), w_xkv_b, layer=l, tm=mem_tokens)
        xf = _xattn(xf, row(xa_pre_g[l]), w_xq_b, kv, w_xo_b, row(xa_post_g[l]), layer=l,
                    batch=batch, seq=seq, mem_tokens=mem_tokens, tm=ts["xattn_tm"])

        xf = _ffn(xf, row(ffn_pre_g[l]), w_up_b, conv_w[l], row(conv_b[l]), w_down_b,
                  row(ffn_post_g[l]), layer=l, seq=seq, tm=ts["ffn_tm"], tf=ts["ffn_tf"])

    return xf.reshape(batch, seq, d)
```

```python
import functools
import math

import jax
import jax.numpy as jnp
from jax import lax
from jax.experimental import pallas as pl
from jax.experimental.pallas import tpu as pltpu

HEAD_DIM = 128
N_KV = 4
GROUP = 4
GRID_W = 64
ROPE_THETA = 10000.0
N_FOURIER_GROUPS = 4
X_HEADS = 4
X_HEAD_DIM = 128
EPS = 1e-6

V7X_VMEM_LIMIT_BYTES = 56 * 1024 * 1024
BF16_SUBLANES = 16
ATTN_MIN_DENOMINATOR = 2.0 ** -60
ATTN_SHIFT_MARGIN = 1.0 + 2.0 ** -6
FOUR_EXTRA_ROWS = BF16_SUBLANES
FFN_NORM_ROWS = 128

F32 = jnp.float32
BF16 = jnp.bfloat16


def _params(*sem):
    return pltpu.CompilerParams(dimension_semantics=sem, vmem_limit_bytes=V7X_VMEM_LIMIT_BYTES)


def _rms(xf, g):
    ms = jnp.mean(xf * xf, axis=-1, keepdims=True)
    return xf * lax.rsqrt(ms + EPS) * g


def _dot(a, b):
    return jnp.dot(a, b, preferred_element_type=F32)


def _dot_nt(a, b):
    return lax.dot_general(a, b, (((1,), (1,)), ((), ())), preferred_element_type=F32)


def _epilogue_plain(z, o_ref):
    o_ref[...] = z.astype(BF16)


def _epilogue_gate(z, b_ref, o_ref):
    o_ref[...] = (0.5 * jnp.tanh(0.5 * (z + b_ref[...])) + 0.5).astype(BF16)


def _proj_qk_t_kernel(x_ref, g_ref, wt_ref, cg_ref, sg_ref, q_ref, kt_ref, *, n_q_heads):
    hn = _rms(x_ref[...], g_ref[...]).astype(BF16)
    zt = _dot_nt(wt_ref[...], hn)
    half = HEAD_DIM // 2
    for h in range(zt.shape[0] // HEAD_DIM):
        tab = slice(0, HEAD_DIM) if h < n_q_heads else slice(HEAD_DIM, 2 * HEAD_DIM)
        y = zt[h * HEAD_DIM:(h + 1) * HEAD_DIM, :]
        y = y * lax.rsqrt(jnp.mean(y * y, axis=0, keepdims=True) + EPS)
        rot = jnp.concatenate([y[half:], y[:half]], axis=0)
        out = (y * cg_ref[tab, :] + rot * sg_ref[tab, :]).astype(BF16)
        if h < n_q_heads:
            q_ref[:, h * HEAD_DIM:(h + 1) * HEAD_DIM] = out.T
        else:
            kt_ref[(h - n_q_heads) * HEAD_DIM:(h - n_q_heads + 1) * HEAD_DIM, :] = out


def _proj_qk_t(x, g_pre, w_t, cg, sg, *, layer, seq, n_q_heads, tm):
    t, d = x.shape
    n = w_t.shape[1]
    seq_tiles = seq // tm
    table_spec = pl.BlockSpec((2 * HEAD_DIM, tm), lambda i: (0, i % seq_tiles))
    return pl.pallas_call(
        functools.partial(_proj_qk_t_kernel, n_q_heads=n_q_heads),
        grid=(t // tm,),
        in_specs=[
            pl.BlockSpec((tm, d), lambda i: (i, 0)),
            _resident(g_pre.shape),
            _resident_layer(w_t, layer),
            table_spec,
            table_spec,
        ],
        out_specs=[
            pl.BlockSpec((tm, n_q_heads * HEAD_DIM), lambda i: (i, 0)),
            pl.BlockSpec((n - n_q_heads * HEAD_DIM, tm), lambda i: (0, i)),
        ],
        out_shape=[
            jax.ShapeDtypeStruct((t, n_q_heads * HEAD_DIM), BF16),
            jax.ShapeDtypeStruct((n - n_q_heads * HEAD_DIM, t), BF16),
        ],
        compiler_params=_params("parallel"),
        name="proj_qk",
    )(x, g_pre, w_t, cg, sg)


def _proj_kernel(x_ref, g_ref, w_ref, *rest, epilogue):
    hn_ref = rest[-1]

    @pl.when(pl.program_id(1) == 0)
    def _():
        hn_ref[...] = _rms(x_ref[...], g_ref[...]).astype(BF16)

    epilogue(_dot(hn_ref[...], w_ref[...]), *rest[:-1])


def _proj(x, g_pre, w, epilogue, extra, extra_specs, *, layer, tm, tn, name):
    t, d = x.shape
    n = w.shape[2]
    return pl.pallas_call(
        functools.partial(_proj_kernel, epilogue=epilogue),
        grid=(t // tm, n // tn),
        in_specs=[
            pl.BlockSpec((tm, d), lambda i, j: (i, 0)),
            pl.BlockSpec((1, d), lambda i, j: (0, 0)),
            _resident_layer(w, layer) if tn == n else _layer_spec((d, tn), lambda i, j: (0, j), layer),
            *extra_specs,
        ],
        out_specs=pl.BlockSpec((tm, tn), lambda i, j: (i, j)),
        out_shape=jax.ShapeDtypeStruct((t, n), BF16),
        scratch_shapes=[pltpu.VMEM((tm, d), BF16)],
        compiler_params=_params("parallel", "arbitrary"),
        name=name,
    )(x, g_pre, w, *extra)


def _attn_exact(q, ka_ref, va_ref, kc):
    rows = q.shape[0]
    m = jnp.full((rows, 1), -jnp.inf, F32)
    l = jnp.zeros((rows, 1), F32)
    acc = jnp.zeros((rows, HEAD_DIM), F32)
    for c in range(ka_ref.shape[0] // kc):
        ks = slice(c * kc, (c + 1) * kc)
        s = _dot_nt(q, ka_ref[ks, :HEAD_DIM])
        m_new = jnp.maximum(m, jnp.max(s, axis=-1, keepdims=True))
        alpha = jnp.exp2(m - m_new)
        p = jnp.exp2(s - m_new)
        l = alpha * l + jnp.sum(p, axis=-1, keepdims=True)
        acc = alpha * acc + _dot(p.astype(BF16), va_ref[ks, :HEAD_DIM])
        m = m_new
    return acc * (1.0 / l)


def _attn_kernel(q_ref, kt_ref, v_ref, qb_ref, o_ref, ka_ref, va_ref, *, kc):
    tq = q_ref.shape[0]
    s_len = v_ref.shape[0]
    n_chunks = s_len // kc
    lane = lax.broadcasted_iota(jnp.int32, (1, HEAD_DIM), 1)

    @pl.when(pl.program_id(2) == 0)
    def _():
        lane0 = jnp.where(lane == 0, 1.0, 0.0)
        k2max = jnp.zeros((1, 1), F32)
        for c in range(n_chunks):
            kf = kt_ref[:, c * kc:(c + 1) * kc].astype(F32)
            k2max = jnp.maximum(k2max, jnp.max(jnp.sum(kf * kf, axis=0, keepdims=True),
                                               axis=1, keepdims=True))
        kmax_col = jnp.broadcast_to(jnp.sqrt(k2max) * ATTN_SHIFT_MARGIN * lane0, (kc, HEAD_DIM))
        ones_col = jnp.broadcast_to(lane0, (kc, HEAD_DIM)).astype(BF16)
        for c in range(n_chunks):
            ks = slice(c * kc, (c + 1) * kc)
            ka_ref[ks, :HEAD_DIM] = kt_ref[:, ks].T
            ka_ref[ks, HEAD_DIM:] = kmax_col.astype(BF16)
            va_ref[ks, :HEAD_DIM] = v_ref[ks, :]
            va_ref[ks, HEAD_DIM:] = ones_col

    bound_col = jnp.broadcast_to(qb_ref[...], (tq, HEAD_DIM)).astype(BF16)
    q_aug = jnp.concatenate(
        [jnp.concatenate([q_ref[:, g * HEAD_DIM:(g + 1) * HEAD_DIM], bound_col], axis=1)
         for g in range(GROUP)], axis=0)
    acc = jnp.zeros((GROUP * tq, 2 * HEAD_DIM), F32)
    for c in range(n_chunks):
        ks = slice(c * kc, (c + 1) * kc)
        p = jnp.exp2(_dot_nt(q_aug, ka_ref[ks, :])).astype(BF16)
        acc = acc + _dot(p, va_ref[ks, :])
    l = acc[:, HEAD_DIM:HEAD_DIM + 1]
    accurate = jnp.min(l) >= ATTN_MIN_DENOMINATOR

    def write(o):
        for g in range(GROUP):
            o_ref[:, g * HEAD_DIM:(g + 1) * HEAD_DIM] = o[g * tq:(g + 1) * tq].astype(BF16)

    write(acc[:, :HEAD_DIM] * (1.0 / l))

    @pl.when(jnp.logical_not(accurate))
    def _():
        write(_attn_exact(q_aug[:, :HEAD_DIM], ka_ref, va_ref, kc))


def _attention(q, k_t, vu, q_bound, *, batch, seq, q_width, v_col, tq, kc):
    t = vu.shape[0]
    gw = GROUP * HEAD_DIM
    v_blk0 = v_col // HEAD_DIM
    nq = seq // tq
    return pl.pallas_call(
        functools.partial(_attn_kernel, kc=kc),
        grid=(batch, N_KV, nq),
        in_specs=[
            pl.BlockSpec((tq, gw), lambda b, h, i: (b * nq + i, h)),
            pl.BlockSpec((HEAD_DIM, seq), lambda b, h, i: (h, b)),
            pl.BlockSpec((seq, HEAD_DIM), lambda b, h, i: (b, v_blk0 + h)),
            _resident(q_bound.shape),
        ],
        out_specs=pl.BlockSpec((tq, gw), lambda b, h, i: (b * nq + i, h)),
        out_shape=jax.ShapeDtypeStruct((t, q_width), BF16),
        scratch_shapes=[
            pltpu.VMEM((seq, 2 * HEAD_DIM), BF16),
            pltpu.VMEM((seq, 2 * HEAD_DIM), BF16),
        ],
        compiler_params=_params("parallel", "parallel", "arbitrary"),
        name="attention",
    )(q, k_t, vu, q_bound)


def _four_chan_kernel(u_ref, p_ref, e_ref, m_ref, dft_ref, flip_ref, o_ref, mid_ref, *, gdim):
    i = pl.program_id(1)
    ext = jnp.where(i == 0, 0.0, e_ref[...].astype(F32)).astype(BF16)
    partner = _dot(flip_ref[...], jnp.concatenate([p_ref[...], ext], axis=0))
    u = u_ref[...].astype(F32)
    even = (u + partner).astype(BF16)
    odd = (u - partner).astype(BF16)
    for g in range(u_ref.shape[1] // gdim):
        sl = slice(g * gdim, (g + 1) * gdim)
        o_ref[0, 0, :, sl] = _dot(even[:, sl], dft_ref[:, :gdim]).astype(BF16)
        o_ref[0, 1, :, sl] = _dot(odd[:, sl], dft_ref[:, gdim:]).astype(BF16)

    @pl.when(i == 0)
    def _():
        for g in range(u_ref.shape[1] // gdim):
            sl = slice(g * gdim, (g + 1) * gdim)
            mid_ref[0, :, sl] = _dot(m_ref[:, sl], dft_ref[:, :gdim])


def _four_chan(uv, dft_c, flip, *, batch, seq, f_width):
    gdim = f_width // N_FOURIER_GROUPS
    fr = flip.shape[0]
    nb = seq // fr
    ext = FOUR_EXTRA_ROWS
    last_ext = uv.shape[0] // ext - 1
    return pl.pallas_call(
        functools.partial(_four_chan_kernel, gdim=gdim),
        grid=(batch, nb // 2),
        in_specs=[
            pl.BlockSpec((fr, f_width), lambda b, i: (b * nb + i, 0)),
            pl.BlockSpec((fr, f_width), lambda b, i: (b * nb + nb - 1 - i, 0)),
            pl.BlockSpec((ext, f_width),
                         lambda b, i: (jnp.minimum(((b + 1) * nb - i) * (fr // ext), last_ext), 0)),
            pl.BlockSpec((ext, f_width), lambda b, i: ((2 * b + 1) * (seq // 2 // ext), 0)),
            _resident(dft_c.shape),
            _resident(flip.shape),
        ],
        out_specs=[
            pl.BlockSpec((1, 2, fr, f_width), lambda b, i: (b, 0, i, 0)),
            pl.BlockSpec((1, ext, f_width), lambda b, i: (b, 0, 0)),
        ],
        out_shape=[
            jax.ShapeDtypeStruct((batch, 2, seq // 2, f_width), BF16),
            jax.ShapeDtypeStruct((batch, ext, f_width), F32),
        ],
        compiler_params=_params("parallel", "arbitrary"),
        name="fourier_chan",
    )(uv, uv, uv, uv, dft_c, flip)


def _four_seq_kernel(ch_ref, sh_ref, uc_ref, us_ref, mid_ref, flip_ref, o_ref, *, seq):
    fr = o_ref.shape[3]
    rows = ch_ref.shape[1]
    p = _dot(ch_ref[0], uc_ref[0, 0])
    q = _dot(sh_ref[0], us_ref[0, 0])
    k = pl.program_id(1) * fr + lax.broadcasted_iota(jnp.int32, (rows, 1), 0)
    p = p + jnp.where(k % 2 == 0, seq ** -0.5, -(seq ** -0.5)) * mid_ref[0, 0:1, :]
    o_ref[0, 0, 0] = (p - q)[:fr].astype(BF16)
    o_ref[0, 1, 0] = _dot(flip_ref[...], (p + q).astype(BF16)).astype(BF16)


def _four_seq(ch, sh, flip, ucs, mid, *, batch, seq, f_width):
    nt, rows, half = ch.shape
    fr = flip.shape[0]
    table_spec = pl.BlockSpec((1, rows, half), lambda b, t: (t, 0, 0))
    return pl.pallas_call(
        functools.partial(_four_seq_kernel, seq=seq),
        grid=(batch, nt),
        in_specs=[
            table_spec,
            table_spec,
            pl.BlockSpec((1, 1, half, f_width), lambda b, t: (b, 0, 0, 0)),
            pl.BlockSpec((1, 1, half, f_width), lambda b, t: (b, 1, 0, 0)),
            pl.BlockSpec((1, FOUR_EXTRA_ROWS, f_width), lambda b, t: (b, 0, 0)),
            _resident(flip.shape),
        ],
        out_specs=pl.BlockSpec((1, 2, 1, fr, f_width), lambda b, t: (b, 0, t, 0, 0)),
        out_shape=jax.ShapeDtypeStruct((batch, 2, nt, fr, f_width), BF16),
        compiler_params=_params("parallel", "arbitrary"),
        name="fourier_seq",
    )(ch, sh, ucs, ucs, mid, flip)


def _merge_kernel(att_ref, y_ref, g1_ref, g2_ref, x_ref, wa_ref, wf_ref, wm_ref, gp_ref, o_ref):
    a = _dot(att_ref[...], wa_ref[...])
    f = _dot(y_ref[...], wf_ref[...])
    merged = g1_ref[...].astype(F32) * a + g2_ref[...].astype(F32) * f
    y = _dot(merged.astype(BF16), wm_ref[...])
    o_ref[...] = x_ref[...] + _rms(y, gp_ref[...])


def _resident(shape):
    return pl.BlockSpec(shape, lambda *_: (0,) * len(shape), pipeline_mode=pl.Buffered(1))


def _layer_spec(block, index_map, layer, **kwargs):
    return pl.BlockSpec((None, *block), lambda *idx: (layer, *index_map(*idx)), **kwargs)


def _resident_layer(w, layer):
    return _layer_spec(w.shape[1:], lambda *_: (0,) * (w.ndim - 1), layer,
                       pipeline_mode=pl.Buffered(1))


def _merge(att, four, gates, x, wa, wf, wm, g_post, *, layer, seq, tm):
    t, d = x.shape
    tiles = seq // tm

    def four_block(i):
        it = i % tiles
        return (i - it + jnp.where(it < tiles // 2, it, tiles + tiles // 2 - 1 - it), 0)

    return pl.pallas_call(
        _merge_kernel,
        grid=(t // tm,),
        in_specs=[
            pl.BlockSpec((tm, att.shape[1]), lambda i: (i, 0)),
            pl.BlockSpec((tm, four.shape[1]), four_block),
            pl.BlockSpec((tm, d), lambda i: (i, 0)),
            pl.BlockSpec((tm, d), lambda i: (i, 1)),
            pl.BlockSpec((tm, d), lambda i: (i, 0)),
            _resident_layer(wa, layer),
            _resident_layer(wf, layer),
            _resident_layer(wm, layer),
            _resident(g_post.shape),
        ],
        out_specs=pl.BlockSpec((tm, d), lambda i: (i, 0)),
        out_shape=jax.ShapeDtypeStruct((t, d), F32),
        compiler_params=_params("parallel"),
        name="merge",
    )(att, four, gates, gates, x, wa, wf, wm, g_post)


def _norm_matmul_kernel(x_ref, g_ref, w_ref, o_ref):
    hn = _rms(x_ref[...], g_ref[...]).astype(BF16)
    o_ref[...] = _dot(hn, w_ref[...]).astype(BF16)


def _norm_matmul(x, g, w, *, layer, tm):
    t, d = x.shape
    n = w.shape[2]
    return pl.pallas_call(
        _norm_matmul_kernel,
        grid=(t // tm,),
        in_specs=[
            pl.BlockSpec((tm, d), lambda i: (i, 0)),
            _resident(g.shape),
            _resident_layer(w, layer),
        ],
        out_specs=pl.BlockSpec((tm, n), lambda i: (i, 0)),
        out_shape=jax.ShapeDtypeStruct((t, n), BF16),
        compiler_params=_params("parallel"),
        name="mem_kv",
    )(x, g, w)


def _xattn_kernel(x_ref, gpre_ref, wq_ref, k_ref, v_ref, wo_ref, gpost_ref, o_ref):
    x = x_ref[...]
    hn = _rms(x, gpre_ref[...]).astype(BF16)
    q = (_dot(hn, wq_ref[...]) * (X_HEAD_DIM ** -0.5)).astype(BF16)
    outs = []
    for h in range(X_HEADS):
        sl = slice(h * X_HEAD_DIM, (h + 1) * X_HEAD_DIM)
        s = _dot_nt(q[:, sl], k_ref[:, sl])
        m = jnp.max(s, axis=-1, keepdims=True)
        p = jnp.exp(s - m)
        l = jnp.sum(p, axis=-1, keepdims=True)
        outs.append(_dot(p.astype(BF16), v_ref[:, sl]) * (1.0 / l))
    o = jnp.concatenate(outs, axis=1).astype(BF16)
    y = _dot(o, wo_ref[...])
    o_ref[...] = x + _rms(y, gpost_ref[...])


def _xattn(x, g_pre, wq, kv, wo, g_post, *, layer, batch, seq, mem_tokens, tm):
    t, d = x.shape
    xw = wq.shape[2]
    ns = seq // tm
    return pl.pallas_call(
        _xattn_kernel,
        grid=(batch, ns),
        in_specs=[
            pl.BlockSpec((tm, d), lambda b, i: (b * ns + i, 0)),
            _resident(g_pre.shape),
            _resident_layer(wq, layer),
            pl.BlockSpec((mem_tokens, xw), lambda b, i: (b, 0)),
            pl.BlockSpec((mem_tokens, xw), lambda b, i: (b, 1)),
            _resident_layer(wo, layer),
            _resident(g_post.shape),
        ],
        out_specs=pl.BlockSpec((tm, d), lambda b, i: (b * ns + i, 0)),
        out_shape=jax.ShapeDtypeStruct((t, d), F32),
        compiler_params=_params("parallel", "parallel"),
        name="xattn",
    )(x, g_pre, wq, kv, kv, wo, g_post)


def _gelu_tanh(x):
    return 0.5 * x * (1.0 + jnp.tanh(math.sqrt(2.0 / math.pi) * (x + 0.044715 * (x * x * x))))


def _ffn_kernel(xp_ref, x_ref, xn_ref, gpre_ref, wg_ref, wv_ref, cwg_ref, cwv_ref, cbg_ref, cbv_ref,
                wd_ref, gpost_ref, o_ref, hn_ref, ug_ref, uv_ref, *, tiles_per_seq):
    i = pl.program_id(0)
    j = pl.program_id(1)
    tm = x_ref.shape[0]
    halo = xp_ref.shape[0]

    @pl.when(j == 0)
    def _():
        g = gpre_ref[...]
        pos = i % tiles_per_seq
        hp = jnp.where(pos == 0, 0.0, _rms(xp_ref[...], g))
        hx = jnp.where(pos == tiles_per_seq - 1, 0.0, _rms(xn_ref[...], g))
        hn_ref[0:halo, :] = hp.astype(BF16)
        hn_ref[halo + tm:, :] = hx.astype(BF16)
        for r in range(0, tm, FFN_NORM_ROWS):
            rows = slice(r, r + FFN_NORM_ROWS)
            hn_ref[halo + r:halo + r + FFN_NORM_ROWS, :] = _rms(x_ref[rows, :], g).astype(BF16)
        o_ref[...] = jnp.zeros_like(o_ref)

    hn = hn_ref[...]
    ug_ref[...] = _dot(hn, wg_ref[...])
    uv_ref[...] = _dot(hn, wv_ref[...])

    def conv(u_ref, cw_ref, cb_ref):
        return (u_ref[halo - 1:halo - 1 + tm, :] * cw_ref[0:1, :]
                + u_ref[halo:halo + tm, :] * cw_ref[1:2, :]
                + u_ref[halo + 1:halo + 1 + tm, :] * cw_ref[2:3, :]
                + cb_ref[...])

    act = _gelu_tanh(conv(ug_ref, cwg_ref, cbg_ref)) * conv(uv_ref, cwv_ref, cbv_ref)
    o_ref[...] += _dot(act.astype(BF16), wd_ref[...])

    @pl.when(j == pl.num_programs(1) - 1)
    def _():
        for r in range(0, tm, FFN_NORM_ROWS):
            rows = slice(r, r + FFN_NORM_ROWS)
            o_ref[rows, :] = x_ref[rows, :] + _rms(o_ref[rows, :], gpost_ref[...])


def _ffn(x, g_pre, w_up, conv_w, conv_b, w_down, g_post, *, layer, seq, tm, tf):
    t, d = x.shape
    d_ff = w_down.shape[1]
    nf = d_ff // tf
    halo = BF16_SUBLANES
    hb = tm // halo
    n_hblk = t // halo
    kern = functools.partial(_ffn_kernel, tiles_per_seq=seq // tm)
    return pl.pallas_call(
        kern,
        grid=(t // tm, nf),
        in_specs=[
            pl.BlockSpec((halo, d), lambda i, j: (jnp.maximum(i * hb - 1, 0), 0)),
            pl.BlockSpec((tm, d), lambda i, j: (i, 0)),
            pl.BlockSpec((halo, d), lambda i, j: (jnp.minimum((i + 1) * hb, n_hblk - 1), 0)),
            pl.BlockSpec((1, d), lambda i, j: (0, 0)),
            _layer_spec((d, tf), lambda i, j: (0, j), layer),
            _layer_spec((d, tf), lambda i, j: (0, nf + j), layer),
            pl.BlockSpec((3, tf), lambda i, j: (0, j)),
            pl.BlockSpec((3, tf), lambda i, j: (0, nf + j)),
            pl.BlockSpec((1, tf), lambda i, j: (0, j)),
            pl.BlockSpec((1, tf), lambda i, j: (0, nf + j)),
            _layer_spec((tf, d), lambda i, j: (j, 0), layer),
            pl.BlockSpec((1, d), lambda i, j: (0, 0)),
        ],
        out_specs=pl.BlockSpec((tm, d), lambda i, j: (i, 0)),
        out_shape=jax.ShapeDtypeStruct((t, d), F32),
        scratch_shapes=[
            pltpu.VMEM((tm + 2 * halo, d), BF16),
            pltpu.VMEM((tm + 2 * halo, tf), F32),
            pltpu.VMEM((tm + 2 * halo, tf), F32),
        ],
        compiler_params=_params("parallel", "arbitrary"),
        name="conv_ffn",
    )(x, x, x, g_pre, w_up, w_up, conv_w, conv_w, conv_b, conv_b, w_down, g_post)


def _head_lane_order():
    quarter = HEAD_DIM // 4
    blocks = (0, 2, 1, 3)
    return jnp.concatenate([jnp.arange(b * quarter, (b + 1) * quarter) for b in blocks])


def _rope_tables(seq):
    half = HEAD_DIM // 2
    pos = jnp.arange(seq)
    row = (pos // GRID_W).astype(F32)
    col = (pos % GRID_W).astype(F32)
    inv_freq = 1.0 / (ROPE_THETA ** (jnp.arange(0, half, 2, dtype=F32) / half))
    lane = _head_lane_order()
    freq = inv_freq[(lane % half) % (half // 2)]
    ids = jnp.where(lane[None, :] < half, row[:, None], col[:, None])
    ang = ids * freq[None, :]
    sign = jnp.where((lane % half) < half // 2, -1.0, 1.0)
    return jnp.cos(ang), jnp.sin(ang) * sign[None, :]


def _dft_tables(seq, gdim, fr):
    def cs(k, n, period):
        r = (k[..., None] * n) % period
        ang = r.astype(F32) * (2.0 * math.pi / period)
        return jnp.cos(ang), jnp.sin(ang)
    c = jnp.arange(gdim, dtype=jnp.int32)
    cc, sc = (t * gdim ** -0.5 for t in cs(c, c, gdim))
    rows = fr + FOUR_EXTRA_ROWS
    k = jnp.arange(seq // 2 // fr, dtype=jnp.int32)[:, None] * fr + jnp.arange(rows, dtype=jnp.int32)
    ca, sa = cs(k, GRID_W * jnp.arange(seq // 2 // GRID_W, dtype=jnp.int32), seq)
    cb, sb = cs(k, jnp.arange(GRID_W, dtype=jnp.int32), seq)
    scale = seq ** -0.5
    ch = ((ca[..., :, None] * cb[..., None, :] - sa[..., :, None] * sb[..., None, :]) * scale)
    sh = ((sa[..., :, None] * cb[..., None, :] + ca[..., :, None] * sb[..., None, :]) * scale)
    ch, sh = ch.reshape(*k.shape, seq // 2), sh.reshape(*k.shape, seq // 2)
    flip = (jnp.arange(rows)[None, :] == fr - jnp.arange(fr)[:, None]).astype(BF16)
    return (jnp.concatenate([cc, sc], axis=1).astype(BF16), ch.astype(BF16), sh.astype(BF16), flip)


def _tiles(seq, d_ff):
    def pick(n, pref):
        return pref if n % pref == 0 else n
    tf = next(c for c in (512, 256, 128) if d_ff % c == 0)
    return dict(
        qk_tm=pick(seq, 512), mixer_tm=pick(seq, 512), uv_tn=1536, gate_tn=4096,
        attn_tq=pick(seq, 1024), attn_kc=pick(seq, 256),
        merge_tm=pick(seq, 256),
        xattn_tm=pick(seq, 512),
        ffn_tm=pick(seq, 512), ffn_tf=tf,
    )


def kernel(x, mem, mix_pre_g, w_in, q_norm_g, k_norm_g, w_attn_o, w_four_o, w_gate, b_gate, w_mix_o,
           mix_post_g, xa_pre_g, mem_norm_g, w_xq, w_xkv, w_xo, xa_post_g, ffn_pre_g, w_up, conv_w,
           conv_b, w_down, ffn_post_g):
    batch, seq, d = x.shape
    mem_tokens = mem.shape[1]
    depth = w_in.shape[0]
    q_width = w_attn_o.shape[1]
    f_width = w_four_o.shape[1]
    kv_width = (w_in.shape[2] - q_width - f_width) // 2
    n_heads = q_width // HEAD_DIM
    assert n_heads == N_KV * GROUP and kv_width == N_KV * HEAD_DIM
    assert seq % GRID_W == 0 and w_xq.shape[2] == X_HEADS * X_HEAD_DIM
    d_ff = w_down.shape[1]
    ts = _tiles(seq, d_ff)
    qk_width = q_width + kv_width
    u_col = q_width + 2 * kv_width
    tm = ts["mixer_tm"]
    assert (kv_width + f_width) % ts["uv_tn"] == 0 and w_gate.shape[2] % ts["gate_tn"] == 0

    cos_t, sin_t = _rope_tables(seq)
    assert seq % (2 * ts["merge_tm"]) == 0
    dft_c, dft_ch, dft_sh, dft_flip = _dft_tables(seq, f_width // N_FOURIER_GROUPS, ts["merge_tm"])
    order = _head_lane_order()
    partner = jnp.roll(jnp.arange(HEAD_DIM), HEAD_DIM // 2)
    bias_spec = pl.BlockSpec((1, ts["gate_tn"]), lambda i, j: (0, j))

    row = lambda v: v.reshape(1, -1)
    xf = x.reshape(batch * seq, d)
    memf = mem.reshape(batch * mem_tokens, d)

    quarter = HEAD_DIM // 4
    w_qk_t = w_in[:, :, :qk_width].astype(BF16).reshape(depth, d, qk_width // HEAD_DIM, 2, 2, quarter)
    w_qk_t = w_qk_t.transpose(0, 2, 4, 3, 5, 1).reshape(depth, qk_width, d)
    w_uv = jnp.concatenate([w_in[:, :, u_col:], w_in[:, :, qk_width:u_col]], axis=2).astype(BF16)
    w_gate_b, w_attn_o_b, w_four_o_b, w_mix_o_b, w_xq_b, w_xkv_b, w_xo_b, w_up_b, w_down_b = (
        w.astype(BF16) for w in (w_gate, w_attn_o, w_four_o, w_mix_o, w_xq, w_xkv, w_xo, w_up, w_down))

    for l in range(depth):
        q_gain = q_norm_g[l][order] * (HEAD_DIM ** -0.5 * math.log2(math.e))
        k_gain = k_norm_g[l][order]
        cg = jnp.concatenate([g[:, None] * cos_t.T for g in (q_gain, k_gain)], axis=0)
        sg = jnp.concatenate([g[partner][:, None] * sin_t.T for g in (q_gain, k_gain)], axis=0)
        g_pre = row(mix_pre_g[l])
        q, k_t = _proj_qk_t(xf, g_pre, w_qk_t, cg, sg, layer=l, seq=seq, n_q_heads=n_heads,
                            tm=ts["qk_tm"])
        q_bound = (-(HEAD_DIM ** 0.5) * jnp.max(jnp.abs(q_gain))) * (jnp.arange(HEAD_DIM) == 0)
        uv = _proj(xf, g_pre, w_uv, _epilogue_plain, (), (), layer=l, tm=tm, tn=ts["uv_tn"],
                   name="proj_uv")
        gates = _proj(xf, g_pre, w_gate_b, _epilogue_gate, (row(b_gate[l]),), (bias_spec,),
                      layer=l, tm=tm, tn=ts["gate_tn"], name="proj_gate")
        att = _attention(q, k_t, uv, row(q_bound.astype(F32)), batch=batch, seq=seq, q_width=q_width, v_col=f_width,
                         tq=ts["attn_tq"], kc=ts["attn_kc"])
        ucs, mid = _four_chan(uv, dft_c, dft_flip, batch=batch, seq=seq, f_width=f_width)
        four = _four_seq(dft_ch, dft_sh, dft_flip, ucs, mid, batch=batch, seq=seq, f_width=f_width)
        xf = _merge(att, four.reshape(batch * seq, f_width), gates, xf, w_attn_o_b, w_four_o_b,
                    w_mix_o_b, row(mix_post_g[l]), layer=l, seq=seq, tm=ts["merge_tm"])

        kv = _norm_matmul(memf, row(mem_norm_g[l]), w_xkv_b, layer=l, tm=mem_tokens)
        xf = _xattn(xf, row(xa_pre_g[l]), w_xq_b, kv, w_xo_b, row(xa_post_g[l]), layer=l,
                    batch=batch, seq=seq, mem_tokens=mem_tokens, tm=ts["xattn_tm"])

        xf = _ffn(xf, row(ffn_pre_g[l]), w_up_b, conv_w[l], row(conv_b[l]), w_down_b,
                  row(ffn_post_g[l]), layer=l, seq=seq, tm=ts["ffn_tm"], tf=ts["ffn_tf"])

    return xf.reshape(batch, seq, d)
```

```python
import functools
import math

import jax
import jax.numpy as jnp
from jax import lax
from jax.experimental import pallas as pl
from jax.experimental.pallas import tpu as pltpu

HEAD_DIM = 128
N_KV = 4
GROUP = 4
GRID_W = 64
ROPE_THETA = 10000.0
N_FOURIER_GROUPS = 4
X_HEADS = 4
X_HEAD_DIM = 128
EPS = 1e-6

V7X_VMEM_LIMIT_BYTES = 56 * 1024 * 1024
BF16_SUBLANES = 16
ATTN_MIN_DENOMINATOR = 2.0 ** -60
ATTN_SHIFT_MARGIN = 1.0 + 2.0 ** -6
FOUR_EXTRA_ROWS = BF16_SUBLANES
FFN_NORM_ROWS = 128

F32 = jnp.float32
BF16 = jnp.bfloat16


def _params(*sem):
    return pltpu.CompilerParams(dimension_semantics=sem, vmem_limit_bytes=V7X_VMEM_LIMIT_BYTES)


def _rms(xf, g):
    ms = jnp.mean(xf * xf, axis=-1, keepdims=True)
    return xf * lax.rsqrt(ms + EPS) * g


def _dot(a, b):
    return jnp.dot(a, b, preferred_element_type=F32)


def _dot_nt(a, b):
    return lax.dot_general(a, b, (((1,), (1,)), ((), ())), preferred_element_type=F32)


def _epilogue_plain(z, o_ref):
    o_ref[...] = z.astype(BF16)


def _epilogue_gate(z, b_ref, o_ref):
    o_ref[...] = (0.5 * jnp.tanh(0.5 * (z + b_ref[...])) + 0.5).astype(BF16)


def _proj_qk_t_kernel(x_ref, g_ref, wt_ref, cg_ref, sg_ref, q_ref, kt_ref, *, n_q_heads):
    hn = _rms(x_ref[...], g_ref[...]).astype(BF16)
    zt = _dot_nt(wt_ref[...], hn)
    half = HEAD_DIM // 2
    for h in range(zt.shape[0] // HEAD_DIM):
        tab = slice(0, HEAD_DIM) if h < n_q_heads else slice(HEAD_DIM, 2 * HEAD_DIM)
        y = zt[h * HEAD_DIM:(h + 1) * HEAD_DIM, :]
        y = y * lax.rsqrt(jnp.mean(y * y, axis=0, keepdims=True) + EPS)
        rot = jnp.concatenate([y[half:], y[:half]], axis=0)
        out = (y * cg_ref[tab, :] + rot * sg_ref[tab, :]).astype(BF16)
        if h < n_q_heads:
            q_ref[:, h * HEAD_DIM:(h + 1) * HEAD_DIM] = out.T
        else:
            kt_ref[(h - n_q_heads) * HEAD_DIM:(h - n_q_heads + 1) * HEAD_DIM, :] = out


def _proj_qk_t(x, g_pre, w_t, cg, sg, *, layer, seq, n_q_heads, tm):
    t, d = x.shape
    n = w_t.shape[1]
    seq_tiles = seq // tm
    table_spec = pl.BlockSpec((2 * HEAD_DIM, tm), lambda i: (0, i % seq_tiles))
    return pl.pallas_call(
        functools.partial(_proj_qk_t_kernel, n_q_heads=n_q_heads),
        grid=(t // tm,),
        in_specs=[
            pl.BlockSpec((tm, d), lambda i: (i, 0)),
            _resident(g_pre.shape),
            _resident_layer(w_t, layer),
            table_spec,
            table_spec,
        ],
        out_specs=[
            pl.BlockSpec((tm, n_q_heads * HEAD_DIM), lambda i: (i, 0)),
            pl.BlockSpec((n - n_q_heads * HEAD_DIM, tm), lambda i: (0, i)),
        ],
        out_shape=[
            jax.ShapeDtypeStruct((t, n_q_heads * HEAD_DIM), BF16),
            jax.ShapeDtypeStruct((n - n_q_heads * HEAD_DIM, t), BF16),
        ],
        compiler_params=_params("parallel"),
        name="proj_qk",
    )(x, g_pre, w_t, cg, sg)


def _proj_kernel(x_ref, g_ref, w_ref, *rest, epilogue):
    hn_ref = rest[-1]

    @pl.when(pl.program_id(1) == 0)
    def _():
        hn_ref[...] = _rms(x_ref[...], g_ref[...]).astype(BF16)

    epilogue(_dot(hn_ref[...], w_ref[...]), *rest[:-1])


def _proj(x, g_pre, w, epilogue, extra, extra_specs, *, layer, tm, tn, name):
    t, d = x.shape
    n = w.shape[2]
    return pl.pallas_call(
        functools.partial(_proj_kernel, epilogue=epilogue),
        grid=(t // tm, n // tn),
        in_specs=[
            pl.BlockSpec((tm, d), lambda i, j: (i, 0)),
            pl.BlockSpec((1, d), lambda i, j: (0, 0)),
            _resident_layer(w, layer) if tn == n else _layer_spec((d, tn), lambda i, j: (0, j), layer),
            *extra_specs,
        ],
        out_specs=pl.BlockSpec((tm, tn), lambda i, j: (i, j)),
        out_shape=jax.ShapeDtypeStruct((t, n), BF16),
        scratch_shapes=[pltpu.VMEM((tm, d), BF16)],
        compiler_params=_params("parallel", "arbitrary"),
        name=name,
    )(x, g_pre, w, *extra)


def _attn_exact(q, ka_ref, va_ref, kc):
    rows = q.shape[0]
    m = jnp.full((rows, 1), -jnp.inf, F32)
    l = jnp.zeros((rows, 1), F32)
    acc = jnp.zeros((rows, HEAD_DIM), F32)
    for c in range(ka_ref.shape[0] // kc):
        ks = slice(c * kc, (c + 1) * kc)
        s = _dot_nt(q, ka_ref[ks, :HEAD_DIM])
        m_new = jnp.maximum(m, jnp.max(s, axis=-1, keepdims=True))
        alpha = jnp.exp2(m - m_new)
        p = jnp.exp2(s - m_new)
        l = alpha * l + jnp.sum(p, axis=-1, keepdims=True)
        acc = alpha * acc + _dot(p.astype(BF16), va_ref[ks, :HEAD_DIM])
        m = m_new
    return acc * (1.0 / l)


def _attn_kernel(q_ref, kt_ref, v_ref, qb_ref, o_ref, ka_ref, va_ref, *, kc):
    tq = q_ref.shape[0]
    s_len = v_ref.shape[0]
    n_chunks = s_len // kc
    lane = lax.broadcasted_iota(jnp.int32, (1, HEAD_DIM), 1)

    @pl.when(pl.program_id(2) == 0)
    def _():
        lane0 = jnp.where(lane == 0, 1.0, 0.0)
        k2max = jnp.zeros((1, 1), F32)
        for c in range(n_chunks):
            kf = kt_ref[:, c * kc:(c + 1) * kc].astype(F32)
            k2max = jnp.maximum(k2max, jnp.max(jnp.sum(kf * kf, axis=0, keepdims=True),
                                               axis=1, keepdims=True))
        kmax_col = jnp.broadcast_to(jnp.sqrt(k2max) * ATTN_SHIFT_MARGIN * lane0, (kc, HEAD_DIM))
        ones_col = jnp.broadcast_to(lane0, (kc, HEAD_DIM)).astype(BF16)
        for c in range(n_chunks):
            ks = slice(c * kc, (c + 1) * kc)
            ka_ref[ks, :HEAD_DIM] = kt_ref[:, ks].T
            ka_ref[ks, HEAD_DIM:] = kmax_col.astype(BF16)
            va_ref[ks, :HEAD_DIM] = v_ref[ks, :]
            va_ref[ks, HEAD_DIM:] = ones_col

    bound_col = jnp.broadcast_to(qb_ref[...], (tq, HEAD_DIM)).astype(BF16)
    q_aug = jnp.concatenate(
        [jnp.concatenate([q_ref[:, g * HEAD_DIM:(g + 1) * HEAD_DIM], bound_col], axis=1)
         for g in range(GROUP)], axis=0)
    acc = jnp.zeros((GROUP * tq, 2 * HEAD_DIM), F32)
    for c in range(n_chunks):
        ks = slice(c * kc, (c + 1) * kc)
        p = jnp.exp2(_dot_nt(q_aug, ka_ref[ks, :])).astype(BF16)
        acc = acc + _dot(p, va_ref[ks, :])
    l = acc[:, HEAD_DIM:HEAD_DIM + 1]
    accurate = jnp.min(l) >= ATTN_MIN_DENOMINATOR

    def write(o):
        for g in range(GROUP):
            o_ref[:, g * HEAD_DIM:(g + 1) * HEAD_DIM] = o[g * tq:(g + 1) * tq].astype(BF16)

    write(acc[:, :HEAD_DIM] * (1.0 / l))

    @pl.when(jnp.logical_not(accurate))
    def _():
        write(_attn_exact(q_aug[:, :HEAD_DIM], ka_ref, va_ref, kc))


def _attention(q, k_t, vu, q_bound, *, batch, seq, q_width, v_col, tq, kc):
    t = vu.shape[0]
    gw = GROUP * HEAD_DIM
    v_blk0 = v_col // HEAD_DIM
    nq = seq // tq
    return pl.pallas_call(
        functools.partial(_attn_kernel, kc=kc),
        grid=(batch, N_KV, nq),
        in_specs=[
            pl.BlockSpec((tq, gw), lambda b, h, i: (b * nq + i, h)),
            pl.BlockSpec((HEAD_DIM, seq), lambda b, h, i: (h, b)),
            pl.BlockSpec((seq, HEAD_DIM), lambda b, h, i: (b, v_blk0 + h)),
            _resident(q_bound.shape),
        ],
        out_specs=pl.BlockSpec((tq, gw), lambda b, h, i: (b * nq + i, h)),
        out_shape=jax.ShapeDtypeStruct((t, q_width), BF16),
        scratch_shapes=[
            pltpu.VMEM((seq, 2 * HEAD_DIM), BF16),
            pltpu.VMEM((seq, 2 * HEAD_DIM), BF16),
        ],
        compiler_params=_params("parallel", "parallel", "arbitrary"),
        name="attention",
    )(q, k_t, vu, q_bound)


def _four_chan_kernel(u_ref, p_ref, e_ref, m_ref, dft_ref, flip_ref, o_ref, mid_ref, *, gdim):
    i = pl.program_id(1)
    ext = jnp.where(i == 0, 0.0, e_ref[...].astype(F32)).astype(BF16)
    partner = _dot(flip_ref[...], jnp.concatenate([p_ref[...], ext], axis=0))
    u = u_ref[...].astype(F32)
    even = (u + partner).astype(BF16)
    odd = (u - partner).astype(BF16)
    for g in range(u_ref.shape[1] // gdim):
        sl = slice(g * gdim, (g + 1) * gdim)
        o_ref[0, 0, :, sl] = _dot(even[:, sl], dft_ref[:, :gdim]).astype(BF16)
        o_ref[0, 1, :, sl] = _dot(odd[:, sl], dft_ref[:, gdim:]).astype(BF16)

    @pl.when(i == 0)
    def _():
        for g in range(u_ref.shape[1] // gdim):
            sl = slice(g * gdim, (g + 1) * gdim)
            mid_ref[0, :, sl] = _dot(m_ref[:, sl], dft_ref[:, :gdim])


def _four_chan(uv, dft_c, flip, *, batch, seq, f_width):
    gdim = f_width // N_FOURIER_GROUPS
    fr = flip.shape[0]
    nb = seq // fr
    ext = FOUR_EXTRA_ROWS
    last_ext = uv.shape[0] // ext - 1
    return pl.pallas_call(
        functools.partial(_four_chan_kernel, gdim=gdim),
        grid=(batch, nb // 2),
        in_specs=[
            pl.BlockSpec((fr, f_width), lambda b, i: (b * nb + i, 0)),
            pl.BlockSpec((fr, f_width), lambda b, i: (b * nb + nb - 1 - i, 0)),
            pl.BlockSpec((ext, f_width),
                         lambda b, i: (jnp.minimum(((b + 1) * nb - i) * (fr // ext), last_ext), 0)),
            pl.BlockSpec((ext, f_width), lambda b, i: ((2 * b + 1) * (seq // 2 // ext), 0)),
            _resident(dft_c.shape),
            _resident(flip.shape),
        ],
        out_specs=[
            pl.BlockSpec((1, 2, fr, f_width), lambda b, i: (b, 0, i, 0)),
            pl.BlockSpec((1, ext, f_width), lambda b, i: (b, 0, 0)),
        ],
        out_shape=[
            jax.ShapeDtypeStruct((batch, 2, seq // 2, f_width), BF16),
            jax.ShapeDtypeStruct((batch, ext, f_width), F32),
        ],
        compiler_params=_params("parallel", "arbitrary"),
        name="fourier_chan",
    )(uv, uv, uv, uv, dft_c, flip)


def _four_seq_kernel(ch_ref, sh_ref, uc_ref, us_ref, mid_ref, flip_ref, o_ref, *, seq):
    fr = o_ref.shape[3]
    rows = ch_ref.shape[1]
    p = _dot(ch_ref[0], uc_ref[0, 0])
    q = _dot(sh_ref[0], us_ref[0, 0])
    k = pl.program_id(1) * fr + lax.broadcasted_iota(jnp.int32, (rows, 1), 0)
    p = p + jnp.where(k % 2 == 0, seq ** -0.5, -(seq ** -0.5)) * mid_ref[0, 0:1, :]
    o_ref[0, 0, 0] = (p - q)[:fr].astype(BF16)
    o_ref[0, 1, 0] = _dot(flip_ref[...], (p + q).astype(BF16)).astype(BF16)


def _four_seq(ch, sh, flip, ucs, mid, *, batch, seq, f_width):
    nt, rows, half = ch.shape
    fr = flip.shape[0]
    table_spec = pl.BlockSpec((1, rows, half), lambda b, t: (t, 0, 0))
    return pl.pallas_call(
        functools.partial(_four_seq_kernel, seq=seq),
        grid=(batch, nt),
        in_specs=[
            table_spec,
            table_spec,
            pl.BlockSpec((1, 1, half, f_width), lambda b, t: (b, 0, 0, 0)),
            pl.BlockSpec((1, 1, half, f_width), lambda b, t: (b, 1, 0, 0)),
            pl.BlockSpec((1, FOUR_EXTRA_ROWS, f_width), lambda b, t: (b, 0, 0)),
            _resident(flip.shape),
        ],
        out_specs=pl.BlockSpec((1, 2, 1, fr, f_width), lambda b, t: (b, 0, t, 0, 0)),
        out_shape=jax.ShapeDtypeStruct((batch, 2, nt, fr, f_width), BF16),
        compiler_params=_params("parallel", "arbitrary"),
        name="fourier_seq",
    )(ch, sh, ucs, ucs, mid, flip)


def _merge_kernel(att_ref, y_ref, g1_ref, g2_ref, x_ref, wa_ref, wf_ref, wm_ref, gp_ref, o_ref):
    a = _dot(att_ref[...], wa_ref[...])
    f = _dot(y_ref[...], wf_ref[...])
    merged = g1_ref[...].astype(F32) * a + g2_ref[...].astype(F32) * f
    y = _dot(merged.astype(BF16), wm_ref[...])
    o_ref[...] = x_ref[...] + _rms(y, gp_ref[...])


def _resident(shape):
    return pl.BlockSpec(shape, lambda *_: (0,) * len(shape), pipeline_mode=pl.Buffered(1))


def _layer_spec(block, index_map, layer, **kwargs):
    return pl.BlockSpec((None, *block), lambda *idx: (layer, *index_map(*idx)), **kwargs)


def _resident_layer(w, layer):
    return _layer_spec(w.shape[1:], lambda *_: (0,) * (w.ndim - 1), layer,
                       pipeline_mode=pl.Buffered(1))


def _merge(att, four, gates, x, wa, wf, wm, g_post, *, layer, seq, tm):
    t, d = x.shape
    tiles = seq // tm

    def four_block(i):
        it = i % tiles
        return (i - it + jnp.where(it < tiles // 2, it, tiles + tiles // 2 - 1 - it), 0)

    return pl.pallas_call(
        _merge_kernel,
        grid=(t // tm,),
        in_specs=[
            pl.BlockSpec((tm, att.shape[1]), lambda i: (i, 0)),
            pl.BlockSpec((tm, four.shape[1]), four_block),
            pl.BlockSpec((tm, d), lambda i: (i, 0)),
            pl.BlockSpec((tm, d), lambda i: (i, 1)),
            pl.BlockSpec((tm, d), lambda i: (i, 0)),
            _resident_layer(wa, layer),
            _resident_layer(wf, layer),
            _resident_layer(wm, layer),
            _resident(g_post.shape),
        ],
        out_specs=pl.BlockSpec((tm, d), lambda i: (i, 0)),
        out_shape=jax.ShapeDtypeStruct((t, d), F32),
        compiler_params=_params("parallel"),
        name="merge",
    )(att, four, gates, gates, x, wa, wf, wm, g_post)


def _norm_matmul_kernel(x_ref, g_ref, w_ref, o_ref):
    hn = _rms(x_ref[...], g_ref[...]).astype(BF16)
    o_ref[...] = _dot(hn, w_ref[...]).astype(BF16)


def _norm_matmul(x, g, w, *, layer, tm):
    t, d = x.shape
    n = w.shape[2]
    return pl.pallas_call(
        _norm_matmul_kernel,
        grid=(t // tm,),
        in_specs=[
            pl.BlockSpec((tm, d), lambda i: (i, 0)),
            _resident(g.shape),
            _resident_layer(w, layer),
        ],
        out_specs=pl.BlockSpec((tm, n), lambda i: (i, 0)),
        out_shape=jax.ShapeDtypeStruct((t, n), BF16),
        compiler_params=_params("parallel"),
        name="mem_kv",
    )(x, g, w)


def _xattn_kernel(x_ref, gpre_ref, wq_ref, k_ref, v_ref, wo_ref, gpost_ref, o_ref):
    x = x_ref[...]
    hn = _rms(x, gpre_ref[...]).astype(BF16)
    q = (_dot(hn, wq_ref[...]) * (X_HEAD_DIM ** -0.5)).astype(BF16)
    outs = []
    for h in range(X_HEADS):
        sl = slice(h * X_HEAD_DIM, (h + 1) * X_HEAD_DIM)
        s = _dot_nt(q[:, sl], k_ref[:, sl])
        m = jnp.max(s, axis=-1, keepdims=True)
        p = jnp.exp(s - m)
        l = jnp.sum(p, axis=-1, keepdims=True)
        outs.append(_dot(p.astype(BF16), v_ref[:, sl]) * (1.0 / l))
    o = jnp.concatenate(outs, axis=1).astype(BF16)
    y = _dot(o, wo_ref[...])
    o_ref[...] = x + _rms(y, gpost_ref[...])


def _xattn(x, g_pre, wq, kv, wo, g_post, *, layer, batch, seq, mem_tokens, tm):
    t, d = x.shape
    xw = wq.shape[2]
    ns = seq // tm
    return pl.pallas_call(
        _xattn_kernel,
        grid=(batch, ns),
        in_specs=[
            pl.BlockSpec((tm, d), lambda b, i: (b * ns + i, 0)),
            _resident(g_pre.shape),
            _resident_layer(wq, layer),
            pl.BlockSpec((mem_tokens, xw), lambda b, i: (b, 0)),
            pl.BlockSpec((mem_tokens, xw), lambda b, i: (b, 1)),
            _resident_layer(wo, layer),
            _resident(g_post.shape),
        ],
        out_specs=pl.BlockSpec((tm, d), lambda b, i: (b * ns + i, 0)),
        out_shape=jax.ShapeDtypeStruct((t, d), F32),
        compiler_params=_params("parallel", "parallel"),
        name="xattn",
    )(x, g_pre, wq, kv, kv, wo, g_post)


def _gelu_tanh(x):
    return 0.5 * x * (1.0 + jnp.tanh(math.sqrt(2.0 / math.pi) * (x + 0.044715 * (x * x * x))))


def _ffn_kernel(xp_ref, x_ref, xn_ref, gpre_ref, wg_ref, wv_ref, cwg_ref, cwv_ref, cbg_ref, cbv_ref,
                wd_ref, gpost_ref, o_ref, hn_ref, ug_ref, uv_ref, *, tiles_per_seq):
    i = pl.program_id(0)
    j = pl.program_id(1)
    tm = x_ref.shape[0]
    halo = xp_ref.shape[0]

    @pl.when(j == 0)
    def _():
        g = gpre_ref[...]
        pos = i % tiles_per_seq
        hp = jnp.where(pos == 0, 0.0, _rms(xp_ref[...], g))
        hx = jnp.where(pos == tiles_per_seq - 1, 0.0, _rms(xn_ref[...], g))
        hn_ref[0:halo, :] = hp.astype(BF16)
        hn_ref[halo + tm:, :] = hx.astype(BF16)
        for r in range(0, tm, FFN_NORM_ROWS):
            rows = slice(r, r + FFN_NORM_ROWS)
            hn_ref[halo + r:halo + r + FFN_NORM_ROWS, :] = _rms(x_ref[rows, :], g).astype(BF16)
        o_ref[...] = jnp.zeros_like(o_ref)

    hn = hn_ref[...]
    ug_ref[...] = _dot(hn, wg_ref[...])
    uv_ref[...] = _dot(hn, wv_ref[...])

    def conv(u_ref, cw_ref, cb_ref):
        return (u_ref[halo - 1:halo - 1 + tm, :] * cw_ref[0:1, :]
                + u_ref[halo:halo + tm, :] * cw_ref[1:2, :]
                + u_ref[halo + 1:halo + 1 + tm, :] * cw_ref[2:3, :]
                + cb_ref[...])

    act = _gelu_tanh(conv(ug_ref, cwg_ref, cbg_ref)) * conv(uv_ref, cwv_ref, cbv_ref)
    o_ref[...] += _dot(act.astype(BF16), wd_ref[...])

    @pl.when(j == pl.num_programs(1) - 1)
    def _():
        for r in range(0, tm, FFN_NORM_ROWS):
            rows = slice(r, r + FFN_NORM_ROWS)
            o_ref[rows, :] = x_ref[rows, :] + _rms(o_ref[rows, :], gpost_ref[...])


def _ffn(x, g_pre, w_up, conv_w, conv_b, w_down, g_post, *, layer, seq, tm, tf):
    t, d = x.shape
    d_ff = w_down.shape[1]
    nf = d_ff // tf
    halo = BF16_SUBLANES
    hb = tm // halo
    n_hblk = t // halo
    kern = functools.partial(_ffn_kernel, tiles_per_seq=seq // tm)
    return pl.pallas_call(
        kern,
        grid=(t // tm, nf),
        in_specs=[
            pl.BlockSpec((halo, d), lambda i, j: (jnp.maximum(i * hb - 1, 0), 0)),
            pl.BlockSpec((tm, d), lambda i, j: (i, 0)),
            pl.BlockSpec((halo, d), lambda i, j: (jnp.minimum((i + 1) * hb, n_hblk - 1), 0)),
            pl.BlockSpec((1, d), lambda i, j: (0, 0)),
            _layer_spec((d, tf), lambda i, j: (0, j), layer),
            _layer_spec((d, tf), lambda i, j: (0, nf + j), layer),
            pl.BlockSpec((3, tf), lambda i, j: (0, j)),
            pl.BlockSpec((3, tf), lambda i, j: (0, nf + j)),
            pl.BlockSpec((1, tf), lambda i, j: (0, j)),
            pl.BlockSpec((1, tf), lambda i, j: (0, nf + j)),
            _layer_spec((tf, d), lambda i, j: (j, 0), layer),
            pl.BlockSpec((1, d), lambda i, j: (0, 0)),
        ],
        out_specs=pl.BlockSpec((tm, d), lambda i, j: (i, 0)),
        out_shape=jax.ShapeDtypeStruct((t, d), F32),
        scratch_shapes=[
            pltpu.VMEM((tm + 2 * halo, d), BF16),
            pltpu.VMEM((tm + 2 * halo, tf), F32),
            pltpu.VMEM((tm + 2 * halo, tf), F32),
        ],
        compiler_params=_params("parallel", "arbitrary"),
        name="conv_ffn",
    )(x, x, x, g_pre, w_up, w_up, conv_w, conv_w, conv_b, conv_b, w_down, g_post)


def _head_lane_order():
    quarter = HEAD_DIM // 4
    blocks = (0, 2, 1, 3)
    return jnp.concatenate([jnp.arange(b * quarter, (b + 1) * quarter) for b in blocks])


def _rope_tables(seq):
    half = HEAD_DIM // 2
    pos = jnp.arange(seq)
    row = (pos // GRID_W).astype(F32)
    col = (pos % GRID_W).astype(F32)
    inv_freq = 1.0 / (ROPE_THETA ** (jnp.arange(0, half, 2, dtype=F32) / half))
    lane = _head_lane_order()
    freq = inv_freq[(lane % half) % (half // 2)]
    ids = jnp.where(lane[None, :] < half, row[:, None], col[:, None])
    ang = ids * freq[None, :]
    sign = jnp.where((lane % half) < half // 2, -1.0, 1.0)
    return jnp.cos(ang), jnp.sin(ang) * sign[None, :]


def _dft_tables(seq, gdim, fr):
    def cs(k, n, period):
        r = (k[..., None] * n) % period
        ang = r.astype(F32) * (2.0 * math.pi / period)
        return jnp.cos(ang), jnp.sin(ang)
    c = jnp.arange(gdim, dtype=jnp.int32)
    cc, sc = (t * gdim ** -0.5 for t in cs(c, c, gdim))
    rows = fr + FOUR_EXTRA_ROWS
    k = jnp.arange(seq // 2 // fr, dtype=jnp.int32)[:, None] * fr + jnp.arange(rows, dtype=jnp.int32)
    ca, sa = cs(k, GRID_W * jnp.arange(seq // 2 // GRID_W, dtype=jnp.int32), seq)
    cb, sb = cs(k, jnp.arange(GRID_W, dtype=jnp.int32), seq)
    scale = seq ** -0.5
    ch = ((ca[..., :, None] * cb[..., None, :] - sa[..., :, None] * sb[..., None, :]) * scale)
    sh = ((sa[..., :, None] * cb[..., None, :] + ca[..., :, None] * sb[..., None, :]) * scale)
    ch, sh = ch.reshape(*k.shape, seq // 2), sh.reshape(*k.shape, seq // 2)
    flip = (jnp.arange(rows)[None, :] == fr - jnp.arange(fr)[:, None]).astype(BF16)
    return (jnp.concatenate([cc, sc], axis=1).astype(BF16), ch.astype(BF16), sh.astype(BF16), flip)


def _tiles(seq, d_ff):
    def pick(n, pref):
        return pref if n % pref == 0 else n
    tf = next(c for c in (512, 256, 128) if d_ff % c == 0)
    return dict(
        qk_tm=pick(seq, 512), mixer_tm=pick(seq, 512), uv_tn=1536, gate_tn=4096,
        attn_tq=pick(seq, 512), attn_kc=pick(seq, 256),
        merge_tm=pick(seq, 256),
        xattn_tm=pick(seq, 512),
        ffn_tm=pick(seq, 512), ffn_tf=tf,
    )


def kernel(x, mem, mix_pre_g, w_in, q_norm_g, k_norm_g, w_attn_o, w_four_o, w_gate, b_gate, w_mix_o,
           mix_post_g, xa_pre_g, mem_norm_g, w_xq, w_xkv, w_xo, xa_post_g, ffn_pre_g, w_up, conv_w,
           conv_b, w_down, ffn_post_g):
    batch, seq, d = x.shape
    mem_tokens = mem.shape[1]
    depth = w_in.shape[0]
    q_width = w_attn_o.shape[1]
    f_width = w_four_o.shape[1]
    kv_width = (w_in.shape[2] - q_width - f_width) // 2
    n_heads = q_width // HEAD_DIM
    assert n_heads == N_KV * GROUP and kv_width == N_KV * HEAD_DIM
    assert seq % GRID_W == 0 and w_xq.shape[2] == X_HEADS * X_HEAD_DIM
    d_ff = w_down.shape[1]
    ts = _tiles(seq, d_ff)
    qk_width = q_width + kv_width
    u_col = q_width + 2 * kv_width
    tm = ts["mixer_tm"]
    assert (kv_width + f_width) % ts["uv_tn"] == 0 and w_gate.shape[2] % ts["gate_tn"] == 0

    cos_t, sin_t = _rope_tables(seq)
    assert seq % (2 * ts["merge_tm"]) == 0
    dft_c, dft_ch, dft_sh, dft_flip = _dft_tables(seq, f_width // N_FOURIER_GROUPS, ts["merge_tm"])
    order = _head_lane_order()
    partner = jnp.roll(jnp.arange(HEAD_DIM), HEAD_DIM // 2)
    bias_spec = pl.BlockSpec((1, ts["gate_tn"]), lambda i, j: (0, j))

    row = lambda v: v.reshape(1, -1)
    xf = x.reshape(batch * seq, d)
    memf = mem.reshape(batch * mem_tokens, d)

    quarter = HEAD_DIM // 4
    w_qk_t = w_in[:, :, :qk_width].astype(BF16).reshape(depth, d, qk_width // HEAD_DIM, 2, 2, quarter)
    w_qk_t = w_qk_t.transpose(0, 2, 4, 3, 5, 1).reshape(depth, qk_width, d)
    w_uv = jnp.concatenate([w_in[:, :, u_col:], w_in[:, :, qk_width:u_col]], axis=2).astype(BF16)
    w_gate_b, w_attn_o_b, w_four_o_b, w_mix_o_b, w_xq_b, w_xkv_b, w_xo_b, w_up_b, w_down_b = (
        w.astype(BF16) for w in (w_gate, w_attn_o, w_four_o, w_mix_o, w_xq, w_xkv, w_xo, w_up, w_down))

    for l in range(depth):
        q_gain = q_norm_g[l][order] * (HEAD_DIM ** -0.5 * math.log2(math.e))
        k_gain = k_norm_g[l][order]
        cg = jnp.concatenate([g[:, None] * cos_t.T for g in (q_gain, k_gain)], axis=0)
        sg = jnp.concatenate([g[partner][:, None] * sin_t.T for g in (q_gain, k_gain)], axis=0)
        g_pre = row(mix_pre_g[l])
        q, k_t = _proj_qk_t(xf, g_pre, w_qk_t, cg, sg, layer=l, seq=seq, n_q_heads=n_heads,
                            tm=ts["qk_tm"])
        q_bound = (-(HEAD_DIM ** 0.5) * jnp.max(jnp.abs(q_gain))) * (jnp.arange(HEAD_DIM) == 0)
        uv = _proj(xf, g_pre, w_uv, _epilogue_plain, (), (), layer=l, tm=tm, tn=ts["uv_tn"],
                   name="proj_uv")
        gates = _proj(xf, g_pre, w_gate_b, _epilogue_gate, (row(b_gate[l]),), (bias_spec,),
                      layer=l, tm=tm, tn=ts["gate_tn"], name="proj_gate")
        att = _attention(q, k_t, uv, row(q_bound.astype(F32)), batch=batch, seq=seq, q_width=q_width, v_col=f_width,
                         tq=ts["attn_tq"], kc=ts["attn_kc"])
        ucs, mid = _four_chan(uv, dft_c, dft_flip, batch=batch, seq=seq, f_width=f_width)
        four = _four_seq(dft_ch, dft_sh, dft_flip, ucs, mid, batch=batch, seq=seq, f_width=f_width)
        xf = _merge(att, four.reshape(batch * seq, f_width), gates, xf, w_attn_o_b, w_four_o_b,
                    w_mix_o_b, row(mix_post_g[l]), layer=l, seq=seq, tm=ts["merge_tm"])

        kv = _norm_matmul(memf, row(mem_norm_g[l]), w_xkv_b, layer=l, tm=mem_tokens)
        xf = _xattn(xf, row(xa_pre_g[l]), w_xq_b, kv, w_xo_b, row(xa_post_g[l]), layer=l,
                    batch=batch, seq=seq, mem_tokens=mem_tokens, tm=ts["xattn_tm"])

        xf = _ffn(xf, row(ffn_pre_g[l]), w_up_b, conv_w[l], row(conv_b[l]), w_down_b,
                  row(ffn_post_g[l]), layer=l, seq=seq, tm=ts["ffn_tm"], tf=ts["ffn_tf"])

    return xf.reshape(batch, seq, d)
```
